```python
import math
import jax
import jax.numpy as jnp
from jax import lax
import numpy as np

D_MODEL = 1024
BATCH = 2
SEQ = 8192
DEPTH = 2

N_MIXERS = 2
S5_WIDTH = D_MODEL
S5_GROUP = 16
S5_GROUPS = S5_WIDTH // S5_GROUP
S5_STATE = 64
S5_DT_MIN = 1e-3
S5_DT_MAX = 1e-1
N_HEADS = 8
HEAD_DIM = D_MODEL // N_HEADS
MOBA_BLOCK = 256
MOBA_TOPK = 3
Q_CHUNK = 32
ROPE_THETA = 10000.0
N_EXPERTS = 32
TOP_K = 4
D_FF = D_MODEL
SWIGLU_LIMIT = 7.0
SWIGLU_ALPHA = 1.702
EPS = 1e-6

kernel_name = "hybrid_s5_moba_moe_adaln"


def rms_norm(x, g):
    xf = x.astype(jnp.float32)
    y = xf * lax.rsqrt(jnp.mean(xf * xf, axis=-1, keepdims=True) + EPS)
    return y.astype(x.dtype) * g


def modulate(h, shift, scale):
    return h * (1.0 + scale[:, None, :]) + shift[:, None, :]


def rope_tables(positions):
    inv = ROPE_THETA ** (-jnp.arange(0, HEAD_DIM, 2, dtype=jnp.float32) / HEAD_DIM)
    ang = positions.astype(jnp.float32)[..., None] * inv
    ang = jnp.concatenate([ang, ang], axis=-1)
    return jnp.cos(ang), jnp.sin(ang)


def apply_rope(t, cos, sin):
    t1, t2 = jnp.split(t, 2, axis=-1)
    rot = jnp.concatenate([-t2, t1], axis=-1)
    return (t * cos[:, None] + rot * sin[:, None]).astype(t.dtype)


def _ssm_combine(e1, e2):
    a1r, a1i, b1r, b1i = e1
    a2r, a2i, b2r, b2i = e2
    return (a2r * a1r - a2i * a1i,
            a2r * a1i + a2i * a1r,
            a2r * b1r - a2i * b1i + b2r,
            a2r * b1i + a2i * b1r + b2i)


def s5_mixer(h, w_in, b_re, b_im, c_re, c_im, lam_re, lam_im, log_dt, d_skip, w_glu, w_out):
    bsz, seq, _ = h.shape
    u = h @ w_in
    ug = u.reshape(bsz, seq, S5_GROUPS, S5_GROUP)
    dt = jnp.exp(log_dt)[:, None]
    mag = jnp.exp(lam_re * dt)
    a_re = mag * jnp.cos(lam_im * dt)
    a_im = mag * jnp.sin(lam_im * dt)
    den = lam_re * lam_re + lam_im * lam_im
    f_re = ((a_re - 1.0) * lam_re + a_im * lam_im) / den
    f_im = (a_im * lam_re - (a_re - 1.0) * lam_im) / den
    bb_re = f_re[..., None] * b_re - f_im[..., None] * b_im
    bb_im = f_re[..., None] * b_im + f_im[..., None] * b_re
    bu_re = jnp.einsum('bsgc,gpc->bsgp', ug, bb_re)
    bu_im = jnp.einsum('bsgc,gpc->bsgp', ug, bb_im)
    ar = jnp.broadcast_to(a_re.astype(bu_re.dtype), bu_re.shape)
    ai = jnp.broadcast_to(a_im.astype(bu_re.dtype), bu_re.shape)
    _, _, x_re, x_im = lax.associative_scan(_ssm_combine, (ar, ai, bu_re, bu_im), axis=1)
    y = jnp.einsum('bsgp,gcp->bsgc', x_re, c_re) - jnp.einsum('bsgp,gcp->bsgc', x_im, c_im)
    y = y.reshape(bsz, seq, S5_WIDTH) + d_skip * u
    z = jax.nn.gelu(y)
    z = z * jax.nn.sigmoid(z @ w_glu)
    return z @ w_out


def moba_mixer(h, cos, sin, w_qkv, w_o):
    bsz, seq, _ = h.shape
    qkv = (h @ w_qkv).reshape(bsz, seq, 3, N_HEADS, HEAD_DIM)
    q = apply_rope(qkv[:, :, 0].transpose(0, 2, 1, 3), cos, sin)
    k = apply_rope(qkv[:, :, 1].transpose(0, 2, 1, 3), cos, sin)
    v = qkv[:, :, 2].transpose(0, 2, 1, 3)
    scale = HEAD_DIM ** -0.5
    n_blocks = -(-seq // MOBA_BLOCK)
    pad = n_blocks * MOBA_BLOCK - seq
    k_p = jnp.pad(k, ((0, 0), (0, 0), (0, pad), (0, 0)))
    v_p = jnp.pad(v, ((0, 0), (0, 0), (0, pad), (0, 0)))
    k_blocks = k_p.reshape(bsz, N_HEADS, n_blocks, MOBA_BLOCK, HEAD_DIM)
    v_blocks = v_p.reshape(bsz, N_HEADS, n_blocks, MOBA_BLOCK, HEAD_DIM)
    k_mean = jnp.mean(k_blocks, axis=3)
    n_sel = min(MOBA_TOPK, n_blocks)
    bi = jnp.arange(bsz)[:, None, None, None]
    hi = jnp.arange(N_HEADS)[None, :, None, None]
    blk_ids = jnp.arange(n_blocks)
    n_chunks = seq // Q_CHUNK

    def chunk(ci):
        q0 = ci * Q_CHUNK
        qc = lax.dynamic_slice_in_dim(q, q0, Q_CHUNK, axis=2)
        j = q0 // MOBA_BLOCK
        qpos = q0 + jnp.arange(Q_CHUNK)
        gate_s = jnp.einsum('bhqd,bhnd->bhqn', qc, k_mean).astype(jnp.float32)
        gate_s = jnp.where(blk_ids < j, gate_s, -jnp.inf)
        _, idx = lax.top_k(gate_s, n_sel)
        valid = idx < j
        kg = k_blocks[bi, hi, idx]
        vg = v_blocks[bi, hi, idx]
        s_sel = jnp.einsum('bhqd,bhqnkd->bhqnk', qc, kg).astype(jnp.float32) * scale
        s_sel = jnp.where(valid[..., None], s_sel, -jnp.inf)
        s_sel = s_sel.reshape(bsz, N_HEADS, Q_CHUNK, n_sel * MOBA_BLOCK)
        k_own = lax.dynamic_slice_in_dim(k_p, j * MOBA_BLOCK, MOBA_BLOCK, axis=2)
        v_own = lax.dynamic_slice_in_dim(v_p, j * MOBA_BLOCK, MOBA_BLOCK, axis=2)
        s_own = jnp.einsum('bhqd,bhkd->bhqk', qc, k_own).astype(jnp.float32) * scale
        kpos = j * MOBA_BLOCK + jnp.arange(MOBA_BLOCK)
        s_own = jnp.where(kpos[None, :] <= qpos[:, None], s_own, -jnp.inf)
        p = jax.nn.softmax(jnp.concatenate([s_sel, s_own], axis=-1), axis=-1).astype(v.dtype)
        p_sel = p[..., :n_sel * MOBA_BLOCK].reshape(bsz, N_HEADS, Q_CHUNK, n_sel, MOBA_BLOCK)
        p_own = p[..., n_sel * MOBA_BLOCK:]
        return (jnp.einsum('bhqnk,bhqnkd->bhqd', p_sel, vg)
                + jnp.einsum('bhqk,bhkd->bhqd', p_own, v_own))

    o = lax.map(chunk, jnp.arange(n_chunks))
    o = o.transpose(1, 0, 3, 2, 4).reshape(bsz, seq, N_HEADS * HEAD_DIM)
    return o @ w_o


def moe(h, w_router, b_router, w_gate_up, b_gate_up, w_down, b_down):
    bsz, seq, d = h.shape
    xf = h.reshape(-1, d)
    t = xf.shape[0]
    logits = (xf @ w_router + b_router).astype(jnp.float32)
    top_v, top_i = lax.top_k(logits, TOP_K)
    gates = jax.nn.softmax(top_v, axis=-1)
    flat_e = top_i.reshape(-1)
    order = jnp.argsort(flat_e)
    e_sorted = flat_e[order]
    tok = order // TOP_K
    xs = xf[tok]
    sizes = jnp.bincount(flat_e, length=N_EXPERTS).astype(jnp.int32)
    hu = lax.ragged_dot(xs, w_gate_up, sizes) + b_gate_up[e_sorted]
    g = jnp.minimum(hu[:, :D_FF], SWIGLU_LIMIT)
    up = jnp.clip(hu[:, D_FF:], -SWIGLU_LIMIT, SWIGLU_LIMIT)
    act = (up + 1.0) * g * jax.nn.sigmoid(SWIGLU_ALPHA * g)
    out = lax.ragged_dot(act, w_down, sizes) + b_down[e_sorted]
    wts = gates.reshape(-1)[order].astype(out.dtype)
    y = jnp.zeros((t, d), out.dtype).at[tok].add(out * wts[:, None])
    return y.reshape(bsz, seq, d)


def setup_inputs(seed: int = 0) -> dict:
    key = jax.random.key(seed)
    ks = iter(jax.random.split(key, 64))

    def nrm(shape, std):
        return std * jax.random.normal(next(ks), shape, jnp.float32)

    p = {}
    p['x'] = nrm((BATCH, SEQ, D_MODEL), 1.0)
    p['c'] = nrm((BATCH, D_MODEL), 1.0)
    offset = jax.random.randint(next(ks), (BATCH, 1), 0, 4096, dtype=jnp.int32)
    p['positions'] = jnp.arange(SEQ, dtype=jnp.int32)[None, :] + offset
    for i in range(DEPTH):
        pre = 'l%d_' % i
        p[pre + 'norm1_g'] = 1.0 + nrm((D_MODEL,), 0.02)
        p[pre + 'ada_w'] = nrm((D_MODEL, 6 * D_MODEL), 0.02)
        p[pre + 'ada_b'] = nrm((6 * D_MODEL,), 0.01)
        if i % N_MIXERS == 0:
            p[pre + 's5_w_in'] = nrm((D_MODEL, S5_WIDTH), D_MODEL ** -0.5)
            p[pre + 's5_b_re'] = nrm((S5_GROUPS, S5_STATE, S5_GROUP), (2 * S5_GROUP) ** -0.5)
            p[pre + 's5_b_im'] = nrm((S5_GROUPS, S5_STATE, S5_GROUP), (2 * S5_GROUP) ** -0.5)
            p[pre + 's5_c_re'] = nrm((S5_GROUPS, S5_GROUP, S5_STATE), S5_STATE ** -0.5)
            p[pre + 's5_c_im'] = nrm((S5_GROUPS, S5_GROUP, S5_STATE), S5_STATE ** -0.5)
            p[pre + 's5_lam_re'] = -0.5 + nrm((S5_GROUPS, S5_STATE), 0.01)
            p[pre + 's5_lam_im'] = (math.pi * jnp.arange(S5_STATE, dtype=jnp.float32))[None, :] + nrm((S5_GROUPS, S5_STATE), 0.01)
            p[pre + 's5_log_dt'] = jax.random.uniform(next(ks), (S5_GROUPS,), jnp.float32, math.log(S5_DT_MIN), math.log(S5_DT_MAX))
            p[pre + 's5_d'] = nrm((S5_WIDTH,), 1.0)
            p[pre + 's5_w_glu'] = nrm((S5_WIDTH, S5_WIDTH), S5_WIDTH ** -0.5)
            p[pre + 's5_w_out'] = nrm((S5_WIDTH, D_MODEL), S5_WIDTH ** -0.5)
        else:
            p[pre + 'moba_w_qkv'] = nrm((D_MODEL, 3 * N_HEADS * HEAD_DIM), D_MODEL ** -0.5)
            p[pre + 'moba_w_o'] = nrm((N_HEADS * HEAD_DIM, D_MODEL), (N_HEADS * HEAD_DIM) ** -0.5)
        p[pre + 'norm2_g'] = 1.0 + nrm((D_MODEL,), 0.02)
        p[pre + 'moe_w_router'] = nrm((D_MODEL, N_EXPERTS), D_MODEL ** -0.5)
        p[pre + 'moe_b_router'] = nrm((N_EXPERTS,), 0.01)
        p[pre + 'moe_w_gate_up'] = nrm((N_EXPERTS, D_MODEL, 2 * D_FF), D_MODEL ** -0.5)
        p[pre + 'moe_b_gate_up'] = nrm((N_EXPERTS, 2 * D_FF), 0.01)
        p[pre + 'moe_w_down'] = nrm((N_EXPERTS, D_FF, D_MODEL), D_FF ** -0.5)
        p[pre + 'moe_b_down'] = nrm((N_EXPERTS, D_MODEL), 0.01)
    p['final_norm_g'] = 1.0 + nrm((D_MODEL,), 0.02)
    return p


def reference(x, c, positions,
              l0_norm1_g, l0_ada_w, l0_ada_b,
              l0_s5_w_in, l0_s5_b_re, l0_s5_b_im, l0_s5_c_re, l0_s5_c_im,
              l0_s5_lam_re, l0_s5_lam_im, l0_s5_log_dt, l0_s5_d, l0_s5_w_glu, l0_s5_w_out,
              l0_norm2_g, l0_moe_w_router, l0_moe_b_router, l0_moe_w_gate_up, l0_moe_b_gate_up,
              l0_moe_w_down, l0_moe_b_down,
              l1_norm1_g, l1_ada_w, l1_ada_b,
              l1_moba_w_qkv, l1_moba_w_o,
              l1_norm2_g, l1_moe_w_router, l1_moe_b_router, l1_moe_w_gate_up, l1_moe_b_gate_up,
              l1_moe_w_down, l1_moe_b_down,
              final_norm_g):
    cos, sin = rope_tables(positions)
    c_act = jax.nn.silu(c)
    layers = (
        (l0_norm1_g, l0_ada_w, l0_ada_b,
         (l0_s5_w_in, l0_s5_b_re, l0_s5_b_im, l0_s5_c_re, l0_s5_c_im,
          l0_s5_lam_re, l0_s5_lam_im, l0_s5_log_dt, l0_s5_d, l0_s5_w_glu, l0_s5_w_out),
         l0_norm2_g,
         (l0_moe_w_router, l0_moe_b_router, l0_moe_w_gate_up, l0_moe_b_gate_up, l0_moe_w_down, l0_moe_b_down)),
        (l1_norm1_g, l1_ada_w, l1_ada_b,
         (l1_moba_w_qkv, l1_moba_w_o),
         l1_norm2_g,
         (l1_moe_w_router, l1_moe_b_router, l1_moe_w_gate_up, l1_moe_b_gate_up, l1_moe_w_down, l1_moe_b_down)),
    )
    for i in range(DEPTH):
        norm1_g, ada_w, ada_b, mix_p, norm2_g, moe_p = layers[i]
        ada = c_act @ ada_w + ada_b
        sh1, sc1, g1, sh2, sc2, g2 = jnp.split(ada, 6, axis=-1)
        h = modulate(rms_norm(x, norm1_g), sh1, sc1)
        if i % N_MIXERS == 0:
            m = s5_mixer(h, *mix_p)
        else:
            m = moba_mixer(h, cos, sin, *mix_p)
        x = x + g1[:, None, :] * m
        h = modulate(rms_norm(x, norm2_g), sh2, sc2)
        x = x + g2[:, None, :] * moe(h, *moe_p)
    return rms_norm(x, final_norm_g)
```

```python
import functools
import math

import jax
import jax.numpy as jnp
from jax import lax
from jax.experimental import pallas as pl
from jax.experimental.pallas import tpu as pltpu

F32 = jnp.float32
BF16 = jnp.bfloat16
HI = lax.Precision.HIGHEST

EPS = 1e-6
NEG = -1e30
LANES = 128
MIB = 1024 * 1024

S5_GROUP = 16
S5_STATE = 64
N_HEADS = 8
HEAD_DIM = 128
MOBA_BLOCK = 256
MOBA_TOPK = 3
ROPE_THETA = 10000.0
N_EXPERTS = 32
TOP_K = 4
SWIGLU_LIMIT = 7.0
SWIGLU_ALPHA = 1.702

TM_DENSE = 512
S5_CHUNK = 128
S5_SUPER = 8
TM_EXPERT = 256
TD_ROWS = 128


def _cparams(n_axes, vmem_mib):
    return pltpu.CompilerParams(dimension_semantics=("arbitrary",) * n_axes,
                                vmem_limit_bytes=vmem_mib * MIB)


def _normmod(x, g, sc, sh):
    ms = jnp.mean(x * x, axis=-1, keepdims=True)
    return (x * lax.rsqrt(ms + EPS)) * g * (1.0 + sc) + sh


def _ada_kernel(c_ref, w_ref, b_ref, o_ref):
    c = c_ref[...]
    ca = c * jax.nn.sigmoid(c)
    o_ref[...] = jnp.dot(ca, w_ref[...], preferred_element_type=F32, precision=HI) + b_ref[...]


def _ada(c_pad, w, b):
    d, n = w.shape
    tn = 1536
    return pl.pallas_call(
        _ada_kernel,
        grid=(n // tn,),
        in_specs=[pl.BlockSpec((8, d), lambda j: (0, 0)),
                  pl.BlockSpec((d, tn), lambda j: (0, j)),
                  pl.BlockSpec((1, tn), lambda j: (0, j))],
        out_specs=pl.BlockSpec((8, tn), lambda j: (0, j)),
        out_shape=jax.ShapeDtypeStruct((8, n), F32),
        compiler_params=_cparams(1, 32),
    )(c_pad, w, b.reshape(1, n))


def _nml_kernel(x_ref, g_ref, sc_ref, sh_ref, w_ref, o_ref):
    h = _normmod(x_ref[...], g_ref[...], sc_ref[0], sh_ref[0])
    o_ref[...] = jnp.dot(h.astype(BF16), w_ref[...], preferred_element_type=F32)


def _nml(x2, g, sc, sh, w_bf, seq):
    t, d = x2.shape
    n = w_bf.shape[1]
    tm = TM_DENSE
    tpb = seq // tm
    return pl.pallas_call(
        _nml_kernel,
        grid=(t // tm,),
        in_specs=[pl.BlockSpec((tm, d), lambda i: (i, 0)),
                  pl.BlockSpec((1, d), lambda i: (0, 0)),
                  pl.BlockSpec((1, 1, d), lambda i: (i // tpb, 0, 0)),
                  pl.BlockSpec((1, 1, d), lambda i: (i // tpb, 0, 0)),
                  pl.BlockSpec((d, n), lambda i: (0, 0))],
        out_specs=pl.BlockSpec((tm, n), lambda i: (i, 0)),
        out_shape=jax.ShapeDtypeStruct((t, n), F32),
        compiler_params=_cparams(1, 48),
    )(x2, g.reshape(1, d), sc, sh, w_bf)


def _gelu_tanh(y):
    c = math.sqrt(2.0 / math.pi)
    return y * (0.5 * (1.0 + jnp.tanh(c * (y + 0.044715 * (y * y * y)))))


def _s5_kernel(u_ref, rate_ref, theta_ref, bblk_ref, cblk_ref, d_ref, z_ref,
               apr, api, air, aii, xr, xi):
    L = u_ref.shape[0]
    n_super = bblk_ref.shape[0]
    half = apr.shape[1] // n_super
    nch = u_ref.shape[1] // n_super
    b = pl.program_id(0)
    c = pl.program_id(1)

    @pl.when((b == 0) & (c == 0))
    def _():
        tt = lax.broadcasted_iota(jnp.int32, (L, 1), 0).astype(F32)
        ph = theta_ref[...] * tt
        rt = rate_ref[...] * tt
        cs = jnp.cos(ph)
        sn = jnp.sin(ph)
        mag = jnp.exp(rt)
        inv = jnp.exp(-rt)
        apr[...] = mag * cs
        api[...] = mag * sn
        air[...] = inv * cs
        aii[...] = -(inv * sn)

    @pl.when(c == 0)
    def _():
        xr[...] = jnp.zeros_like(xr)
        xi[...] = jnp.zeros_like(xi)

    row = lax.broadcasted_iota(jnp.int32, (L, L), 0)
    col = lax.broadcasted_iota(jnp.int32, (L, L), 1)
    tril = (row >= col).astype(BF16)

    for sg in range(n_super):
        cols = slice(sg * half, (sg + 1) * half)
        ch = slice(sg * nch, (sg + 1) * nch)
        u = u_ref[:, ch]
        bu = jnp.dot(u.astype(BF16), bblk_ref[sg], preferred_element_type=F32)
        bur = bu[:, :half]
        bui = bu[:, half:]
        ir = air[:, cols]
        ii = aii[:, cols]
        zc = jnp.concatenate([bur * ir - bui * ii, bur * ii + bui * ir], axis=1)
        cum = jnp.dot(tril, zc.astype(BF16), preferred_element_type=F32)
        a_r = apr[1:2, cols]
        a_i = api[1:2, cols]
        p_r = xr[:, cols]
        p_i = xi[:, cols]
        cr = cum[:, :half] + (a_r * p_r - a_i * p_i)
        ci = cum[:, half:] + (a_r * p_i + a_i * p_r)
        pw_r = apr[:, cols]
        pw_i = api[:, cols]
        x_re = cr * pw_r - ci * pw_i
        x_im = cr * pw_i + ci * pw_r
        xr[:, cols] = x_re[L - 1:L, :]
        xi[:, cols] = x_im[L - 1:L, :]
        xc = jnp.concatenate([x_re, x_im], axis=1).astype(BF16)
        y = jnp.dot(xc, cblk_ref[sg], preferred_element_type=F32) + d_ref[:, ch] * u
        z_ref[:, ch] = _gelu_tanh(y)


def _s5_scan(u, rate, theta, bblk, cblk, d_skip, bsz, seq):
    t, w = u.shape
    L = S5_CHUNK
    nc = seq // L
    ns = rate.shape[1]
    return pl.pallas_call(
        _s5_kernel,
        grid=(bsz, nc),
        in_specs=[pl.BlockSpec((L, w), lambda b, c: (b * nc + c, 0)),
                  pl.BlockSpec((1, ns), lambda b, c: (0, 0)),
                  pl.BlockSpec((1, ns), lambda b, c: (0, 0)),
                  pl.BlockSpec(bblk.shape, lambda b, c: (0, 0, 0)),
                  pl.BlockSpec(cblk.shape, lambda b, c: (0, 0, 0)),
                  pl.BlockSpec((1, w), lambda b, c: (0, 0))],
        out_specs=pl.BlockSpec((L, w), lambda b, c: (b * nc + c, 0)),
        out_shape=jax.ShapeDtypeStruct((t, w), F32),
        scratch_shapes=[pltpu.VMEM((L, ns), F32)] * 4 + [pltpu.VMEM((1, ns), F32)] * 2,
        compiler_params=_cparams(2, 48),
    )(u, rate, theta, bblk, cblk, d_skip.reshape(1, w))


def _s5_params(b_re, b_im, c_re, c_im, lam_re, lam_im, log_dt):
    g, p, ch = b_re.shape
    dt = jnp.exp(log_dt)[:, None]
    rate = lam_re * dt
    theta = lam_im * dt
    mag = jnp.exp(rate)
    a_re = mag * jnp.cos(theta)
    a_im = mag * jnp.sin(theta)
    den = lam_re * lam_re + lam_im * lam_im
    f_re = ((a_re - 1.0) * lam_re + a_im * lam_im) / den
    f_im = (a_im * lam_re - (a_re - 1.0) * lam_im) / den
    bb_re = f_re[..., None] * b_re - f_im[..., None] * b_im
    bb_im = f_re[..., None] * b_im + f_im[..., None] * b_re
    ns = S5_SUPER
    nsg = g // ns
    eye = jnp.eye(ns, dtype=F32)

    def blk_b(bb):
        return jnp.einsum('sgpc,gh->sgchp', bb.reshape(nsg, ns, p, ch), eye).reshape(nsg, ns * ch, ns * p)

    def blk_c(cc):
        return jnp.einsum('sgcp,gh->sgphc', cc.reshape(nsg, ns, ch, p), eye).reshape(nsg, ns * p, ns * ch)

    bblk = jnp.concatenate([blk_b(bb_re), blk_b(bb_im)], axis=2).astype(BF16)
    cblk = jnp.concatenate([blk_c(c_re), -blk_c(c_im)], axis=1).astype(BF16)
    return rate.reshape(1, g * p), theta.reshape(1, g * p), bblk, cblk


def _glu_out_kernel(z_ref, x_ref, g_ref, wg_ref, wo_ref, o_ref):
    z = z_ref[...]
    gate = jax.nn.sigmoid(jnp.dot(z.astype(BF16), wg_ref[...], preferred_element_type=F32))
    m = jnp.dot((z * gate).astype(BF16), wo_ref[...], preferred_element_type=F32)
    o_ref[...] = x_ref[...] + g_ref[0] * m


def _lin_res_kernel(a_ref, x_ref, g_ref, w_ref, o_ref):
    m = jnp.dot(a_ref[...].astype(BF16), w_ref[...], preferred_element_type=F32)
    o_ref[...] = x_ref[...] + g_ref[0] * m


def _row_call(kernel, acts, x2, gate, weights, seq):
    t, d = x2.shape
    tm = TM_DENSE
    tpb = seq // tm
    row = pl.BlockSpec((tm, d), lambda i: (i, 0))
    return pl.pallas_call(
        kernel,
        grid=(t // tm,),
        in_specs=[row, row, pl.BlockSpec((1, 1, d), lambda i: (i // tpb, 0, 0))]
                 + [pl.BlockSpec(w.shape, lambda i: (0, 0)) for w in weights],
        out_specs=row,
        out_shape=jax.ShapeDtypeStruct((t, d), F32),
        compiler_params=_cparams(1, 48),
    )(acts, x2, gate, *weights)


def _rope_tab_kernel(pos_ref, inv_ref, sgn_ref, cos_ref, sin_ref):
    ang = pos_ref[...].astype(F32) * inv_ref[...]
    cos_ref[...] = jnp.cos(ang)
    sin_ref[...] = jnp.sin(ang) * sgn_ref[...]


def _rope_tables(pos_col):
    t = pos_col.shape[0]
    ts = 512
    half = HEAD_DIM // 2
    inv = ROPE_THETA ** (-jnp.arange(0, HEAD_DIM, 2, dtype=F32) / HEAD_DIM)
    inv = jnp.concatenate([inv, inv]).reshape(1, HEAD_DIM)
    sgn = jnp.concatenate([-jnp.ones((half,), F32), jnp.ones((half,), F32)]).reshape(1, HEAD_DIM)
    const = pl.BlockSpec((1, HEAD_DIM), lambda i: (0, 0))
    tile = pl.BlockSpec((ts, HEAD_DIM), lambda i: (i, 0))
    return pl.pallas_call(
        _rope_tab_kernel,
        grid=(t // ts,),
        in_specs=[pl.BlockSpec((ts, 1), lambda i: (i, 0)), const, const],
        out_specs=[tile, tile],
        out_shape=[jax.ShapeDtypeStruct((t, HEAD_DIM), F32)] * 2,
        compiler_params=_cparams(1, 16),
    )(pos_col, inv, sgn)


def _rope_kernel(q_ref, k_ref, v_ref, cos_ref, sin_ref, qo_ref, ko_ref, vo_ref, km_ref):
    cs = cos_ref[...]
    sn = sin_ref[...]
    q = q_ref[...]
    k = k_ref[...]
    half = q.shape[1] // 2
    qr = q * cs + pltpu.roll(q, half, 1) * sn
    kr = k * cs + pltpu.roll(k, half, 1) * sn
    qo_ref[0, 0] = qr
    ko_ref[0, 0] = kr.astype(BF16)
    vo_ref[0, 0] = v_ref[...].astype(BF16)
    km_ref[0, 0, 0] = jnp.mean(kr, axis=0, keepdims=True)


def _rope(qkv, cos, sin, bsz, seq):
    t = qkv.shape[0]
    h, dh, blk = N_HEADS, HEAD_DIM, MOBA_BLOCK
    nb = seq // blk
    hs = lambda off: pl.BlockSpec((blk, dh), lambda i, j: (i, off + j))
    tab = pl.BlockSpec((blk, dh), lambda i, j: (i, 0))
    head = pl.BlockSpec((1, 1, blk, dh), lambda i, j: (i // nb, j, i % nb, 0))
    return pl.pallas_call(
        _rope_kernel,
        grid=(t // blk, h),
        in_specs=[hs(0), hs(h), hs(2 * h), tab, tab],
        out_specs=[head, head, head,
                   pl.BlockSpec((1, 1, 1, 1, dh), lambda i, j: (i // nb, j, i % nb, 0, 0))],
        out_shape=[jax.ShapeDtypeStruct((bsz, h, seq, dh), F32),
                   jax.ShapeDtypeStruct((bsz, h, seq, dh), BF16),
                   jax.ShapeDtypeStruct((bsz, h, seq, dh), BF16),
                   jax.ShapeDtypeStruct((bsz, h, nb, 1, dh), F32)],
        compiler_params=_cparams(2, 16),
    )(qkv, qkv, qkv, cos, sin)


def _moba_kernel(q_ref, k_ref, v_ref, km_ref, o_ref):
    blk = q_ref.shape[2]
    j = pl.program_id(2)
    q = q_ref[0, 0]
    nt = (((1,), (1,)), ((), ()))
    gate = lax.dot_general(q, km_ref[0, 0], nt, preferred_element_type=F32, precision=HI)
    lane = lax.broadcasted_iota(jnp.int32, gate.shape, 1)
    lane_f = lane.astype(F32)
    work = jnp.where(lane < j, gate, NEG)
    sel = jnp.zeros(gate.shape, F32)
    for _ in range(MOBA_TOPK):
        m = jnp.max(work, axis=-1, keepdims=True)
        idx = jnp.min(jnp.where(work == m, lane_f, float(LANES)), axis=-1, keepdims=True)
        hit = lane_f == idx
        sel = jnp.where(hit & (m > 0.5 * NEG), 1.0, sel)
        work = jnp.where(hit, 2.0 * NEG, work)

    scale = q.shape[1] ** -0.5
    qb = q.astype(BF16)

    def attend(s, vb, m_i, l_i, acc):
        m_new = jnp.maximum(m_i, jnp.max(s, axis=-1, keepdims=True))
        alpha = jnp.exp(m_i - m_new)
        p = jnp.exp(s - m_new)
        l_new = alpha * l_i + jnp.sum(p, axis=-1, keepdims=True)
        acc_new = alpha * acc + jnp.dot(p.astype(BF16), vb, preferred_element_type=F32)
        return m_new, l_new, acc_new

    def body(n, carry):
        m_i, l_i, acc = carry
        start = pl.multiple_of(n * blk, blk)
        kb = k_ref[0, 0, pl.ds(start, blk), :]
        vb = v_ref[0, 0, pl.ds(start, blk), :]
        s = lax.dot_general(qb, kb, nt, preferred_element_type=F32) * scale
        sel_n = jnp.sum(jnp.where(lane == n, sel, 0.0), axis=-1, keepdims=True)
        s = jnp.where(sel_n > 0.5, s, NEG)
        return attend(s, vb, m_i, l_i, acc)

    init = (jnp.full((blk, 1), NEG, F32), jnp.zeros((blk, 1), F32), jnp.zeros(q.shape, F32))
    m_i, l_i, acc = lax.fori_loop(0, j, body, init)

    start = pl.multiple_of(j * blk, blk)
    kb = k_ref[0, 0, pl.ds(start, blk), :]
    vb = v_ref[0, 0, pl.ds(start, blk), :]
    s = lax.dot_general(qb, kb, nt, preferred_element_type=F32) * scale
    r = lax.broadcasted_iota(jnp.int32, s.shape, 0)
    cc = lax.broadcasted_iota(jnp.int32, s.shape, 1)
    s = jnp.where(cc <= r, s, NEG)
    m_i, l_i, acc = attend(s, vb, m_i, l_i, acc)
    o_ref[...] = acc / l_i


def _moba(q, k, v, km_pad, bsz, seq):
    h, dh, blk = N_HEADS, HEAD_DIM, MOBA_BLOCK
    nb = seq // blk
    kv = pl.BlockSpec((1, 1, seq, dh), lambda b, j, i: (b, j, 0, 0))
    return pl.pallas_call(
        _moba_kernel,
        grid=(bsz, h, nb),
        in_specs=[pl.BlockSpec((1, 1, blk, dh), lambda b, j, i: (b, j, i, 0)), kv, kv,
                  pl.BlockSpec((1, 1, LANES, dh), lambda b, j, i: (b, j, 0, 0))],
        out_specs=pl.BlockSpec((blk, dh), lambda b, j, i: (b * nb + i, j)),
        out_shape=jax.ShapeDtypeStruct((bsz * seq, h * dh), F32),
        compiler_params=_cparams(3, 32),
    )(q, k, v, km_pad)


def _router_kernel(x_ref, g_ref, sc_ref, sh_ref, wr_ref, br_ref, h_ref, oh_ref, ei_ref, gt_ref):
    h = _normmod(x_ref[...], g_ref[...], sc_ref[0], sh_ref[0])
    h_ref[...] = h
    logits = jnp.dot(h, wr_ref[...], preferred_element_type=F32, precision=HI) + br_ref[...]
    lane = lax.broadcasted_iota(jnp.int32, logits.shape, 1)
    lane_f = lane.astype(F32)
    work = logits
    onehot = jnp.zeros(logits.shape, F32)
    ei = jnp.zeros(logits.shape, F32)
    vals = []
    for kk in range(TOP_K):
        m = jnp.max(work, axis=-1, keepdims=True)
        idx = jnp.min(jnp.where(work == m, lane_f, float(LANES)), axis=-1, keepdims=True)
        hit = lane_f == idx
        onehot = jnp.where(hit, 1.0, onehot)
        ei = jnp.where(lane == kk, idx, ei)
        vals.append(m)
        work = jnp.where(hit, 2.0 * NEG, work)
    ex = [jnp.exp(vv - vals[0]) for vv in vals]
    tot = ex[0] + ex[1] + ex[2] + ex[3]
    gt = jnp.zeros(logits.shape, F32)
    for kk in range(TOP_K):
        gt = jnp.where(lane == kk, ex[kk] / tot, gt)
    oh_ref[...] = onehot
    ei_ref[...] = ei
    gt_ref[...] = gt


def _router(x2, g, sc, sh, wr_pad, br_pad, seq):
    t, d = x2.shape
    tm = TM_DENSE
    tpb = seq // tm
    lane_tile = pl.BlockSpec((tm, LANES), lambda i: (i, 0))
    return pl.pallas_call(
        _router_kernel,
        grid=(t // tm,),
        in_specs=[pl.BlockSpec((tm, d), lambda i: (i, 0)),
                  pl.BlockSpec((1, d), lambda i: (0, 0)),
                  pl.BlockSpec((1, 1, d), lambda i: (i // tpb, 0, 0)),
                  pl.BlockSpec((1, 1, d), lambda i: (i // tpb, 0, 0)),
                  pl.BlockSpec((d, LANES), lambda i: (0, 0)),
                  pl.BlockSpec((1, LANES), lambda i: (0, 0))],
        out_specs=[pl.BlockSpec((tm, d), lambda i: (i, 0)), lane_tile, lane_tile, lane_tile],
        out_shape=[jax.ShapeDtypeStruct((t, d), F32)] + [jax.ShapeDtypeStruct((t, LANES), F32)] * 3,
        compiler_params=_cparams(1, 32),
    )(x2, g.reshape(1, d), sc, sh, wr_pad, br_pad)


def _pos_kernel(oh_ref, ei_ref, dest_ref, meta_ref, cnt_scr, off_scr, run_scr):
    p = pl.program_id(0)
    i = pl.program_id(1)
    tm = oh_ref.shape[0]
    oh = oh_ref[...]
    colsum = jnp.sum(oh, axis=0, keepdims=True)

    @pl.when((p == 0) & (i == 0))
    def _():
        cnt_scr[...] = jnp.zeros_like(cnt_scr)

    @pl.when(p == 0)
    def _():
        cnt_scr[...] += colsum
        dest_ref[...] = jnp.zeros_like(dest_ref)

    @pl.when((p == 1) & (i == 0))
    def _():
        cnt = cnt_scr[...]
        padded = jnp.ceil(cnt * (1.0 / TM_EXPERT)) * float(TM_EXPERT)
        r = lax.broadcasted_iota(jnp.int32, (LANES, LANES), 0)
        cidx = lax.broadcasted_iota(jnp.int32, (LANES, LANES), 1)
        tri = (r < cidx).astype(F32)
        off = jnp.dot(jnp.broadcast_to(padded, (8, LANES)), tri,
                      preferred_element_type=F32, precision=HI)
        off_scr[...] = off[0:1]
        run_scr[...] = jnp.zeros_like(run_scr)

    @pl.when(p == 1)
    def _():
        r = lax.broadcasted_iota(jnp.int32, (tm, tm), 0)
        cidx = lax.broadcasted_iota(jnp.int32, (tm, tm), 1)
        ltri = (r > cidx).astype(BF16)
        excl = jnp.dot(ltri, oh.astype(BF16), preferred_element_type=F32)
        dfull = off_scr[...] + run_scr[...] + excl
        ei = ei_ref[...]
        lane = lax.broadcasted_iota(jnp.int32, oh.shape, 1)
        lane_f = lane.astype(F32)
        dest = jnp.zeros(oh.shape, F32)
        for kk in range(TOP_K):
            ek = jnp.sum(jnp.where(lane == kk, ei, 0.0), axis=-1, keepdims=True)
            dk = jnp.sum(jnp.where(lane_f == ek, dfull, 0.0), axis=-1, keepdims=True)
            dest = jnp.where(lane == kk, dk, dest)
        dest_ref[...] = dest.astype(jnp.int32)
        run_scr[...] += colsum

    meta_ref[...] = jnp.concatenate([off_scr[...], cnt_scr[...],
                                     jnp.zeros((6, LANES), F32)], axis=0)


def _positions(onehot, ei):
    t = onehot.shape[0]
    tm = TM_DENSE
    tile = pl.BlockSpec((tm, LANES), lambda p, i: (i, 0))
    return pl.pallas_call(
        _pos_kernel,
        grid=(2, t // tm),
        in_specs=[tile, tile],
        out_specs=[pl.BlockSpec((tm, LANES), lambda p, i: (i * p, 0)),
                   pl.BlockSpec((8, LANES), lambda p, i: (0, 0))],
        out_shape=[jax.ShapeDtypeStruct((t, LANES), jnp.int32),
                   jax.ShapeDtypeStruct((8, LANES), F32)],
        scratch_shapes=[pltpu.VMEM((1, LANES), F32)] * 3,
        compiler_params=_cparams(2, 16),
    )(onehot, ei)


def _row_copy(src, s_row, dst, d_row, sem):
    return pltpu.make_async_copy(src.at[pl.ds(s_row, 1)], dst.at[pl.ds(d_row, 1)], sem)


def _dispatch_kernel(dest_ref, h_hbm, xs_in, xs_hbm, sem):
    del xs_in
    td = dest_ref.shape[0] // TOP_K
    base = pl.program_id(0) * td

    def issue(tk, carry):
        _row_copy(h_hbm, base + tk // TOP_K, xs_hbm, dest_ref[tk], sem).start()
        return carry

    def drain(tk, carry):
        _row_copy(h_hbm, 0, xs_hbm, 0, sem).wait()
        return carry

    lax.fori_loop(0, td * TOP_K, issue, 0)
    lax.fori_loop(0, td * TOP_K, drain, 0)


def _dispatch(dest_flat, h2, n_rows):
    t, d = h2.shape
    td = TD_ROWS
    xs0 = jnp.zeros((n_rows, d), h2.dtype)
    return pl.pallas_call(
        _dispatch_kernel,
        grid=(t // td,),
        in_specs=[pl.BlockSpec((td * TOP_K,), lambda i: (i,), memory_space=pltpu.SMEM),
                  pl.BlockSpec(memory_space=pl.ANY),
                  pl.BlockSpec(memory_space=pl.ANY)],
        out_specs=pl.BlockSpec(memory_space=pl.ANY),
        out_shape=jax.ShapeDtypeStruct((n_rows, d), h2.dtype),
        scratch_shapes=[pltpu.SemaphoreType.DMA(())],
        input_output_aliases={2: 0},
        compiler_params=_cparams(1, 16),
    )(dest_flat, h2, xs0)


def _expert_kernel(te_ref, tv_ref, xs_ref, wgu_ref, bgu_ref, wd_ref, bd_ref, o_ref, wgu_bf, wd_bf):
    i = pl.program_id(0)
    e = te_ref[i]
    prev = te_ref[jnp.maximum(i - 1, 0)]
    dff = wd_ref.shape[1]

    @pl.when((i == 0) | (e != prev))
    def _():
        wgu_bf[...] = wgu_ref[0].astype(BF16)
        wd_bf[...] = wd_ref[0].astype(BF16)

    @pl.when(tv_ref[i] > 0)
    def _():
        x = xs_ref[...].astype(BF16)
        hu = jnp.dot(x, wgu_bf[...], preferred_element_type=F32) + bgu_ref[0]
        g = jnp.minimum(hu[:, :dff], SWIGLU_LIMIT)
        up = jnp.clip(hu[:, dff:], -SWIGLU_LIMIT, SWIGLU_LIMIT)
        act = (up + 1.0) * g * jax.nn.sigmoid(SWIGLU_ALPHA * g)
        o_ref[...] = jnp.dot(act.astype(BF16), wd_bf[...], preferred_element_type=F32) + bd_ref[0]

    @pl.when(tv_ref[i] == 0)
    def _():
        o_ref[...] = jnp.zeros_like(o_ref)


def _experts(tile_e, tile_v, xs, w_gu, b_gu, w_d, b_d):
    n_rows, d = xs.shape
    ne, _, f2 = w_gu.shape
    dff = w_d.shape[1]
    tm = TM_EXPERT
    grid_spec = pltpu.PrefetchScalarGridSpec(
        num_scalar_prefetch=2,
        grid=(n_rows // tm,),
        in_specs=[pl.BlockSpec((tm, d), lambda i, te, tv: (i, 0)),
                  pl.BlockSpec((1, d, f2), lambda i, te, tv: (te[i], 0, 0)),
                  pl.BlockSpec((1, 1, f2), lambda i, te, tv: (te[i], 0, 0)),
                  pl.BlockSpec((1, dff, d), lambda i, te, tv: (te[i], 0, 0)),
                  pl.BlockSpec((1, 1, d), lambda i, te, tv: (te[i], 0, 0))],
        out_specs=pl.BlockSpec((tm, d), lambda i, te, tv: (i, 0)),
        scratch_shapes=[pltpu.VMEM((d, f2), BF16), pltpu.VMEM((dff, d), BF16)],
    )
    return pl.pallas_call(
        _expert_kernel,
        grid_spec=grid_spec,
        out_shape=jax.ShapeDtypeStruct((n_rows, d), F32),
        compiler_params=_cparams(1, 56),
    )(tile_e, tile_v, xs, w_gu, b_gu.reshape(ne, 1, f2), w_d, b_d.reshape(ne, 1, d))


def _combine_kernel(dest_ref, gt_ref, x_ref, g_ref, gf_ref, ys_hbm, o_ref, buf, sem, *, final):
    td = x_ref.shape[0]

    def issue(tk, carry):
        pltpu.make_async_copy(ys_hbm.at[pl.ds(dest_ref[tk], 1)],
                              buf.at[tk % TOP_K, pl.ds(tk // TOP_K, 1)], sem).start()
        return carry

    def drain(tk, carry):
        pltpu.make_async_copy(ys_hbm.at[pl.ds(0, 1)], buf.at[0, pl.ds(0, 1)], sem).wait()
        return carry

    lax.fori_loop(0, td * TOP_K, issue, 0)
    lax.fori_loop(0, td * TOP_K, drain, 0)
    gt = gt_ref[...]
    y = gt[:, 0:1] * buf[0]
    for kk in range(1, TOP_K):
        y = y + gt[:, kk:kk + 1] * buf[kk]
    xn = x_ref[...] + g_ref[0] * y
    if final:
        ms = jnp.mean(xn * xn, axis=-1, keepdims=True)
        xn = (xn * lax.rsqrt(ms + EPS)) * gf_ref[...]
    o_ref[...] = xn


def _combine(dest_flat, gates, x2, gate_vec, gf, ys, seq, final):
    t, d = x2.shape
    td = TD_ROWS
    tpb = seq // td
    return pl.pallas_call(
        functools.partial(_combine_kernel, final=final),
        grid=(t // td,),
        in_specs=[pl.BlockSpec((td * TOP_K,), lambda i: (i,), memory_space=pltpu.SMEM),
                  pl.BlockSpec((td, LANES), lambda i: (i, 0)),
                  pl.BlockSpec((td, d), lambda i: (i, 0)),
                  pl.BlockSpec((1, 1, d), lambda i: (i // tpb, 0, 0)),
                  pl.BlockSpec((1, d), lambda i: (0, 0)),
                  pl.BlockSpec(memory_space=pl.ANY)],
        out_specs=pl.BlockSpec((td, d), lambda i: (i, 0)),
        out_shape=jax.ShapeDtypeStruct((t, d), F32),
        scratch_shapes=[pltpu.VMEM((TOP_K, td, d), F32), pltpu.SemaphoreType.DMA(())],
        compiler_params=_cparams(1, 32),
    )(dest_flat, gates, x2, gate_vec, gf.reshape(1, d), ys)


def _moe(x2, norm_g, sc, sh, gate_vec, w_router, b_router, w_gu, b_gu, w_d, b_d, gf, seq, final):
    t, d = x2.shape
    ne = w_router.shape[1]
    wr_pad = jnp.zeros((d, LANES), F32).at[:, :ne].set(w_router)
    br_pad = jnp.full((1, LANES), NEG, F32).at[0, :ne].set(b_router)
    h2, onehot, ei, gates = _router(x2, norm_g, sc, sh, wr_pad, br_pad, seq)
    dest, meta = _positions(onehot, ei)
    dest_flat = dest[:, :TOP_K].reshape(t * TOP_K)

    tm = TM_EXPERT
    n_rows = t * TOP_K + ne * tm
    n_tiles = n_rows // tm
    off = meta[0, :ne].astype(jnp.int32)
    cnt = meta[1, :ne].astype(jnp.int32)
    ends = off + ((cnt + tm - 1) // tm) * tm
    starts = jnp.arange(n_tiles, dtype=jnp.int32) * tm
    valid = starts < ends[-1]
    te = jnp.searchsorted(ends, starts, side='right').astype(jnp.int32)
    te_last = jnp.searchsorted(ends, ends[-1] - 1, side='right').astype(jnp.int32)
    tile_e = jnp.minimum(jnp.where(valid, te, te_last), ne - 1)
    tile_v = valid.astype(jnp.int32)

    xs = _dispatch(dest_flat, h2, n_rows)
    ys = _experts(tile_e, tile_v, xs, w_gu, b_gu, w_d, b_d)
    return _combine(dest_flat, gates, x2, gate_vec, gf, ys, seq, final)


def kernel(x, c, positions, l0_norm1_g, l0_ada_w, l0_ada_b, l0_s5_w_in, l0_s5_b_re, l0_s5_b_im, l0_s5_c_re, l0_s5_c_im, l0_s5_lam_re, l0_s5_lam_im, l0_s5_log_dt, l0_s5_d, l0_s5_w_glu, l0_s5_w_out, l0_norm2_g, l0_moe_w_router, l0_moe_b_router, l0_moe_w_gate_up, l0_moe_b_gate_up, l0_moe_w_down, l0_moe_b_down, l1_norm1_g, l1_ada_w, l1_ada_b, l1_moba_w_qkv, l1_moba_w_o, l1_norm2_g, l1_moe_w_router, l1_moe_b_router, l1_moe_w_gate_up, l1_moe_b_gate_up, l1_moe_w_down, l1_moe_b_down, final_norm_g):
    bsz, seq, d = x.shape
    t = bsz * seq
    assert seq % TM_DENSE == 0 and seq % MOBA_BLOCK == 0 and seq % S5_CHUNK == 0
    assert seq // MOBA_BLOCK <= LANES and d == N_HEADS * HEAD_DIM
    x2 = x.reshape(t, d)
    c_pad = jnp.zeros((8, d), F32).at[:bsz].set(c)

    def ada_parts(w, b):
        a = _ada(c_pad, w, b)[:bsz]
        return [a[:, i * d:(i + 1) * d].reshape(bsz, 1, d) for i in range(6)]

    sh1, sc1, g1, sh2, sc2, g2 = ada_parts(l0_ada_w, l0_ada_b)
    u = _nml(x2, l0_norm1_g, sc1, sh1, l0_s5_w_in.astype(BF16), seq)
    rate, theta, bblk, cblk = _s5_params(l0_s5_b_re, l0_s5_b_im, l0_s5_c_re, l0_s5_c_im,
                                         l0_s5_lam_re, l0_s5_lam_im, l0_s5_log_dt)
    z = _s5_scan(u, rate, theta, bblk, cblk, l0_s5_d, bsz, seq)
    x2 = _row_call(_glu_out_kernel, z, x2, g1,
                   [l0_s5_w_glu.astype(BF16), l0_s5_w_out.astype(BF16)], seq)
    x2 = _moe(x2, l0_norm2_g, sc2, sh2, g2, l0_moe_w_router, l0_moe_b_router,
              l0_moe_w_gate_up, l0_moe_b_gate_up, l0_moe_w_down, l0_moe_b_down,
              final_norm_g, seq, final=False)

    sh1, sc1, g1, sh2, sc2, g2 = ada_parts(l1_ada_w, l1_ada_b)
    qkv = _nml(x2, l1_norm1_g, sc1, sh1, l1_moba_w_qkv.astype(BF16), seq)
    cos, sin = _rope_tables(positions.reshape(t, 1))
    q, k, v, km = _rope(qkv, cos, sin, bsz, seq)
    nb = seq // MOBA_BLOCK
    km_pad = jnp.zeros((bsz, N_HEADS, LANES, HEAD_DIM), F32).at[:, :, :nb].set(km[:, :, :, 0])
    o = _moba(q, k, v, km_pad, bsz, seq)
    x2 = _row_call(_lin_res_kernel, o, x2, g1, [l1_moba_w_o.astype(BF16)], seq)
    x2 = _moe(x2, l1_norm2_g, sc2, sh2, g2, l1_moe_w_router, l1_moe_b_router,
              l1_moe_w_gate_up, l1_moe_b_gate_up, l1_moe_w_down, l1_moe_b_down,
              final_norm_g, seq, final=True)
    return x2.reshape(bsz, seq, d)
```

```python
import functools
import math

import jax
import jax.numpy as jnp
from jax import lax
from jax.experimental import pallas as pl
from jax.experimental.pallas import tpu as pltpu

F32 = jnp.float32
BF16 = jnp.bfloat16
HI = lax.Precision.HIGHEST

EPS = 1e-6
NEG = -1e30
LANES = 128
MIB = 1024 * 1024

S5_GROUP = 16
S5_STATE = 64
N_HEADS = 8
HEAD_DIM = 128
MOBA_BLOCK = 256
MOBA_TOPK = 3
ROPE_THETA = 10000.0
N_EXPERTS = 32
TOP_K = 4
SWIGLU_LIMIT = 7.0
SWIGLU_ALPHA = 1.702

TM_DENSE = 512
S5_CHUNK = 128
S5_SUPER = 8
TM_EXPERT = 256
TD_ROWS = 128
TM_SRC = 2048


def _cparams(n_axes, vmem_mib):
    return pltpu.CompilerParams(dimension_semantics=("arbitrary",) * n_axes,
                                vmem_limit_bytes=vmem_mib * MIB)


def _normmod(x, g, sc, sh):
    ms = jnp.mean(x * x, axis=-1, keepdims=True)
    return (x * lax.rsqrt(ms + EPS)) * g * (1.0 + sc) + sh


def _ada_kernel(c_ref, w_ref, b_ref, o_ref):
    c = c_ref[...]
    ca = c * jax.nn.sigmoid(c)
    o_ref[...] = jnp.dot(ca, w_ref[...], preferred_element_type=F32, precision=HI) + b_ref[...]


def _ada(c_pad, w, b):
    d, n = w.shape
    tn = 1536
    return pl.pallas_call(
        _ada_kernel,
        grid=(n // tn,),
        in_specs=[pl.BlockSpec((8, d), lambda j: (0, 0)),
                  pl.BlockSpec((d, tn), lambda j: (0, j)),
                  pl.BlockSpec((1, tn), lambda j: (0, j))],
        out_specs=pl.BlockSpec((8, tn), lambda j: (0, j)),
        out_shape=jax.ShapeDtypeStruct((8, n), F32),
        compiler_params=_cparams(1, 32),
    )(c_pad, w, b.reshape(1, n))


def _nml_kernel(x_ref, g_ref, sc_ref, sh_ref, w_ref, o_ref):
    h = _normmod(x_ref[...], g_ref[...], sc_ref[0], sh_ref[0])
    o_ref[...] = jnp.dot(h.astype(BF16), w_ref[...], preferred_element_type=F32)


def _nml(x2, g, sc, sh, w_bf, seq):
    t, d = x2.shape
    n = w_bf.shape[1]
    tm = TM_DENSE
    tpb = seq // tm
    return pl.pallas_call(
        _nml_kernel,
        grid=(t // tm,),
        in_specs=[pl.BlockSpec((tm, d), lambda i: (i, 0)),
                  pl.BlockSpec((1, d), lambda i: (0, 0)),
                  pl.BlockSpec((1, 1, d), lambda i: (i // tpb, 0, 0)),
                  pl.BlockSpec((1, 1, d), lambda i: (i // tpb, 0, 0)),
                  pl.BlockSpec((d, n), lambda i: (0, 0))],
        out_specs=pl.BlockSpec((tm, n), lambda i: (i, 0)),
        out_shape=jax.ShapeDtypeStruct((t, n), F32),
        compiler_params=_cparams(1, 48),
    )(x2, g.reshape(1, d), sc, sh, w_bf)


def _gelu_tanh(y):
    c = math.sqrt(2.0 / math.pi)
    return y * (0.5 * (1.0 + jnp.tanh(c * (y + 0.044715 * (y * y * y)))))


def _s5_kernel(u_ref, rate_ref, theta_ref, bblk_ref, cblk_ref, d_ref, z_ref,
               apr, api, air, aii, xr, xi):
    L = u_ref.shape[0]
    n_super = bblk_ref.shape[0]
    half = apr.shape[1] // n_super
    nch = u_ref.shape[1] // n_super
    b = pl.program_id(0)
    c = pl.program_id(1)

    @pl.when((b == 0) & (c == 0))
    def _():
        tt = lax.broadcasted_iota(jnp.int32, (L, 1), 0).astype(F32)
        ph = theta_ref[...] * tt
        rt = rate_ref[...] * tt
        cs = jnp.cos(ph)
        sn = jnp.sin(ph)
        mag = jnp.exp(rt)
        inv = jnp.exp(-rt)
        apr[...] = mag * cs
        api[...] = mag * sn
        air[...] = inv * cs
        aii[...] = -(inv * sn)

    @pl.when(c == 0)
    def _():
        xr[...] = jnp.zeros_like(xr)
        xi[...] = jnp.zeros_like(xi)

    row = lax.broadcasted_iota(jnp.int32, (L, L), 0)
    col = lax.broadcasted_iota(jnp.int32, (L, L), 1)
    tril = (row >= col).astype(BF16)

    for sg in range(n_super):
        cols = slice(sg * half, (sg + 1) * half)
        ch = slice(sg * nch, (sg + 1) * nch)
        u = u_ref[:, ch]
        bu = jnp.dot(u.astype(BF16), bblk_ref[sg], preferred_element_type=F32)
        bur = bu[:, :half]
        bui = bu[:, half:]
        ir = air[:, cols]
        ii = aii[:, cols]
        zc = jnp.concatenate([bur * ir - bui * ii, bur * ii + bui * ir], axis=1)
        cum = jnp.dot(tril, zc.astype(BF16), preferred_element_type=F32)
        a_r = apr[1:2, cols]
        a_i = api[1:2, cols]
        p_r = xr[:, cols]
        p_i = xi[:, cols]
        cr = cum[:, :half] + (a_r * p_r - a_i * p_i)
        ci = cum[:, half:] + (a_r * p_i + a_i * p_r)
        pw_r = apr[:, cols]
        pw_i = api[:, cols]
        x_re = cr * pw_r - ci * pw_i
        x_im = cr * pw_i + ci * pw_r
        xr[:, cols] = x_re[L - 1:L, :]
        xi[:, cols] = x_im[L - 1:L, :]
        xc = jnp.concatenate([x_re, x_im], axis=1).astype(BF16)
        y = jnp.dot(xc, cblk_ref[sg], preferred_element_type=F32) + d_ref[:, ch] * u
        z_ref[:, ch] = _gelu_tanh(y)


def _s5_scan(u, rate, theta, bblk, cblk, d_skip, bsz, seq):
    t, w = u.shape
    L = S5_CHUNK
    nc = seq // L
    ns = rate.shape[1]
    return pl.pallas_call(
        _s5_kernel,
        grid=(bsz, nc),
        in_specs=[pl.BlockSpec((L, w), lambda b, c: (b * nc + c, 0)),
                  pl.BlockSpec((1, ns), lambda b, c: (0, 0)),
                  pl.BlockSpec((1, ns), lambda b, c: (0, 0)),
                  pl.BlockSpec(bblk.shape, lambda b, c: (0, 0, 0)),
                  pl.BlockSpec(cblk.shape, lambda b, c: (0, 0, 0)),
                  pl.BlockSpec((1, w), lambda b, c: (0, 0))],
        out_specs=pl.BlockSpec((L, w), lambda b, c: (b * nc + c, 0)),
        out_shape=jax.ShapeDtypeStruct((t, w), F32),
        scratch_shapes=[pltpu.VMEM((L, ns), F32)] * 4 + [pltpu.VMEM((1, ns), F32)] * 2,
        compiler_params=_cparams(2, 48),
    )(u, rate, theta, bblk, cblk, d_skip.reshape(1, w))


def _s5_params(b_re, b_im, c_re, c_im, lam_re, lam_im, log_dt):
    g, p, ch = b_re.shape
    dt = jnp.exp(log_dt)[:, None]
    rate = lam_re * dt
    theta = lam_im * dt
    mag = jnp.exp(rate)
    a_re = mag * jnp.cos(theta)
    a_im = mag * jnp.sin(theta)
    den = lam_re * lam_re + lam_im * lam_im
    f_re = ((a_re - 1.0) * lam_re + a_im * lam_im) / den
    f_im = (a_im * lam_re - (a_re - 1.0) * lam_im) / den
    bb_re = f_re[..., None] * b_re - f_im[..., None] * b_im
    bb_im = f_re[..., None] * b_im + f_im[..., None] * b_re
    ns = S5_SUPER
    nsg = g // ns
    eye = jnp.eye(ns, dtype=F32)

    def blk_b(bb):
        return jnp.einsum('sgpc,gh->sgchp', bb.reshape(nsg, ns, p, ch), eye).reshape(nsg, ns * ch, ns * p)

    def blk_c(cc):
        return jnp.einsum('sgcp,gh->sgphc', cc.reshape(nsg, ns, ch, p), eye).reshape(nsg, ns * p, ns * ch)

    bblk = jnp.concatenate([blk_b(bb_re), blk_b(bb_im)], axis=2).astype(BF16)
    cblk = jnp.concatenate([blk_c(c_re), -blk_c(c_im)], axis=1).astype(BF16)
    return rate.reshape(1, g * p), theta.reshape(1, g * p), bblk, cblk


def _glu_out_kernel(z_ref, x_ref, g_ref, wg_ref, wo_ref, o_ref):
    z = z_ref[...]
    gate = jax.nn.sigmoid(jnp.dot(z.astype(BF16), wg_ref[...], preferred_element_type=F32))
    m = jnp.dot((z * gate).astype(BF16), wo_ref[...], preferred_element_type=F32)
    o_ref[...] = x_ref[...] + g_ref[0] * m


def _lin_res_kernel(a_ref, x_ref, g_ref, w_ref, o_ref):
    m = jnp.dot(a_ref[...].astype(BF16), w_ref[...], preferred_element_type=F32)
    o_ref[...] = x_ref[...] + g_ref[0] * m


def _row_call(kernel, acts, x2, gate, weights, seq):
    t, d = x2.shape
    tm = TM_DENSE
    tpb = seq // tm
    row = pl.BlockSpec((tm, d), lambda i: (i, 0))
    return pl.pallas_call(
        kernel,
        grid=(t // tm,),
        in_specs=[row, row, pl.BlockSpec((1, 1, d), lambda i: (i // tpb, 0, 0))]
                 + [pl.BlockSpec(w.shape, lambda i: (0, 0)) for w in weights],
        out_specs=row,
        out_shape=jax.ShapeDtypeStruct((t, d), F32),
        compiler_params=_cparams(1, 48),
    )(acts, x2, gate, *weights)


def _rope_tab_kernel(pos_ref, inv_ref, sgn_ref, cos_ref, sin_ref):
    ang = pos_ref[...].astype(F32) * inv_ref[...]
    cos_ref[...] = jnp.cos(ang)
    sin_ref[...] = jnp.sin(ang) * sgn_ref[...]


def _rope_tables(pos_col):
    t = pos_col.shape[0]
    ts = 512
    half = HEAD_DIM // 2
    inv = ROPE_THETA ** (-jnp.arange(0, HEAD_DIM, 2, dtype=F32) / HEAD_DIM)
    inv = jnp.concatenate([inv, inv]).reshape(1, HEAD_DIM)
    sgn = jnp.concatenate([-jnp.ones((half,), F32), jnp.ones((half,), F32)]).reshape(1, HEAD_DIM)
    const = pl.BlockSpec((1, HEAD_DIM), lambda i: (0, 0))
    tile = pl.BlockSpec((ts, HEAD_DIM), lambda i: (i, 0))
    return pl.pallas_call(
        _rope_tab_kernel,
        grid=(t // ts,),
        in_specs=[pl.BlockSpec((ts, 1), lambda i: (i, 0)), const, const],
        out_specs=[tile, tile],
        out_shape=[jax.ShapeDtypeStruct((t, HEAD_DIM), F32)] * 2,
        compiler_params=_cparams(1, 16),
    )(pos_col, inv, sgn)


def _rope_kernel(q_ref, k_ref, v_ref, cos_ref, sin_ref, qo_ref, ko_ref, vo_ref, km_ref):
    cs = cos_ref[...]
    sn = sin_ref[...]
    q = q_ref[...]
    k = k_ref[...]
    half = q.shape[1] // 2
    qr = q * cs + pltpu.roll(q, half, 1) * sn
    kr = k * cs + pltpu.roll(k, half, 1) * sn
    qo_ref[0, 0] = qr
    ko_ref[0, 0] = kr.astype(BF16)
    vo_ref[0, 0] = v_ref[...].astype(BF16)
    km_ref[0, 0, 0] = jnp.mean(kr, axis=0, keepdims=True)


def _rope(qkv, cos, sin, bsz, seq):
    t = qkv.shape[0]
    h, dh, blk = N_HEADS, HEAD_DIM, MOBA_BLOCK
    nb = seq // blk
    hs = lambda off: pl.BlockSpec((blk, dh), lambda i, j: (i, off + j))
    tab = pl.BlockSpec((blk, dh), lambda i, j: (i, 0))
    head = pl.BlockSpec((1, 1, blk, dh), lambda i, j: (i // nb, j, i % nb, 0))
    return pl.pallas_call(
        _rope_kernel,
        grid=(t // blk, h),
        in_specs=[hs(0), hs(h), hs(2 * h), tab, tab],
        out_specs=[head, head, head,
                   pl.BlockSpec((1, 1, 1, 1, dh), lambda i, j: (i // nb, j, i % nb, 0, 0))],
        out_shape=[jax.ShapeDtypeStruct((bsz, h, seq, dh), F32),
                   jax.ShapeDtypeStruct((bsz, h, seq, dh), BF16),
                   jax.ShapeDtypeStruct((bsz, h, seq, dh), BF16),
                   jax.ShapeDtypeStruct((bsz, h, nb, 1, dh), F32)],
        compiler_params=_cparams(2, 16),
    )(qkv, qkv, qkv, cos, sin)


def _moba_kernel(q_ref, k_ref, v_ref, km_ref, o_ref):
    blk = q_ref.shape[2]
    j = pl.program_id(2)
    q = q_ref[0, 0]
    nt = (((1,), (1,)), ((), ()))
    gate = lax.dot_general(q, km_ref[0, 0], nt, preferred_element_type=F32, precision=HI)
    lane = lax.broadcasted_iota(jnp.int32, gate.shape, 1)
    lane_f = lane.astype(F32)
    work = jnp.where(lane < j, gate, NEG)
    sel = jnp.zeros(gate.shape, F32)
    for _ in range(MOBA_TOPK):
        m = jnp.max(work, axis=-1, keepdims=True)
        idx = jnp.min(jnp.where(work == m, lane_f, float(LANES)), axis=-1, keepdims=True)
        hit = lane_f == idx
        sel = jnp.where(hit & (m > 0.5 * NEG), 1.0, sel)
        work = jnp.where(hit, 2.0 * NEG, work)

    scale = q.shape[1] ** -0.5
    qb = q.astype(BF16)

    def attend(s, vb, m_i, l_i, acc):
        m_new = jnp.maximum(m_i, jnp.max(s, axis=-1, keepdims=True))
        alpha = jnp.exp(m_i - m_new)
        p = jnp.exp(s - m_new)
        l_new = alpha * l_i + jnp.sum(p, axis=-1, keepdims=True)
        acc_new = alpha * acc + jnp.dot(p.astype(BF16), vb, preferred_element_type=F32)
        return m_new, l_new, acc_new

    def body(n, carry):
        m_i, l_i, acc = carry
        start = pl.multiple_of(n * blk, blk)
        kb = k_ref[0, 0, pl.ds(start, blk), :]
        vb = v_ref[0, 0, pl.ds(start, blk), :]
        s = lax.dot_general(qb, kb, nt, preferred_element_type=F32) * scale
        sel_n = jnp.sum(jnp.where(lane == n, sel, 0.0), axis=-1, keepdims=True)
        s = jnp.where(sel_n > 0.5, s, NEG)
        return attend(s, vb, m_i, l_i, acc)

    init = (jnp.full((blk, 1), NEG, F32), jnp.zeros((blk, 1), F32), jnp.zeros(q.shape, F32))
    m_i, l_i, acc = lax.fori_loop(0, j, body, init)

    start = pl.multiple_of(j * blk, blk)
    kb = k_ref[0, 0, pl.ds(start, blk), :]
    vb = v_ref[0, 0, pl.ds(start, blk), :]
    s = lax.dot_general(qb, kb, nt, preferred_element_type=F32) * scale
    r = lax.broadcasted_iota(jnp.int32, s.shape, 0)
    cc = lax.broadcasted_iota(jnp.int32, s.shape, 1)
    s = jnp.where(cc <= r, s, NEG)
    m_i, l_i, acc = attend(s, vb, m_i, l_i, acc)
    o_ref[...] = acc / l_i


def _moba(q, k, v, km_pad, bsz, seq):
    h, dh, blk = N_HEADS, HEAD_DIM, MOBA_BLOCK
    nb = seq // blk
    kv = pl.BlockSpec((1, 1, seq, dh), lambda b, j, i: (b, j, 0, 0))
    return pl.pallas_call(
        _moba_kernel,
        grid=(bsz, h, nb),
        in_specs=[pl.BlockSpec((1, 1, blk, dh), lambda b, j, i: (b, j, i, 0)), kv, kv,
                  pl.BlockSpec((1, 1, LANES, dh), lambda b, j, i: (b, j, 0, 0))],
        out_specs=pl.BlockSpec((blk, dh), lambda b, j, i: (b * nb + i, j)),
        out_shape=jax.ShapeDtypeStruct((bsz * seq, h * dh), F32),
        compiler_params=_cparams(3, 32),
    )(q, k, v, km_pad)


def _router_kernel(x_ref, g_ref, sc_ref, sh_ref, wr_ref, br_ref, h_ref, oh_ref, ei_ref, gt_ref):
    h = _normmod(x_ref[...], g_ref[...], sc_ref[0], sh_ref[0])
    h_ref[...] = h
    logits = jnp.dot(h, wr_ref[...], preferred_element_type=F32, precision=HI) + br_ref[...]
    lane = lax.broadcasted_iota(jnp.int32, logits.shape, 1)
    lane_f = lane.astype(F32)
    work = logits
    onehot = jnp.zeros(logits.shape, F32)
    ei = jnp.zeros(logits.shape, F32)
    vals = []
    for kk in range(TOP_K):
        m = jnp.max(work, axis=-1, keepdims=True)
        idx = jnp.min(jnp.where(work == m, lane_f, float(LANES)), axis=-1, keepdims=True)
        hit = lane_f == idx
        onehot = jnp.where(hit, 1.0, onehot)
        ei = jnp.where(lane == kk, idx, ei)
        vals.append(m)
        work = jnp.where(hit, 2.0 * NEG, work)
    ex = [jnp.exp(vv - vals[0]) for vv in vals]
    tot = ex[0] + ex[1] + ex[2] + ex[3]
    gt = jnp.zeros(logits.shape, F32)
    for kk in range(TOP_K):
        gt = jnp.where(lane == kk, ex[kk] / tot, gt)
    oh_ref[...] = onehot
    ei_ref[...] = ei
    gt_ref[...] = gt


def _router(x2, g, sc, sh, wr_pad, br_pad, seq):
    t, d = x2.shape
    tm = TM_DENSE
    tpb = seq // tm
    lane_tile = pl.BlockSpec((tm, LANES), lambda i: (i, 0))
    return pl.pallas_call(
        _router_kernel,
        grid=(t // tm,),
        in_specs=[pl.BlockSpec((tm, d), lambda i: (i, 0)),
                  pl.BlockSpec((1, d), lambda i: (0, 0)),
                  pl.BlockSpec((1, 1, d), lambda i: (i // tpb, 0, 0)),
                  pl.BlockSpec((1, 1, d), lambda i: (i // tpb, 0, 0)),
                  pl.BlockSpec((d, LANES), lambda i: (0, 0)),
                  pl.BlockSpec((1, LANES), lambda i: (0, 0))],
        out_specs=[pl.BlockSpec((tm, d), lambda i: (i, 0)), lane_tile, lane_tile, lane_tile],
        out_shape=[jax.ShapeDtypeStruct((t, d), F32)] + [jax.ShapeDtypeStruct((t, LANES), F32)] * 3,
        compiler_params=_cparams(1, 32),
    )(x2, g.reshape(1, d), sc, sh, wr_pad, br_pad)


def _pos_kernel(oh_ref, ei_ref, dest_ref, meta_ref, cnt_scr, off_scr, run_scr):
    p = pl.program_id(0)
    i = pl.program_id(1)
    tm = oh_ref.shape[0]
    oh = oh_ref[...]
    colsum = jnp.sum(oh, axis=0, keepdims=True)

    @pl.when((p == 0) & (i == 0))
    def _():
        cnt_scr[...] = jnp.zeros_like(cnt_scr)

    @pl.when(p == 0)
    def _():
        cnt_scr[...] += colsum
        dest_ref[...] = jnp.zeros_like(dest_ref)

    @pl.when((p == 1) & (i == 0))
    def _():
        cnt = cnt_scr[...]
        padded = jnp.ceil(cnt * (1.0 / TM_EXPERT)) * float(TM_EXPERT)
        r = lax.broadcasted_iota(jnp.int32, (LANES, LANES), 0)
        cidx = lax.broadcasted_iota(jnp.int32, (LANES, LANES), 1)
        tri = (r < cidx).astype(F32)
        off = jnp.dot(jnp.broadcast_to(padded, (8, LANES)), tri,
                      preferred_element_type=F32, precision=HI)
        off_scr[...] = off[0:1]
        run_scr[...] = jnp.zeros_like(run_scr)

    @pl.when(p == 1)
    def _():
        r = lax.broadcasted_iota(jnp.int32, (tm, tm), 0)
        cidx = lax.broadcasted_iota(jnp.int32, (tm, tm), 1)
        ltri = (r > cidx).astype(BF16)
        excl = jnp.dot(ltri, oh.astype(BF16), preferred_element_type=F32)
        dfull = off_scr[...] + run_scr[...] + excl
        ei = ei_ref[...]
        lane = lax.broadcasted_iota(jnp.int32, oh.shape, 1)
        lane_f = lane.astype(F32)
        dest = jnp.zeros(oh.shape, F32)
        for kk in range(TOP_K):
            ek = jnp.sum(jnp.where(lane == kk, ei, 0.0), axis=-1, keepdims=True)
            dk = jnp.sum(jnp.where(lane_f == ek, dfull, 0.0), axis=-1, keepdims=True)
            dest = jnp.where(lane == kk, dk, dest)
        dest_ref[...] = dest.astype(jnp.int32)
        run_scr[...] += colsum

    meta_ref[...] = jnp.concatenate([off_scr[...], cnt_scr[...],
                                     jnp.zeros((6, LANES), F32)], axis=0)


def _positions(onehot, ei):
    t = onehot.shape[0]
    tm = TM_DENSE
    tile = pl.BlockSpec((tm, LANES), lambda p, i: (i, 0))
    return pl.pallas_call(
        _pos_kernel,
        grid=(2, t // tm),
        in_specs=[tile, tile],
        out_specs=[pl.BlockSpec((tm, LANES), lambda p, i: (i * p, 0)),
                   pl.BlockSpec((8, LANES), lambda p, i: (0, 0))],
        out_shape=[jax.ShapeDtypeStruct((t, LANES), jnp.int32),
                   jax.ShapeDtypeStruct((8, LANES), F32)],
        scratch_shapes=[pltpu.VMEM((1, LANES), F32)] * 3,
        compiler_params=_cparams(2, 16),
    )(onehot, ei)


def _src_kernel(dest_ref, src_ref):
    i = pl.program_id(0)
    tb = dest_ref.shape[0] // TOP_K

    @pl.when(i == 0)
    def _():
        def zero(j, carry):
            src_ref[j] = 0
            return carry
        lax.fori_loop(0, src_ref.shape[0], zero, 0)

    def body(tt, carry):
        for kk in range(TOP_K):
            src_ref[dest_ref[tt * TOP_K + kk]] = i * tb + tt
        return carry

    lax.fori_loop(0, tb, body, 0, unroll=4)


def _sources(dest_flat, n_rows):
    n = dest_flat.shape[0]
    tb = TM_SRC
    return pl.pallas_call(
        _src_kernel,
        grid=(n // (tb * TOP_K),),
        in_specs=[pl.BlockSpec((tb * TOP_K,), lambda i: (i,), memory_space=pltpu.SMEM)],
        out_specs=pl.BlockSpec(memory_space=pltpu.SMEM),
        out_shape=jax.ShapeDtypeStruct((n_rows,), jnp.int32),
        compiler_params=_cparams(1, 16),
    )(dest_flat)


def _expert_kernel(te_ref, tv_ref, src_cur, src_nxt, h_hbm, wgu_ref, bgu_ref, wd_ref, bd_ref,
                   o_ref, xbuf, sem, wgu_bf, wd_bf):
    i = pl.program_id(0)
    nt = pl.num_programs(0)
    tm = o_ref.shape[0]
    slot = lax.rem(i, 2)
    e = te_ref[i]
    prev = te_ref[jnp.maximum(i - 1, 0)]
    dff = wd_ref.shape[1]

    def gather(src_ref, s):
        def body(r8, carry):
            for uu in range(8):
                r = r8 * 8 + uu
                pltpu.make_async_copy(h_hbm.at[pl.ds(src_ref[r], 1)],
                                      xbuf.at[s, pl.ds(r, 1)], sem.at[s]).start()
            return carry
        lax.fori_loop(0, tm // 8, body, 0)

    @pl.when((i == 0) & (tv_ref[0] > 0))
    def _():
        gather(src_cur, 0)

    nxt = jnp.minimum(i + 1, nt - 1)

    @pl.when((i + 1 < nt) & (tv_ref[nxt] > 0))
    def _():
        gather(src_nxt, 1 - slot)

    @pl.when((i == 0) | (e != prev))
    def _():
        wgu_bf[...] = wgu_ref[0].astype(BF16)
        wd_bf[...] = wd_ref[0].astype(BF16)

    @pl.when(tv_ref[i] > 0)
    def _():
        pltpu.make_async_copy(h_hbm.at[pl.ds(0, tm)], xbuf.at[slot], sem.at[slot]).wait()
        x = xbuf[slot].astype(BF16)
        hu = jnp.dot(x, wgu_bf[...], preferred_element_type=F32) + bgu_ref[0]
        g = jnp.minimum(hu[:, :dff], SWIGLU_LIMIT)
        up = jnp.clip(hu[:, dff:], -SWIGLU_LIMIT, SWIGLU_LIMIT)
        act = (up + 1.0) * g * jax.nn.sigmoid(SWIGLU_ALPHA * g)
        o_ref[...] = jnp.dot(act.astype(BF16), wd_bf[...], preferred_element_type=F32) + bd_ref[0]

    @pl.when(tv_ref[i] == 0)
    def _():
        o_ref[...] = jnp.zeros_like(o_ref)


def _experts(tile_e, tile_v, src, h2, w_gu, b_gu, w_d, b_d):
    n_rows = src.shape[0]
    d = h2.shape[1]
    ne, _, f2 = w_gu.shape
    dff = w_d.shape[1]
    tm = TM_EXPERT
    nt = n_rows // tm
    grid_spec = pltpu.PrefetchScalarGridSpec(
        num_scalar_prefetch=2,
        grid=(nt,),
        in_specs=[pl.BlockSpec((tm,), lambda i, te, tv: (i,), memory_space=pltpu.SMEM),
                  pl.BlockSpec((tm,), lambda i, te, tv: (jnp.minimum(i + 1, nt - 1),),
                               memory_space=pltpu.SMEM),
                  pl.BlockSpec(memory_space=pl.ANY),
                  pl.BlockSpec((1, d, f2), lambda i, te, tv: (te[i], 0, 0)),
                  pl.BlockSpec((1, 1, f2), lambda i, te, tv: (te[i], 0, 0)),
                  pl.BlockSpec((1, dff, d), lambda i, te, tv: (te[i], 0, 0)),
                  pl.BlockSpec((1, 1, d), lambda i, te, tv: (te[i], 0, 0))],
        out_specs=pl.BlockSpec((tm, d), lambda i, te, tv: (i, 0)),
        scratch_shapes=[pltpu.VMEM((2, tm, d), F32), pltpu.SemaphoreType.DMA((2,)),
                        pltpu.VMEM((d, f2), BF16), pltpu.VMEM((dff, d), BF16)],
    )
    return pl.pallas_call(
        _expert_kernel,
        grid_spec=grid_spec,
        out_shape=jax.ShapeDtypeStruct((n_rows, d), F32),
        compiler_params=_cparams(1, 56),
    )(tile_e, tile_v, src, src, h2, w_gu, b_gu.reshape(ne, 1, f2), w_d, b_d.reshape(ne, 1, d))


def _combine_kernel(dcur, dnxt, gt_ref, x_ref, g_ref, gf_ref, ys_hbm, o_ref, buf, sem, *, final):
    i = pl.program_id(0)
    nt = pl.num_programs(0)
    td = x_ref.shape[0]
    slot = lax.rem(i, 2)

    def gather(dref, s):
        def body(t2, carry):
            for uu in range(2):
                tt = t2 * 2 + uu
                for kk in range(TOP_K):
                    pltpu.make_async_copy(ys_hbm.at[pl.ds(dref[tt * TOP_K + kk], 1)],
                                          buf.at[s, kk, pl.ds(tt, 1)], sem.at[s]).start()
            return carry
        lax.fori_loop(0, td // 2, body, 0)

    @pl.when(i == 0)
    def _():
        gather(dcur, 0)

    @pl.when(i + 1 < nt)
    def _():
        gather(dnxt, 1 - slot)

    for kk in range(TOP_K):
        pltpu.make_async_copy(ys_hbm.at[pl.ds(0, td)], buf.at[slot, kk], sem.at[slot]).wait()
    gt = gt_ref[...]
    y = gt[:, 0:1] * buf[slot, 0]
    for kk in range(1, TOP_K):
        y = y + gt[:, kk:kk + 1] * buf[slot, kk]
    xn = x_ref[...] + g_ref[0] * y
    if final:
        ms = jnp.mean(xn * xn, axis=-1, keepdims=True)
        xn = (xn * lax.rsqrt(ms + EPS)) * gf_ref[...]
    o_ref[...] = xn


def _combine(dest_flat, gates, x2, gate_vec, gf, ys, seq, final):
    t, d = x2.shape
    td = TD_ROWS
    tpb = seq // td
    nt = t // td
    return pl.pallas_call(
        functools.partial(_combine_kernel, final=final),
        grid=(nt,),
        in_specs=[pl.BlockSpec((td * TOP_K,), lambda i: (i,), memory_space=pltpu.SMEM),
                  pl.BlockSpec((td * TOP_K,), lambda i: (jnp.minimum(i + 1, nt - 1),),
                               memory_space=pltpu.SMEM),
                  pl.BlockSpec((td, LANES), lambda i: (i, 0)),
                  pl.BlockSpec((td, d), lambda i: (i, 0)),
                  pl.BlockSpec((1, 1, d), lambda i: (i // tpb, 0, 0)),
                  pl.BlockSpec((1, d), lambda i: (0, 0)),
                  pl.BlockSpec(memory_space=pl.ANY)],
        out_specs=pl.BlockSpec((td, d), lambda i: (i, 0)),
        out_shape=jax.ShapeDtypeStruct((t, d), F32),
        scratch_shapes=[pltpu.VMEM((2, TOP_K, td, d), F32), pltpu.SemaphoreType.DMA((2,))],
        compiler_params=_cparams(1, 32),
    )(dest_flat, dest_flat, gates, x2, gate_vec, gf.reshape(1, d), ys)


def _moe(x2, norm_g, sc, sh, gate_vec, w_router, b_router, w_gu, b_gu, w_d, b_d, gf, seq, final):
    t, d = x2.shape
    ne = w_router.shape[1]
    wr_pad = jnp.zeros((d, LANES), F32).at[:, :ne].set(w_router)
    br_pad = jnp.full((1, LANES), NEG, F32).at[0, :ne].set(b_router)
    h2, onehot, ei, gates = _router(x2, norm_g, sc, sh, wr_pad, br_pad, seq)
    dest, meta = _positions(onehot, ei)
    dest_flat = dest[:, :TOP_K].reshape(t * TOP_K)

    tm = TM_EXPERT
    n_rows = t * TOP_K + ne * tm
    n_tiles = n_rows // tm
    off = meta[0, :ne].astype(jnp.int32)
    cnt = meta[1, :ne].astype(jnp.int32)
    ends = off + ((cnt + tm - 1) // tm) * tm
    starts = jnp.arange(n_tiles, dtype=jnp.int32) * tm
    valid = starts < ends[-1]
    te = jnp.sum((starts[:, None] >= ends[None, :]).astype(jnp.int32), axis=1)
    te_last = jnp.sum((ends[-1] - 1 >= ends).astype(jnp.int32))
    tile_e = jnp.minimum(jnp.where(valid, te, te_last), ne - 1)
    tile_v = valid.astype(jnp.int32)

    src = _sources(dest_flat, n_rows)
    ys = _experts(tile_e, tile_v, src, h2, w_gu, b_gu, w_d, b_d)
    return _combine(dest_flat, gates, x2, gate_vec, gf, ys, seq, final)


def kernel(x, c, positions, l0_norm1_g, l0_ada_w, l0_ada_b, l0_s5_w_in, l0_s5_b_re, l0_s5_b_im, l0_s5_c_re, l0_s5_c_im, l0_s5_lam_re, l0_s5_lam_im, l0_s5_log_dt, l0_s5_d, l0_s5_w_glu, l0_s5_w_out, l0_norm2_g, l0_moe_w_router, l0_moe_b_router, l0_moe_w_gate_up, l0_moe_b_gate_up, l0_moe_w_down, l0_moe_b_down, l1_norm1_g, l1_ada_w, l1_ada_b, l1_moba_w_qkv, l1_moba_w_o, l1_norm2_g, l1_moe_w_router, l1_moe_b_router, l1_moe_w_gate_up, l1_moe_b_gate_up, l1_moe_w_down, l1_moe_b_down, final_norm_g):
    bsz, seq, d = x.shape
    t = bsz * seq
    assert seq % TM_DENSE == 0 and seq % MOBA_BLOCK == 0 and seq % S5_CHUNK == 0
    assert seq // MOBA_BLOCK <= LANES and d == N_HEADS * HEAD_DIM
    x2 = x.reshape(t, d)
    c_pad = jnp.zeros((8, d), F32).at[:bsz].set(c)

    def ada_parts(w, b):
        a = _ada(c_pad, w, b)[:bsz]
        return [a[:, i * d:(i + 1) * d].reshape(bsz, 1, d) for i in range(6)]

    sh1, sc1, g1, sh2, sc2, g2 = ada_parts(l0_ada_w, l0_ada_b)
    u = _nml(x2, l0_norm1_g, sc1, sh1, l0_s5_w_in.astype(BF16), seq)
    rate, theta, bblk, cblk = _s5_params(l0_s5_b_re, l0_s5_b_im, l0_s5_c_re, l0_s5_c_im,
                                         l0_s5_lam_re, l0_s5_lam_im, l0_s5_log_dt)
    z = _s5_scan(u, rate, theta, bblk, cblk, l0_s5_d, bsz, seq)
    x2 = _row_call(_glu_out_kernel, z, x2, g1,
                   [l0_s5_w_glu.astype(BF16), l0_s5_w_out.astype(BF16)], seq)
    x2 = _moe(x2, l0_norm2_g, sc2, sh2, g2, l0_moe_w_router, l0_moe_b_router,
              l0_moe_w_gate_up, l0_moe_b_gate_up, l0_moe_w_down, l0_moe_b_down,
              final_norm_g, seq, final=False)

    sh1, sc1, g1, sh2, sc2, g2 = ada_parts(l1_ada_w, l1_ada_b)
    qkv = _nml(x2, l1_norm1_g, sc1, sh1, l1_moba_w_qkv.astype(BF16), seq)
    cos, sin = _rope_tables(positions.reshape(t, 1))
    q, k, v, km = _rope(qkv, cos, sin, bsz, seq)
    nb = seq // MOBA_BLOCK
    km_pad = jnp.zeros((bsz, N_HEADS, LANES, HEAD_DIM), F32).at[:, :, :nb].set(km[:, :, :, 0])
    o = _moba(q, k, v, km_pad, bsz, seq)
    x2 = _row_call(_lin_res_kernel, o, x2, g1, [l1_moba_w_o.astype(BF16)], seq)
    x2 = _moe(x2, l1_norm2_g, sc2, sh2, g2, l1_moe_w_router, l1_moe_b_router,
              l1_moe_w_gate_up, l1_moe_b_gate_up, l1_moe_w_down, l1_moe_b_down,
              final_norm_g, seq, final=True)
    return x2.reshape(bsz, seq, d)
```

```python
import functools
import math

import jax
import jax.numpy as jnp
from jax import lax
from jax.experimental import pallas as pl
from jax.experimental.pallas import tpu as pltpu

F32 = jnp.float32
BF16 = jnp.bfloat16
HI = lax.Precision.HIGHEST

EPS = 1e-6
NEG = -1e30
LOG2E = math.log2(math.e)
LANES = 128
MIB = 1024 * 1024

S5_GROUP = 16
S5_STATE = 64
N_HEADS = 8
HEAD_DIM = 128
MOBA_BLOCK = 256
MOBA_TOPK = 3
ROPE_THETA = 10000.0
N_EXPERTS = 32
TOP_K = 4
SWIGLU_LIMIT = 7.0
SWIGLU_ALPHA = 1.702

TM_DENSE = 512
S5_CHUNK = 128
S5_SUPER = 8
MOBA_HEADS_PER_STEP = 4
TM_EXPERT = 256
TD_ROWS = 128
TM_SRC = 2048


def _cparams(n_axes, vmem_mib):
    return pltpu.CompilerParams(dimension_semantics=("arbitrary",) * n_axes,
                                vmem_limit_bytes=vmem_mib * MIB)


def _normmod(x, g, sc, sh):
    ms = jnp.mean(x * x, axis=-1, keepdims=True)
    return (x * lax.rsqrt(ms + EPS)) * g * (1.0 + sc) + sh


def _ada_kernel(c_ref, w_ref, b_ref, o_ref):
    c = c_ref[...]
    ca = c * jax.nn.sigmoid(c)
    o_ref[...] = jnp.dot(ca, w_ref[...], preferred_element_type=F32, precision=HI) + b_ref[...]


def _ada(c_pad, w, b):
    d, n = w.shape
    tn = 1536
    return pl.pallas_call(
        _ada_kernel,
        grid=(n // tn,),
        in_specs=[pl.BlockSpec((8, d), lambda j: (0, 0)),
                  pl.BlockSpec((d, tn), lambda j: (0, j)),
                  pl.BlockSpec((1, tn), lambda j: (0, j))],
        out_specs=pl.BlockSpec((8, tn), lambda j: (0, j)),
        out_shape=jax.ShapeDtypeStruct((8, n), F32),
        compiler_params=_cparams(1, 32),
    )(c_pad, w, b.reshape(1, n))


def _nml_kernel(x_ref, g_ref, sc_ref, sh_ref, w_ref, o_ref):
    h = _normmod(x_ref[...], g_ref[...], sc_ref[0], sh_ref[0])
    o_ref[...] = jnp.dot(h.astype(BF16), w_ref[...], preferred_element_type=F32)


def _nml(x2, g, sc, sh, w_bf, seq):
    t, d = x2.shape
    n = w_bf.shape[1]
    tm = TM_DENSE
    tpb = seq // tm
    return pl.pallas_call(
        _nml_kernel,
        grid=(t // tm,),
        in_specs=[pl.BlockSpec((tm, d), lambda i: (i, 0)),
                  pl.BlockSpec((1, d), lambda i: (0, 0)),
                  pl.BlockSpec((1, 1, d), lambda i: (i // tpb, 0, 0)),
                  pl.BlockSpec((1, 1, d), lambda i: (i // tpb, 0, 0)),
                  pl.BlockSpec((d, n), lambda i: (0, 0))],
        out_specs=pl.BlockSpec((tm, n), lambda i: (i, 0)),
        out_shape=jax.ShapeDtypeStruct((t, n), F32),
        compiler_params=_cparams(1, 48),
    )(x2, g.reshape(1, d), sc, sh, w_bf)


def _gelu_tanh(y):
    c = math.sqrt(2.0 / math.pi)
    return y * (0.5 * (1.0 + jnp.tanh(c * (y + 0.044715 * (y * y * y)))))


def _s5_kernel(u_ref, rate_ref, theta_ref, bblk_ref, cblk_ref, d_ref, z_ref,
               apr, api, air, aii, xr, xi):
    L = u_ref.shape[0]
    n_super = bblk_ref.shape[0]
    half = apr.shape[1] // n_super
    nch = u_ref.shape[1] // n_super
    b = pl.program_id(0)
    c = pl.program_id(1)

    @pl.when((b == 0) & (c == 0))
    def _():
        tt = lax.broadcasted_iota(jnp.int32, (L, 1), 0).astype(F32)
        ph = theta_ref[...] * tt
        rt = rate_ref[...] * tt
        cs = jnp.cos(ph)
        sn = jnp.sin(ph)
        mag = jnp.exp(rt)
        inv = jnp.exp(-rt)
        apr[...] = mag * cs
        api[...] = mag * sn
        air[...] = inv * cs
        aii[...] = -(inv * sn)

    @pl.when(c == 0)
    def _():
        xr[...] = jnp.zeros_like(xr)
        xi[...] = jnp.zeros_like(xi)

    row = lax.broadcasted_iota(jnp.int32, (L, L), 0)
    col = lax.broadcasted_iota(jnp.int32, (L, L), 1)
    tril = (row >= col).astype(BF16)

    for sg in range(n_super):
        cols = slice(sg * half, (sg + 1) * half)
        ch = slice(sg * nch, (sg + 1) * nch)
        u = u_ref[:, ch]
        bu = jnp.dot(u.astype(BF16), bblk_ref[sg], preferred_element_type=F32)
        bur = bu[:, :half]
        bui = bu[:, half:]
        ir = air[:, cols]
        ii = aii[:, cols]
        zc = jnp.concatenate([bur * ir - bui * ii, bur * ii + bui * ir], axis=1)
        cum = jnp.dot(tril, zc.astype(BF16), preferred_element_type=F32)
        a_r = apr[1:2, cols]
        a_i = api[1:2, cols]
        p_r = xr[:, cols]
        p_i = xi[:, cols]
        cr = cum[:, :half] + (a_r * p_r - a_i * p_i)
        ci = cum[:, half:] + (a_r * p_i + a_i * p_r)
        pw_r = apr[:, cols]
        pw_i = api[:, cols]
        x_re = cr * pw_r - ci * pw_i
        x_im = cr * pw_i + ci * pw_r
        xr[:, cols] = x_re[L - 1:L, :]
        xi[:, cols] = x_im[L - 1:L, :]
        xc = jnp.concatenate([x_re, x_im], axis=1).astype(BF16)
        y = jnp.dot(xc, cblk_ref[sg], preferred_element_type=F32) + d_ref[:, ch] * u
        z_ref[:, ch] = _gelu_tanh(y)


def _s5_scan(u, rate, theta, bblk, cblk, d_skip, bsz, seq):
    t, w = u.shape
    L = S5_CHUNK
    nc = seq // L
    ns = rate.shape[1]
    return pl.pallas_call(
        _s5_kernel,
        grid=(bsz, nc),
        in_specs=[pl.BlockSpec((L, w), lambda b, c: (b * nc + c, 0)),
                  pl.BlockSpec((1, ns), lambda b, c: (0, 0)),
                  pl.BlockSpec((1, ns), lambda b, c: (0, 0)),
                  pl.BlockSpec(bblk.shape, lambda b, c: (0, 0, 0)),
                  pl.BlockSpec(cblk.shape, lambda b, c: (0, 0, 0)),
                  pl.BlockSpec((1, w), lambda b, c: (0, 0))],
        out_specs=pl.BlockSpec((L, w), lambda b, c: (b * nc + c, 0)),
        out_shape=jax.ShapeDtypeStruct((t, w), F32),
        scratch_shapes=[pltpu.VMEM((L, ns), F32)] * 4 + [pltpu.VMEM((1, ns), F32)] * 2,
        compiler_params=_cparams(2, 48),
    )(u, rate, theta, bblk, cblk, d_skip.reshape(1, w))


def _s5_params(b_re, b_im, c_re, c_im, lam_re, lam_im, log_dt):
    g, p, ch = b_re.shape
    dt = jnp.exp(log_dt)[:, None]
    rate = lam_re * dt
    theta = lam_im * dt
    mag = jnp.exp(rate)
    a_re = mag * jnp.cos(theta)
    a_im = mag * jnp.sin(theta)
    den = lam_re * lam_re + lam_im * lam_im
    f_re = ((a_re - 1.0) * lam_re + a_im * lam_im) / den
    f_im = (a_im * lam_re - (a_re - 1.0) * lam_im) / den
    bb_re = f_re[..., None] * b_re - f_im[..., None] * b_im
    bb_im = f_re[..., None] * b_im + f_im[..., None] * b_re
    ns = S5_SUPER
    nsg = g // ns
    eye = jnp.eye(ns, dtype=F32)

    def blk_b(bb):
        return jnp.einsum('sgpc,gh->sgchp', bb.reshape(nsg, ns, p, ch), eye).reshape(nsg, ns * ch, ns * p)

    def blk_c(cc):
        return jnp.einsum('sgcp,gh->sgphc', cc.reshape(nsg, ns, ch, p), eye).reshape(nsg, ns * p, ns * ch)

    bblk = jnp.concatenate([blk_b(bb_re), blk_b(bb_im)], axis=2).astype(BF16)
    cblk = jnp.concatenate([blk_c(c_re), -blk_c(c_im)], axis=1).astype(BF16)
    return rate.reshape(1, g * p), theta.reshape(1, g * p), bblk, cblk


def _glu_out_kernel(z_ref, x_ref, g_ref, wg_ref, wo_ref, o_ref):
    z = z_ref[...]
    gate = jax.nn.sigmoid(jnp.dot(z.astype(BF16), wg_ref[...], preferred_element_type=F32))
    m = jnp.dot((z * gate).astype(BF16), wo_ref[...], preferred_element_type=F32)
    o_ref[...] = x_ref[...] + g_ref[0] * m


def _lin_res_kernel(a_ref, x_ref, g_ref, w_ref, o_ref):
    m = jnp.dot(a_ref[...].astype(BF16), w_ref[...], preferred_element_type=F32)
    o_ref[...] = x_ref[...] + g_ref[0] * m


def _row_call(kernel, acts, x2, gate, weights, seq):
    t, d = x2.shape
    tm = TM_DENSE
    tpb = seq // tm
    row = pl.BlockSpec((tm, d), lambda i: (i, 0))
    return pl.pallas_call(
        kernel,
        grid=(t // tm,),
        in_specs=[row, row, pl.BlockSpec((1, 1, d), lambda i: (i // tpb, 0, 0))]
                 + [pl.BlockSpec(w.shape, lambda i: (0, 0)) for w in weights],
        out_specs=row,
        out_shape=jax.ShapeDtypeStruct((t, d), F32),
        compiler_params=_cparams(1, 48),
    )(acts, x2, gate, *weights)


def _rope_tab_kernel(pos_ref, inv_ref, sgn_ref, cos_ref, sin_ref):
    ang = pos_ref[...].astype(F32) * inv_ref[...]
    cos_ref[...] = jnp.cos(ang)
    sin_ref[...] = jnp.sin(ang) * sgn_ref[...]


def _rope_tables(pos_col):
    t = pos_col.shape[0]
    ts = 512
    half = HEAD_DIM // 2
    inv = ROPE_THETA ** (-jnp.arange(0, HEAD_DIM, 2, dtype=F32) / HEAD_DIM)
    inv = jnp.concatenate([inv, inv]).reshape(1, HEAD_DIM)
    sgn = jnp.concatenate([-jnp.ones((half,), F32), jnp.ones((half,), F32)]).reshape(1, HEAD_DIM)
    const = pl.BlockSpec((1, HEAD_DIM), lambda i: (0, 0))
    tile = pl.BlockSpec((ts, HEAD_DIM), lambda i: (i, 0))
    return pl.pallas_call(
        _rope_tab_kernel,
        grid=(t // ts,),
        in_specs=[pl.BlockSpec((ts, 1), lambda i: (i, 0)), const, const],
        out_specs=[tile, tile],
        out_shape=[jax.ShapeDtypeStruct((t, HEAD_DIM), F32)] * 2,
        compiler_params=_cparams(1, 16),
    )(pos_col, inv, sgn)


def _rope_kernel(q_ref, k_ref, v_ref, cos_ref, sin_ref, qo_ref, ko_ref, vo_ref, km_ref):
    cs = cos_ref[...]
    sn = sin_ref[...]
    dh = cs.shape[1]
    half = dh // 2
    for hh in range(qo_ref.shape[1]):
        cols = slice(hh * dh, (hh + 1) * dh)
        q = q_ref[:, cols]
        k = k_ref[:, cols]
        qr = q * cs + pltpu.roll(q, half, 1) * sn
        kr = k * cs + pltpu.roll(k, half, 1) * sn
        qo_ref[0, hh] = qr
        ko_ref[0, hh] = kr.astype(BF16)
        vo_ref[0, hh, 0] = v_ref[:, cols].T.astype(BF16)
        km_ref[0, hh, 0] = jnp.mean(kr, axis=0, keepdims=True)


def _rope(qkv, cos, sin, bsz, seq):
    t = qkv.shape[0]
    h, dh, blk = N_HEADS, HEAD_DIM, MOBA_BLOCK
    nb = seq // blk
    part = lambda p: pl.BlockSpec((blk, h * dh), lambda i: (i, p))
    tab = pl.BlockSpec((blk, dh), lambda i: (i, 0))
    head = pl.BlockSpec((1, h, blk, dh), lambda i: (i // nb, 0, i % nb, 0))
    return pl.pallas_call(
        _rope_kernel,
        grid=(t // blk,),
        in_specs=[part(0), part(1), part(2), tab, tab],
        out_specs=[head, head,
                   pl.BlockSpec((1, h, 1, dh, blk), lambda i: (i // nb, 0, i % nb, 0, 0)),
                   pl.BlockSpec((1, h, 1, 1, dh), lambda i: (i // nb, 0, i % nb, 0, 0))],
        out_shape=[jax.ShapeDtypeStruct((bsz, h, seq, dh), F32),
                   jax.ShapeDtypeStruct((bsz, h, seq, dh), BF16),
                   jax.ShapeDtypeStruct((bsz, h, nb, dh, blk), BF16),
                   jax.ShapeDtypeStruct((bsz, h, nb, 1, dh), F32)],
        compiler_params=_cparams(1, 32),
    )(qkv, qkv, qkv, cos, sin)


def _moba_kernel(q_ref, k_ref, vt_ref, km_ref, o_ref, sel_scr):
    nh = q_ref.shape[1]
    blk = q_ref.shape[2]
    dh = q_ref.shape[3]
    j = pl.program_id(2)
    nt = (((1,), (1,)), ((), ()))
    heads = range(nh)

    qs = []
    for hh in heads:
        q = q_ref[0, hh]
        gate = lax.dot_general(km_ref[0, hh], q, nt, preferred_element_type=F32, precision=HI)
        bidx = lax.broadcasted_iota(jnp.int32, gate.shape, 0)
        bidx_f = bidx.astype(F32)
        work = jnp.where(bidx < j, gate, NEG)
        sel = jnp.zeros(gate.shape, F32)
        for _ in range(MOBA_TOPK):
            m = jnp.max(work, axis=0, keepdims=True)
            idx = jnp.min(jnp.where(work == m, bidx_f, float(LANES)), axis=0, keepdims=True)
            hit = bidx_f == idx
            sel = jnp.where(hit & (m > 0.5 * NEG), 1.0, sel)
            work = jnp.where(hit, 2.0 * NEG, work)
        sel_scr[hh] = sel
        qs.append((q * (dh ** -0.5 * LOG2E)).astype(BF16))

    def scores(hh, n):
        start = pl.multiple_of(n * blk, blk)
        kb = k_ref[0, hh, pl.ds(start, blk), :]
        return lax.dot_general(kb, qs[hh], nt, preferred_element_type=F32)

    def past_pair(hh, nn, m_i, l_i, acc):
        n0 = 2 * nn
        n1 = n0 + 1
        s0 = scores(hh, n0)
        s1 = scores(hh, n1)
        r0 = sel_scr[hh, pl.ds(n0, 1), :] > 0.5
        r1 = sel_scr[hh, pl.ds(n1, 1), :] > 0.5
        rm0 = jnp.where(r0, jnp.max(s0, axis=0, keepdims=True), NEG)
        rm1 = jnp.where(r1, jnp.max(s1, axis=0, keepdims=True), NEG)
        m_new = jnp.maximum(m_i, jnp.maximum(rm0, rm1))
        alpha = jnp.exp2(m_i - m_new)
        p0 = jnp.exp2(s0 - jnp.where(r0, m_new, -NEG))
        p1 = jnp.exp2(s1 - jnp.where(r1, m_new, -NEG))
        l_new = alpha * l_i + (jnp.sum(p0, axis=0, keepdims=True) + jnp.sum(p1, axis=0, keepdims=True))
        pv = (jnp.dot(vt_ref[0, hh, n0], p0.astype(BF16), preferred_element_type=F32)
              + jnp.dot(vt_ref[0, hh, n1], p1.astype(BF16), preferred_element_type=F32))
        return m_new, l_new, alpha * acc + pv

    def body(nn, carry):
        return tuple(past_pair(hh, nn, *carry[hh]) for hh in heads)

    init = tuple((jnp.full((1, blk), NEG, F32), jnp.zeros((1, blk), F32), jnp.zeros((dh, blk), F32))
                 for _ in heads)
    carry = lax.fori_loop(0, (j + 1) // 2, body, init)

    for hh in heads:
        m_i, l_i, acc = carry[hh]
        s = scores(hh, j)
        kk = lax.broadcasted_iota(jnp.int32, s.shape, 0)
        qq = lax.broadcasted_iota(jnp.int32, s.shape, 1)
        s = jnp.where(kk <= qq, s, NEG)
        m_new = jnp.maximum(m_i, jnp.max(s, axis=0, keepdims=True))
        alpha = jnp.exp2(m_i - m_new)
        p = jnp.exp2(s - m_new)
        l_i = alpha * l_i + jnp.sum(p, axis=0, keepdims=True)
        acc = alpha * acc + jnp.dot(vt_ref[0, hh, j], p.astype(BF16), preferred_element_type=F32)
        o_ref[:, hh * dh:(hh + 1) * dh] = (acc / l_i).T


def _moba(q, k, vt, km_pad, bsz, seq):
    h, dh, blk = N_HEADS, HEAD_DIM, MOBA_BLOCK
    nb = seq // blk
    nh = MOBA_HEADS_PER_STEP
    return pl.pallas_call(
        _moba_kernel,
        grid=(bsz, h // nh, nb),
        in_specs=[pl.BlockSpec((1, nh, blk, dh), lambda b, j, i: (b, j, i, 0)),
                  pl.BlockSpec((1, nh, seq, dh), lambda b, j, i: (b, j, 0, 0)),
                  pl.BlockSpec((1, nh, nb, dh, blk), lambda b, j, i: (b, j, 0, 0, 0)),
                  pl.BlockSpec((1, nh, LANES, dh), lambda b, j, i: (b, j, 0, 0))],
        out_specs=pl.BlockSpec((blk, nh * dh), lambda b, j, i: (b * nb + i, j)),
        out_shape=jax.ShapeDtypeStruct((bsz * seq, h * dh), F32),
        scratch_shapes=[pltpu.VMEM((nh, LANES, blk), F32)],
        compiler_params=_cparams(3, 48),
    )(q, k, vt, km_pad)


def _router_kernel(x_ref, g_ref, sc_ref, sh_ref, wr_ref, br_ref, h_ref, oh_ref, ei_ref, gt_ref):
    h = _normmod(x_ref[...], g_ref[...], sc_ref[0], sh_ref[0])
    h_ref[...] = h
    logits = jnp.dot(h, wr_ref[...], preferred_element_type=F32, precision=HI) + br_ref[...]
    lane = lax.broadcasted_iota(jnp.int32, logits.shape, 1)
    lane_f = lane.astype(F32)
    work = logits
    onehot = jnp.zeros(logits.shape, F32)
    ei = jnp.zeros(logits.shape, F32)
    vals = []
    for kk in range(TOP_K):
        m = jnp.max(work, axis=-1, keepdims=True)
        idx = jnp.min(jnp.where(work == m, lane_f, float(LANES)), axis=-1, keepdims=True)
        hit = lane_f == idx
        onehot = jnp.where(hit, 1.0, onehot)
        ei = jnp.where(lane == kk, idx, ei)
        vals.append(m)
        work = jnp.where(hit, 2.0 * NEG, work)
    ex = [jnp.exp(vv - vals[0]) for vv in vals]
    tot = ex[0] + ex[1] + ex[2] + ex[3]
    gt = jnp.zeros(logits.shape, F32)
    for kk in range(TOP_K):
        gt = jnp.where(lane == kk, ex[kk] / tot, gt)
    oh_ref[...] = onehot
    ei_ref[...] = ei
    gt_ref[...] = gt


def _router(x2, g, sc, sh, wr_pad, br_pad, seq):
    t, d = x2.shape
    tm = TM_DENSE
    tpb = seq // tm
    lane_tile = pl.BlockSpec((tm, LANES), lambda i: (i, 0))
    return pl.pallas_call(
        _router_kernel,
        grid=(t // tm,),
        in_specs=[pl.BlockSpec((tm, d), lambda i: (i, 0)),
                  pl.BlockSpec((1, d), lambda i: (0, 0)),
                  pl.BlockSpec((1, 1, d), lambda i: (i // tpb, 0, 0)),
                  pl.BlockSpec((1, 1, d), lambda i: (i // tpb, 0, 0)),
                  pl.BlockSpec((d, LANES), lambda i: (0, 0)),
                  pl.BlockSpec((1, LANES), lambda i: (0, 0))],
        out_specs=[pl.BlockSpec((tm, d), lambda i: (i, 0)), lane_tile, lane_tile, lane_tile],
        out_shape=[jax.ShapeDtypeStruct((t, d), F32)] + [jax.ShapeDtypeStruct((t, LANES), F32)] * 3,
        compiler_params=_cparams(1, 32),
    )(x2, g.reshape(1, d), sc, sh, wr_pad, br_pad)


def _pos_kernel(oh_ref, ei_ref, dest_ref, meta_ref, cnt_scr, off_scr, run_scr):
    p = pl.program_id(0)
    i = pl.program_id(1)
    tm = oh_ref.shape[0]
    oh = oh_ref[...]
    colsum = jnp.sum(oh, axis=0, keepdims=True)

    @pl.when((p == 0) & (i == 0))
    def _():
        cnt_scr[...] = jnp.zeros_like(cnt_scr)

    @pl.when(p == 0)
    def _():
        cnt_scr[...] += colsum
        dest_ref[...] = jnp.zeros_like(dest_ref)

    @pl.when((p == 1) & (i == 0))
    def _():
        cnt = cnt_scr[...]
        padded = jnp.ceil(cnt * (1.0 / TM_EXPERT)) * float(TM_EXPERT)
        r = lax.broadcasted_iota(jnp.int32, (LANES, LANES), 0)
        cidx = lax.broadcasted_iota(jnp.int32, (LANES, LANES), 1)
        tri = (r < cidx).astype(F32)
        off = jnp.dot(jnp.broadcast_to(padded, (8, LANES)), tri,
                      preferred_element_type=F32, precision=HI)
        off_scr[...] = off[0:1]
        run_scr[...] = jnp.zeros_like(run_scr)

    @pl.when(p == 1)
    def _():
        r = lax.broadcasted_iota(jnp.int32, (tm, tm), 0)
        cidx = lax.broadcasted_iota(jnp.int32, (tm, tm), 1)
        ltri = (r > cidx).astype(BF16)
        excl = jnp.dot(ltri, oh.astype(BF16), preferred_element_type=F32)
        dfull = off_scr[...] + run_scr[...] + excl
        ei = ei_ref[...]
        lane = lax.broadcasted_iota(jnp.int32, oh.shape, 1)
        lane_f = lane.astype(F32)
        dest = jnp.zeros(oh.shape, F32)
        for kk in range(TOP_K):
            ek = jnp.sum(jnp.where(lane == kk, ei, 0.0), axis=-1, keepdims=True)
            dk = jnp.sum(jnp.where(lane_f == ek, dfull, 0.0), axis=-1, keepdims=True)
            dest = jnp.where(lane == kk, dk, dest)
        dest_ref[...] = dest.astype(jnp.int32)
        run_scr[...] += colsum

    meta_ref[...] = jnp.concatenate([off_scr[...], cnt_scr[...],
                                     jnp.zeros((6, LANES), F32)], axis=0)


def _positions(onehot, ei):
    t = onehot.shape[0]
    tm = TM_DENSE
    tile = pl.BlockSpec((tm, LANES), lambda p, i: (i, 0))
    return pl.pallas_call(
        _pos_kernel,
        grid=(2, t // tm),
        in_specs=[tile, tile],
        out_specs=[pl.BlockSpec((tm, LANES), lambda p, i: (i * p, 0)),
                   pl.BlockSpec((8, LANES), lambda p, i: (0, 0))],
        out_shape=[jax.ShapeDtypeStruct((t, LANES), jnp.int32),
                   jax.ShapeDtypeStruct((8, LANES), F32)],
        scratch_shapes=[pltpu.VMEM((1, LANES), F32)] * 3,
        compiler_params=_cparams(2, 16),
    )(onehot, ei)


def _src_kernel(dest_ref, src_ref):
    i = pl.program_id(0)
    tb = dest_ref.shape[0] // TOP_K

    @pl.when(i == 0)
    def _():
        def zero(j, carry):
            src_ref[j] = 0
            return carry
        lax.fori_loop(0, src_ref.shape[0], zero, 0, unroll=16)

    def body(tt, carry):
        for kk in range(TOP_K):
            src_ref[dest_ref[tt * TOP_K + kk]] = i * tb + tt
        return carry

    lax.fori_loop(0, tb, body, 0, unroll=4)


def _sources(dest_flat, n_rows):
    n = dest_flat.shape[0]
    tb = TM_SRC
    return pl.pallas_call(
        _src_kernel,
        grid=(n // (tb * TOP_K),),
        in_specs=[pl.BlockSpec((tb * TOP_K,), lambda i: (i,), memory_space=pltpu.SMEM)],
        out_specs=pl.BlockSpec(memory_space=pltpu.SMEM),
        out_shape=jax.ShapeDtypeStruct((n_rows,), jnp.int32),
        compiler_params=_cparams(1, 16),
    )(dest_flat)


def _expert_kernel(te_ref, tv_ref, src_cur, src_nxt, h_hbm, wgu_ref, bgu_ref, wd_ref, bd_ref,
                   o_ref, xbuf, sem, wgu_bf, wd_bf):
    i = pl.program_id(0)
    nt = pl.num_programs(0)
    tm = o_ref.shape[0]
    slot = lax.rem(i, 2)
    e = te_ref[i]
    prev = te_ref[jnp.maximum(i - 1, 0)]
    dff = wd_ref.shape[1]

    def gather(src_ref, s):
        def body(r8, carry):
            for uu in range(8):
                r = r8 * 8 + uu
                pltpu.make_async_copy(h_hbm.at[pl.ds(src_ref[r], 1)],
                                      xbuf.at[s, pl.ds(r, 1)], sem.at[s]).start()
            return carry
        lax.fori_loop(0, tm // 8, body, 0)

    @pl.when((i == 0) & (tv_ref[0] > 0))
    def _():
        gather(src_cur, 0)

    nxt = jnp.minimum(i + 1, nt - 1)

    @pl.when((i + 1 < nt) & (tv_ref[nxt] > 0))
    def _():
        gather(src_nxt, 1 - slot)

    @pl.when((i == 0) | (e != prev))
    def _():
        wgu_bf[...] = wgu_ref[0].astype(BF16)
        wd_bf[...] = wd_ref[0].astype(BF16)

    @pl.when(tv_ref[i] > 0)
    def _():
        pltpu.make_async_copy(h_hbm.at[pl.ds(0, tm)], xbuf.at[slot], sem.at[slot]).wait()
        x = xbuf[slot].astype(BF16)
        hu = jnp.dot(x, wgu_bf[...], preferred_element_type=F32) + bgu_ref[0]
        g = jnp.minimum(hu[:, :dff], SWIGLU_LIMIT)
        up = jnp.clip(hu[:, dff:], -SWIGLU_LIMIT, SWIGLU_LIMIT)
        act = (up + 1.0) * g * jax.nn.sigmoid(SWIGLU_ALPHA * g)
        o_ref[...] = jnp.dot(act.astype(BF16), wd_bf[...], preferred_element_type=F32) + bd_ref[0]

    @pl.when(tv_ref[i] == 0)
    def _():
        o_ref[...] = jnp.zeros_like(o_ref)


def _experts(tile_e, tile_v, src, h2, w_gu, b_gu, w_d, b_d):
    n_rows = src.shape[0]
    d = h2.shape[1]
    ne, _, f2 = w_gu.shape
    dff = w_d.shape[1]
    tm = TM_EXPERT
    nt = n_rows // tm
    grid_spec = pltpu.PrefetchScalarGridSpec(
        num_scalar_prefetch=2,
        grid=(nt,),
        in_specs=[pl.BlockSpec((tm,), lambda i, te, tv: (i,), memory_space=pltpu.SMEM),
                  pl.BlockSpec((tm,), lambda i, te, tv: (jnp.minimum(i + 1, nt - 1),),
                               memory_space=pltpu.SMEM),
                  pl.BlockSpec(memory_space=pl.ANY),
                  pl.BlockSpec((1, d, f2), lambda i, te, tv: (te[i], 0, 0)),
                  pl.BlockSpec((1, 1, f2), lambda i, te, tv: (te[i], 0, 0)),
                  pl.BlockSpec((1, dff, d), lambda i, te, tv: (te[i], 0, 0)),
                  pl.BlockSpec((1, 1, d), lambda i, te, tv: (te[i], 0, 0))],
        out_specs=pl.BlockSpec((tm, d), lambda i, te, tv: (i, 0)),
        scratch_shapes=[pltpu.VMEM((2, tm, d), F32), pltpu.SemaphoreType.DMA((2,)),
                        pltpu.VMEM((d, f2), BF16), pltpu.VMEM((dff, d), BF16)],
    )
    return pl.pallas_call(
        _expert_kernel,
        grid_spec=grid_spec,
        out_shape=jax.ShapeDtypeStruct((n_rows, d), F32),
        compiler_params=_cparams(1, 56),
    )(tile_e, tile_v, src, src, h2, w_gu, b_gu.reshape(ne, 1, f2), w_d, b_d.reshape(ne, 1, d))


def _combine_kernel(dcur, dnxt, gt_ref, x_ref, g_ref, gf_ref, ys_hbm, o_ref, buf, sem, *, final):
    i = pl.program_id(0)
    nt = pl.num_programs(0)
    td = x_ref.shape[0]
    slot = lax.rem(i, 2)

    def gather(dref, s):
        def body(t2, carry):
            for uu in range(2):
                tt = t2 * 2 + uu
                for kk in range(TOP_K):
                    pltpu.make_async_copy(ys_hbm.at[pl.ds(dref[tt * TOP_K + kk], 1)],
                                          buf.at[s, kk, pl.ds(tt, 1)], sem.at[s]).start()
            return carry
        lax.fori_loop(0, td // 2, body, 0)

    @pl.when(i == 0)
    def _():
        gather(dcur, 0)

    @pl.when(i + 1 < nt)
    def _():
        gather(dnxt, 1 - slot)

    for kk in range(TOP_K):
        pltpu.make_async_copy(ys_hbm.at[pl.ds(0, td)], buf.at[slot, kk], sem.at[slot]).wait()
    gt = gt_ref[...]
    y = gt[:, 0:1] * buf[slot, 0]
    for kk in range(1, TOP_K):
        y = y + gt[:, kk:kk + 1] * buf[slot, kk]
    xn = x_ref[...] + g_ref[0] * y
    if final:
        ms = jnp.mean(xn * xn, axis=-1, keepdims=True)
        xn = (xn * lax.rsqrt(ms + EPS)) * gf_ref[...]
    o_ref[...] = xn


def _combine(dest_flat, gates, x2, gate_vec, gf, ys, seq, final):
    t, d = x2.shape
    td = TD_ROWS
    tpb = seq // td
    nt = t // td
    return pl.pallas_call(
        functools.partial(_combine_kernel, final=final),
        grid=(nt,),
        in_specs=[pl.BlockSpec((td * TOP_K,), lambda i: (i,), memory_space=pltpu.SMEM),
                  pl.BlockSpec((td * TOP_K,), lambda i: (jnp.minimum(i + 1, nt - 1),),
                               memory_space=pltpu.SMEM),
                  pl.BlockSpec((td, LANES), lambda i: (i, 0)),
                  pl.BlockSpec((td, d), lambda i: (i, 0)),
                  pl.BlockSpec((1, 1, d), lambda i: (i // tpb, 0, 0)),
                  pl.BlockSpec((1, d), lambda i: (0, 0)),
                  pl.BlockSpec(memory_space=pl.ANY)],
        out_specs=pl.BlockSpec((td, d), lambda i: (i, 0)),
        out_shape=jax.ShapeDtypeStruct((t, d), F32),
        scratch_shapes=[pltpu.VMEM((2, TOP_K, td, d), F32), pltpu.SemaphoreType.DMA((2,))],
        compiler_params=_cparams(1, 32),
    )(dest_flat, dest_flat, gates, x2, gate_vec, gf.reshape(1, d), ys)


def _moe(x2, norm_g, sc, sh, gate_vec, w_router, b_router, w_gu, b_gu, w_d, b_d, gf, seq, final):
    t, d = x2.shape
    ne = w_router.shape[1]
    wr_pad = jnp.zeros((d, LANES), F32).at[:, :ne].set(w_router)
    br_pad = jnp.full((1, LANES), NEG, F32).at[0, :ne].set(b_router)
    h2, onehot, ei, gates = _router(x2, norm_g, sc, sh, wr_pad, br_pad, seq)
    dest, meta = _positions(onehot, ei)
    dest_flat = dest[:, :TOP_K].reshape(t * TOP_K)

    tm = TM_EXPERT
    n_rows = t * TOP_K + ne * tm
    n_tiles = n_rows // tm
    off = meta[0, :ne].astype(jnp.int32)
    cnt = meta[1, :ne].astype(jnp.int32)
    ends = off + ((cnt + tm - 1) // tm) * tm
    starts = jnp.arange(n_tiles, dtype=jnp.int32) * tm
    valid = starts < ends[-1]
    te = jnp.sum((starts[:, None] >= ends[None, :]).astype(jnp.int32), axis=1)
    te_last = jnp.sum((ends[-1] - 1 >= ends).astype(jnp.int32))
    tile_e = jnp.minimum(jnp.where(valid, te, te_last), ne - 1)
    tile_v = valid.astype(jnp.int32)

    src = _sources(dest_flat, n_rows)
    ys = _experts(tile_e, tile_v, src, h2, w_gu, b_gu, w_d, b_d)
    return _combine(dest_flat, gates, x2, gate_vec, gf, ys, seq, final)


def kernel(x, c, positions, l0_norm1_g, l0_ada_w, l0_ada_b, l0_s5_w_in, l0_s5_b_re, l0_s5_b_im, l0_s5_c_re, l0_s5_c_im, l0_s5_lam_re, l0_s5_lam_im, l0_s5_log_dt, l0_s5_d, l0_s5_w_glu, l0_s5_w_out, l0_norm2_g, l0_moe_w_router, l0_moe_b_router, l0_moe_w_gate_up, l0_moe_b_gate_up, l0_moe_w_down, l0_moe_b_down, l1_norm1_g, l1_ada_w, l1_ada_b, l1_moba_w_qkv, l1_moba_w_o, l1_norm2_g, l1_moe_w_router, l1_moe_b_router, l1_moe_w_gate_up, l1_moe_b_gate_up, l1_moe_w_down, l1_moe_b_down, final_norm_g):
    bsz, seq, d = x.shape
    t = bsz * seq
    assert seq % TM_DENSE == 0 and seq % MOBA_BLOCK == 0 and seq % S5_CHUNK == 0
    assert seq // MOBA_BLOCK <= LANES and d == N_HEADS * HEAD_DIM
    x2 = x.reshape(t, d)
    c_pad = jnp.zeros((8, d), F32).at[:bsz].set(c)

    def ada_parts(w, b):
        a = _ada(c_pad, w, b)[:bsz]
        return [a[:, i * d:(i + 1) * d].reshape(bsz, 1, d) for i in range(6)]

    sh1, sc1, g1, sh2, sc2, g2 = ada_parts(l0_ada_w, l0_ada_b)
    u = _nml(x2, l0_norm1_g, sc1, sh1, l0_s5_w_in.astype(BF16), seq)
    rate, theta, bblk, cblk = _s5_params(l0_s5_b_re, l0_s5_b_im, l0_s5_c_re, l0_s5_c_im,
                                         l0_s5_lam_re, l0_s5_lam_im, l0_s5_log_dt)
    z = _s5_scan(u, rate, theta, bblk, cblk, l0_s5_d, bsz, seq)
    x2 = _row_call(_glu_out_kernel, z, x2, g1,
                   [l0_s5_w_glu.astype(BF16), l0_s5_w_out.astype(BF16)], seq)
    x2 = _moe(x2, l0_norm2_g, sc2, sh2, g2, l0_moe_w_router, l0_moe_b_router,
              l0_moe_w_gate_up, l0_moe_b_gate_up, l0_moe_w_down, l0_moe_b_down,
              final_norm_g, seq, final=False)

    sh1, sc1, g1, sh2, sc2, g2 = ada_parts(l1_ada_w, l1_ada_b)
    qkv = _nml(x2, l1_norm1_g, sc1, sh1, l1_moba_w_qkv.astype(BF16), seq)
    cos, sin = _rope_tables(positions.reshape(t, 1))
    q, k, v, km = _rope(qkv, cos, sin, bsz, seq)
    nb = seq // MOBA_BLOCK
    km_pad = jnp.zeros((bsz, N_HEADS, LANES, HEAD_DIM), F32).at[:, :, :nb].set(km[:, :, :, 0])
    o = _moba(q, k, v, km_pad, bsz, seq)
    x2 = _row_call(_lin_res_kernel, o, x2, g1, [l1_moba_w_o.astype(BF16)], seq)
    x2 = _moe(x2, l1_norm2_g, sc2, sh2, g2, l1_moe_w_router, l1_moe_b_router,
              l1_moe_w_gate_up, l1_moe_b_gate_up, l1_moe_w_down, l1_moe_b_down,
              final_norm_g, seq, final=True)
    return x2.reshape(bsz, seq, d)
```

```python
import functools
import math

import jax
import jax.numpy as jnp
from jax import lax
from jax.experimental import pallas as pl
from jax.experimental.pallas import tpu as pltpu

F32 = jnp.float32
BF16 = jnp.bfloat16
HI = lax.Precision.HIGHEST

EPS = 1e-6
NEG = -1e30
LOG2E = math.log2(math.e)
LANES = 128
MIB = 1024 * 1024

S5_GROUP = 16
S5_STATE = 64
N_HEADS = 8
HEAD_DIM = 128
MOBA_BLOCK = 256
MOBA_TOPK = 3
ROPE_THETA = 10000.0
N_EXPERTS = 32
TOP_K = 4
SWIGLU_LIMIT = 7.0
SWIGLU_ALPHA = 1.702

TM_DENSE = 512
S5_CHUNK = 128
S5_SUPER = 8
MOBA_HEADS_PER_STEP = 4
TM_EXPERT = 256
EXPERT_CHUNKS = 4
TD_ROWS = 128
TM_SRC = 2048


def _cparams(n_axes, vmem_mib):
    return pltpu.CompilerParams(dimension_semantics=("arbitrary",) * n_axes,
                                vmem_limit_bytes=vmem_mib * MIB)


def _pack_rows(x):
    n = x.shape[1] // 2
    xb = x.astype(BF16).astype(F32)
    lo = lax.bitcast_convert_type(xb[:, :n], jnp.uint32) >> 16
    hi = lax.bitcast_convert_type(xb[:, n:], jnp.uint32) & jnp.uint32(0xFFFF0000)
    return lo | hi


def _unpack_rows(p):
    lo = lax.bitcast_convert_type(p << 16, F32)
    hi = lax.bitcast_convert_type(p & jnp.uint32(0xFFFF0000), F32)
    return jnp.concatenate([lo, hi], axis=1)


def _normmod(x, g, sc, sh):
    ms = jnp.mean(x * x, axis=-1, keepdims=True)
    return (x * lax.rsqrt(ms + EPS)) * g * (1.0 + sc) + sh


def _ada_kernel(c_ref, w_ref, b_ref, o_ref):
    c = c_ref[...]
    ca = c * jax.nn.sigmoid(c)
    o_ref[...] = jnp.dot(ca, w_ref[...], preferred_element_type=F32, precision=HI) + b_ref[...]


def _ada(c_pad, w, b):
    d, n = w.shape
    tn = 1536
    return pl.pallas_call(
        _ada_kernel,
        grid=(n // tn,),
        in_specs=[pl.BlockSpec((8, d), lambda j: (0, 0)),
                  pl.BlockSpec((d, tn), lambda j: (0, j)),
                  pl.BlockSpec((1, tn), lambda j: (0, j))],
        out_specs=pl.BlockSpec((8, tn), lambda j: (0, j)),
        out_shape=jax.ShapeDtypeStruct((8, n), F32),
        compiler_params=_cparams(1, 32),
    )(c_pad, w, b.reshape(1, n))


def _nml_kernel(x_ref, g_ref, sc_ref, sh_ref, w_ref, o_ref):
    h = _normmod(x_ref[...], g_ref[...], sc_ref[0], sh_ref[0])
    o_ref[...] = jnp.dot(h.astype(BF16), w_ref[...], preferred_element_type=F32)


def _nml(x2, g, sc, sh, w_bf, seq):
    t, d = x2.shape
    n = w_bf.shape[1]
    tm = TM_DENSE
    tpb = seq // tm
    return pl.pallas_call(
        _nml_kernel,
        grid=(t // tm,),
        in_specs=[pl.BlockSpec((tm, d), lambda i: (i, 0)),
                  pl.BlockSpec((1, d), lambda i: (0, 0)),
                  pl.BlockSpec((1, 1, d), lambda i: (i // tpb, 0, 0)),
                  pl.BlockSpec((1, 1, d), lambda i: (i // tpb, 0, 0)),
                  pl.BlockSpec((d, n), lambda i: (0, 0))],
        out_specs=pl.BlockSpec((tm, n), lambda i: (i, 0)),
        out_shape=jax.ShapeDtypeStruct((t, n), F32),
        compiler_params=_cparams(1, 48),
    )(x2, g.reshape(1, d), sc, sh, w_bf)


def _gelu_tanh(y):
    c = math.sqrt(2.0 / math.pi)
    return y * (0.5 * (1.0 + jnp.tanh(c * (y + 0.044715 * (y * y * y)))))


def _s5_kernel(u_ref, rate_ref, theta_ref, bblk_ref, cblk_ref, d_ref, z_ref,
               apr, api, air, aii, xr, xi):
    L = u_ref.shape[0]
    n_super = bblk_ref.shape[0]
    half = apr.shape[1] // n_super
    nch = u_ref.shape[1] // n_super
    b = pl.program_id(0)
    c = pl.program_id(1)

    @pl.when((b == 0) & (c == 0))
    def _():
        tt = lax.broadcasted_iota(jnp.int32, (L, 1), 0).astype(F32)
        ph = theta_ref[...] * tt
        rt = rate_ref[...] * tt
        cs = jnp.cos(ph)
        sn = jnp.sin(ph)
        mag = jnp.exp(rt)
        inv = jnp.exp(-rt)
        apr[...] = mag * cs
        api[...] = mag * sn
        air[...] = inv * cs
        aii[...] = -(inv * sn)

    @pl.when(c == 0)
    def _():
        xr[...] = jnp.zeros_like(xr)
        xi[...] = jnp.zeros_like(xi)

    row = lax.broadcasted_iota(jnp.int32, (L, L), 0)
    col = lax.broadcasted_iota(jnp.int32, (L, L), 1)
    tril = (row >= col).astype(BF16)

    for sg in range(n_super):
        cols = slice(sg * half, (sg + 1) * half)
        ch = slice(sg * nch, (sg + 1) * nch)
        u = u_ref[:, ch]
        bu = jnp.dot(u.astype(BF16), bblk_ref[sg], preferred_element_type=F32)
        bur = bu[:, :half]
        bui = bu[:, half:]
        ir = air[:, cols]
        ii = aii[:, cols]
        zc = jnp.concatenate([bur * ir - bui * ii, bur * ii + bui * ir], axis=1)
        cum = jnp.dot(tril, zc.astype(BF16), preferred_element_type=F32)
        a_r = apr[1:2, cols]
        a_i = api[1:2, cols]
        p_r = xr[:, cols]
        p_i = xi[:, cols]
        cr = cum[:, :half] + (a_r * p_r - a_i * p_i)
        ci = cum[:, half:] + (a_r * p_i + a_i * p_r)
        pw_r = apr[:, cols]
        pw_i = api[:, cols]
        x_re = cr * pw_r - ci * pw_i
        x_im = cr * pw_i + ci * pw_r
        xr[:, cols] = x_re[L - 1:L, :]
        xi[:, cols] = x_im[L - 1:L, :]
        xc = jnp.concatenate([x_re, x_im], axis=1).astype(BF16)
        y = jnp.dot(xc, cblk_ref[sg], preferred_element_type=F32) + d_ref[:, ch] * u
        z_ref[:, ch] = _gelu_tanh(y)


def _s5_scan(u, rate, theta, bblk, cblk, d_skip, bsz, seq):
    t, w = u.shape
    L = S5_CHUNK
    nc = seq // L
    ns = rate.shape[1]
    return pl.pallas_call(
        _s5_kernel,
        grid=(bsz, nc),
        in_specs=[pl.BlockSpec((L, w), lambda b, c: (b * nc + c, 0)),
                  pl.BlockSpec((1, ns), lambda b, c: (0, 0)),
                  pl.BlockSpec((1, ns), lambda b, c: (0, 0)),
                  pl.BlockSpec(bblk.shape, lambda b, c: (0, 0, 0)),
                  pl.BlockSpec(cblk.shape, lambda b, c: (0, 0, 0)),
                  pl.BlockSpec((1, w), lambda b, c: (0, 0))],
        out_specs=pl.BlockSpec((L, w), lambda b, c: (b * nc + c, 0)),
        out_shape=jax.ShapeDtypeStruct((t, w), F32),
        scratch_shapes=[pltpu.VMEM((L, ns), F32)] * 4 + [pltpu.VMEM((1, ns), F32)] * 2,
        compiler_params=_cparams(2, 48),
    )(u, rate, theta, bblk, cblk, d_skip.reshape(1, w))


def _s5_params(b_re, b_im, c_re, c_im, lam_re, lam_im, log_dt):
    g, p, ch = b_re.shape
    dt = jnp.exp(log_dt)[:, None]
    rate = lam_re * dt
    theta = lam_im * dt
    mag = jnp.exp(rate)
    a_re = mag * jnp.cos(theta)
    a_im = mag * jnp.sin(theta)
    den = lam_re * lam_re + lam_im * lam_im
    f_re = ((a_re - 1.0) * lam_re + a_im * lam_im) / den
    f_im = (a_im * lam_re - (a_re - 1.0) * lam_im) / den
    bb_re = f_re[..., None] * b_re - f_im[..., None] * b_im
    bb_im = f_re[..., None] * b_im + f_im[..., None] * b_re
    ns = S5_SUPER
    nsg = g // ns
    eye = jnp.eye(ns, dtype=F32)

    def blk_b(bb):
        return jnp.einsum('sgpc,gh->sgchp', bb.reshape(nsg, ns, p, ch), eye).reshape(nsg, ns * ch, ns * p)

    def blk_c(cc):
        return jnp.einsum('sgcp,gh->sgphc', cc.reshape(nsg, ns, ch, p), eye).reshape(nsg, ns * p, ns * ch)

    bblk = jnp.concatenate([blk_b(bb_re), blk_b(bb_im)], axis=2).astype(BF16)
    cblk = jnp.concatenate([blk_c(c_re), -blk_c(c_im)], axis=1).astype(BF16)
    return rate.reshape(1, g * p), theta.reshape(1, g * p), bblk, cblk


def _glu_out_kernel(z_ref, x_ref, g_ref, wg_ref, wo_ref, o_ref):
    z = z_ref[...]
    gate = jax.nn.sigmoid(jnp.dot(z.astype(BF16), wg_ref[...], preferred_element_type=F32))
    m = jnp.dot((z * gate).astype(BF16), wo_ref[...], preferred_element_type=F32)
    o_ref[...] = x_ref[...] + g_ref[0] * m


def _lin_res_kernel(a_ref, x_ref, g_ref, w_ref, o_ref):
    m = jnp.dot(a_ref[...].astype(BF16), w_ref[...], preferred_element_type=F32)
    o_ref[...] = x_ref[...] + g_ref[0] * m


def _row_call(kernel, acts, x2, gate, weights, seq):
    t, d = x2.shape
    tm = TM_DENSE
    tpb = seq // tm
    row = pl.BlockSpec((tm, d), lambda i: (i, 0))
    return pl.pallas_call(
        kernel,
        grid=(t // tm,),
        in_specs=[row, row, pl.BlockSpec((1, 1, d), lambda i: (i // tpb, 0, 0))]
                 + [pl.BlockSpec(w.shape, lambda i: (0, 0)) for w in weights],
        out_specs=row,
        out_shape=jax.ShapeDtypeStruct((t, d), F32),
        compiler_params=_cparams(1, 48),
    )(acts, x2, gate, *weights)


def _rope_tab_kernel(pos_ref, inv_ref, sgn_ref, cos_ref, sin_ref):
    ang = pos_ref[...].astype(F32) * inv_ref[...]
    cos_ref[...] = jnp.cos(ang)
    sin_ref[...] = jnp.sin(ang) * sgn_ref[...]


def _rope_tables(pos_col):
    t = pos_col.shape[0]
    ts = 512
    half = HEAD_DIM // 2
    inv = ROPE_THETA ** (-jnp.arange(0, HEAD_DIM, 2, dtype=F32) / HEAD_DIM)
    inv = jnp.concatenate([inv, inv]).reshape(1, HEAD_DIM)
    sgn = jnp.concatenate([-jnp.ones((half,), F32), jnp.ones((half,), F32)]).reshape(1, HEAD_DIM)
    const = pl.BlockSpec((1, HEAD_DIM), lambda i: (0, 0))
    tile = pl.BlockSpec((ts, HEAD_DIM), lambda i: (i, 0))
    return pl.pallas_call(
        _rope_tab_kernel,
        grid=(t // ts,),
        in_specs=[pl.BlockSpec((ts, 1), lambda i: (i, 0)), const, const],
        out_specs=[tile, tile],
        out_shape=[jax.ShapeDtypeStruct((t, HEAD_DIM), F32)] * 2,
        compiler_params=_cparams(1, 16),
    )(pos_col, inv, sgn)


def _rope_kernel(q_ref, k_ref, v_ref, cos_ref, sin_ref, qo_ref, ko_ref, vo_ref, km_ref):
    cs = cos_ref[...]
    sn = sin_ref[...]
    dh = cs.shape[1]
    half = dh // 2
    for hh in range(qo_ref.shape[1]):
        cols = slice(hh * dh, (hh + 1) * dh)
        q = q_ref[:, cols]
        k = k_ref[:, cols]
        qr = q * cs + pltpu.roll(q, half, 1) * sn
        kr = k * cs + pltpu.roll(k, half, 1) * sn
        qo_ref[0, hh] = qr
        ko_ref[0, hh] = kr.astype(BF16)
        vo_ref[0, hh, 0] = v_ref[:, cols].T.astype(BF16)
        km_ref[0, hh, 0] = jnp.mean(kr, axis=0, keepdims=True)


def _rope(qkv, cos, sin, bsz, seq):
    t = qkv.shape[0]
    h, dh, blk = N_HEADS, HEAD_DIM, MOBA_BLOCK
    nb = seq // blk
    part = lambda p: pl.BlockSpec((blk, h * dh), lambda i: (i, p))
    tab = pl.BlockSpec((blk, dh), lambda i: (i, 0))
    head = pl.BlockSpec((1, h, blk, dh), lambda i: (i // nb, 0, i % nb, 0))
    return pl.pallas_call(
        _rope_kernel,
        grid=(t // blk,),
        in_specs=[part(0), part(1), part(2), tab, tab],
        out_specs=[head, head,
                   pl.BlockSpec((1, h, 1, dh, blk), lambda i: (i // nb, 0, i % nb, 0, 0)),
                   pl.BlockSpec((1, h, 1, 1, dh), lambda i: (i // nb, 0, i % nb, 0, 0))],
        out_shape=[jax.ShapeDtypeStruct((bsz, h, seq, dh), F32),
                   jax.ShapeDtypeStruct((bsz, h, seq, dh), BF16),
                   jax.ShapeDtypeStruct((bsz, h, nb, dh, blk), BF16),
                   jax.ShapeDtypeStruct((bsz, h, nb, 1, dh), F32)],
        compiler_params=_cparams(1, 32),
    )(qkv, qkv, qkv, cos, sin)


def _moba_kernel(q_ref, k_ref, vt_ref, km_ref, o_ref, sel_scr):
    nh = q_ref.shape[1]
    blk = q_ref.shape[2]
    dh = q_ref.shape[3]
    j = pl.program_id(2)
    nt = (((1,), (1,)), ((), ()))
    heads = range(nh)

    qs = []
    for hh in heads:
        q = q_ref[0, hh]
        gate = lax.dot_general(km_ref[0, hh], q, nt, preferred_element_type=F32, precision=HI)
        bidx = lax.broadcasted_iota(jnp.int32, gate.shape, 0)
        bidx_f = bidx.astype(F32)
        work = jnp.where(bidx < j, gate, NEG)
        sel = jnp.zeros(gate.shape, F32)
        for _ in range(MOBA_TOPK):
            m = jnp.max(work, axis=0, keepdims=True)
            idx = jnp.min(jnp.where(work == m, bidx_f, float(LANES)), axis=0, keepdims=True)
            hit = bidx_f == idx
            sel = jnp.where(hit & (m > 0.5 * NEG), 1.0, sel)
            work = jnp.where(hit, 2.0 * NEG, work)
        sel_scr[hh] = sel
        qs.append((q * (dh ** -0.5 * LOG2E)).astype(BF16))

    def scores(hh, n):
        start = pl.multiple_of(n * blk, blk)
        kb = k_ref[0, hh, pl.ds(start, blk), :]
        return lax.dot_general(kb, qs[hh], nt, preferred_element_type=F32)

    def past_pair(hh, nn, m_i, l_i, acc):
        n0 = 2 * nn
        n1 = n0 + 1
        s0 = scores(hh, n0)
        s1 = scores(hh, n1)
        r0 = sel_scr[hh, pl.ds(n0, 1), :] > 0.5
        r1 = sel_scr[hh, pl.ds(n1, 1), :] > 0.5
        rm0 = jnp.where(r0, jnp.max(s0, axis=0, keepdims=True), NEG)
        rm1 = jnp.where(r1, jnp.max(s1, axis=0, keepdims=True), NEG)
        m_new = jnp.maximum(m_i, jnp.maximum(rm0, rm1))
        alpha = jnp.exp2(m_i - m_new)
        p0 = jnp.exp2(s0 - jnp.where(r0, m_new, -NEG))
        p1 = jnp.exp2(s1 - jnp.where(r1, m_new, -NEG))
        l_new = alpha * l_i + (jnp.sum(p0, axis=0, keepdims=True) + jnp.sum(p1, axis=0, keepdims=True))
        pv = (jnp.dot(vt_ref[0, hh, n0], p0.astype(BF16), preferred_element_type=F32)
              + jnp.dot(vt_ref[0, hh, n1], p1.astype(BF16), preferred_element_type=F32))
        return m_new, l_new, alpha * acc + pv

    def body(nn, carry):
        return tuple(past_pair(hh, nn, *carry[hh]) for hh in heads)

    init = tuple((jnp.full((1, blk), NEG, F32), jnp.zeros((1, blk), F32), jnp.zeros((dh, blk), F32))
                 for _ in heads)
    carry = lax.fori_loop(0, (j + 1) // 2, body, init)

    for hh in heads:
        m_i, l_i, acc = carry[hh]
        s = scores(hh, j)
        kk = lax.broadcasted_iota(jnp.int32, s.shape, 0)
        qq = lax.broadcasted_iota(jnp.int32, s.shape, 1)
        s = jnp.where(kk <= qq, s, NEG)
        m_new = jnp.maximum(m_i, jnp.max(s, axis=0, keepdims=True))
        alpha = jnp.exp2(m_i - m_new)
        p = jnp.exp2(s - m_new)
        l_i = alpha * l_i + jnp.sum(p, axis=0, keepdims=True)
        acc = alpha * acc + jnp.dot(vt_ref[0, hh, j], p.astype(BF16), preferred_element_type=F32)
        o_ref[:, hh * dh:(hh + 1) * dh] = (acc / l_i).T.astype(o_ref.dtype)


def _moba(q, k, vt, km_pad, bsz, seq):
    h, dh, blk = N_HEADS, HEAD_DIM, MOBA_BLOCK
    nb = seq // blk
    nh = MOBA_HEADS_PER_STEP
    return pl.pallas_call(
        _moba_kernel,
        grid=(bsz, h // nh, nb),
        in_specs=[pl.BlockSpec((1, nh, blk, dh), lambda b, j, i: (b, j, i, 0)),
                  pl.BlockSpec((1, nh, seq, dh), lambda b, j, i: (b, j, 0, 0)),
                  pl.BlockSpec((1, nh, nb, dh, blk), lambda b, j, i: (b, j, 0, 0, 0)),
                  pl.BlockSpec((1, nh, LANES, dh), lambda b, j, i: (b, j, 0, 0))],
        out_specs=pl.BlockSpec((blk, nh * dh), lambda b, j, i: (b * nb + i, j)),
        out_shape=jax.ShapeDtypeStruct((bsz * seq, h * dh), BF16),
        scratch_shapes=[pltpu.VMEM((nh, LANES, blk), F32)],
        compiler_params=_cparams(3, 48),
    )(q, k, vt, km_pad)


def _router_kernel(x_ref, g_ref, sc_ref, sh_ref, wr_ref, br_ref, h_ref, oh_ref, ei_ref, gt_ref):
    h = _normmod(x_ref[...], g_ref[...], sc_ref[0], sh_ref[0])
    h_ref[...] = _pack_rows(h)
    logits = jnp.dot(h, wr_ref[...], preferred_element_type=F32, precision=HI) + br_ref[...]
    lane = lax.broadcasted_iota(jnp.int32, logits.shape, 1)
    lane_f = lane.astype(F32)
    work = logits
    onehot = jnp.zeros(logits.shape, F32)
    ei = jnp.zeros(logits.shape, F32)
    vals = []
    for kk in range(TOP_K):
        m = jnp.max(work, axis=-1, keepdims=True)
        idx = jnp.min(jnp.where(work == m, lane_f, float(LANES)), axis=-1, keepdims=True)
        hit = lane_f == idx
        onehot = jnp.where(hit, 1.0, onehot)
        ei = jnp.where(lane == kk, idx, ei)
        vals.append(m)
        work = jnp.where(hit, 2.0 * NEG, work)
    ex = [jnp.exp(vv - vals[0]) for vv in vals]
    tot = ex[0] + ex[1] + ex[2] + ex[3]
    gt = jnp.zeros(logits.shape, F32)
    for kk in range(TOP_K):
        gt = jnp.where(lane == kk, ex[kk] / tot, gt)
    oh_ref[...] = onehot
    ei_ref[...] = ei
    gt_ref[...] = gt


def _router(x2, g, sc, sh, wr_pad, br_pad, seq):
    t, d = x2.shape
    tm = TM_DENSE
    tpb = seq // tm
    lane_tile = pl.BlockSpec((tm, LANES), lambda i: (i, 0))
    return pl.pallas_call(
        _router_kernel,
        grid=(t // tm,),
        in_specs=[pl.BlockSpec((tm, d), lambda i: (i, 0)),
                  pl.BlockSpec((1, d), lambda i: (0, 0)),
                  pl.BlockSpec((1, 1, d), lambda i: (i // tpb, 0, 0)),
                  pl.BlockSpec((1, 1, d), lambda i: (i // tpb, 0, 0)),
                  pl.BlockSpec((d, LANES), lambda i: (0, 0)),
                  pl.BlockSpec((1, LANES), lambda i: (0, 0))],
        out_specs=[pl.BlockSpec((tm, d // 2), lambda i: (i, 0)), lane_tile, lane_tile, lane_tile],
        out_shape=[jax.ShapeDtypeStruct((t, d // 2), jnp.uint32)]
                  + [jax.ShapeDtypeStruct((t, LANES), F32)] * 3,
        compiler_params=_cparams(1, 32),
    )(x2, g.reshape(1, d), sc, sh, wr_pad, br_pad)


def _pos_kernel(oh_ref, ei_ref, dest_ref, meta_ref, cnt_scr, off_scr, run_scr):
    p = pl.program_id(0)
    i = pl.program_id(1)
    tm = oh_ref.shape[0]
    oh = oh_ref[...]
    colsum = jnp.sum(oh, axis=0, keepdims=True)

    @pl.when((p == 0) & (i == 0))
    def _():
        cnt_scr[...] = jnp.zeros_like(cnt_scr)

    @pl.when(p == 0)
    def _():
        cnt_scr[...] += colsum
        dest_ref[...] = jnp.zeros_like(dest_ref)

    @pl.when((p == 1) & (i == 0))
    def _():
        cnt = cnt_scr[...]
        padded = jnp.ceil(cnt * (1.0 / TM_EXPERT)) * float(TM_EXPERT)
        r = lax.broadcasted_iota(jnp.int32, (LANES, LANES), 0)
        cidx = lax.broadcasted_iota(jnp.int32, (LANES, LANES), 1)
        tri = (r < cidx).astype(F32)
        off = jnp.dot(jnp.broadcast_to(padded, (8, LANES)), tri,
                      preferred_element_type=F32, precision=HI)
        off_scr[...] = off[0:1]
        run_scr[...] = jnp.zeros_like(run_scr)

    @pl.when(p == 1)
    def _():
        r = lax.broadcasted_iota(jnp.int32, (tm, tm), 0)
        cidx = lax.broadcasted_iota(jnp.int32, (tm, tm), 1)
        ltri = (r > cidx).astype(BF16)
        excl = jnp.dot(ltri, oh.astype(BF16), preferred_element_type=F32)
        dfull = off_scr[...] + run_scr[...] + excl
        ei = ei_ref[...]
        lane = lax.broadcasted_iota(jnp.int32, oh.shape, 1)
        lane_f = lane.astype(F32)
        dest = jnp.zeros(oh.shape, F32)
        for kk in range(TOP_K):
            ek = jnp.sum(jnp.where(lane == kk, ei, 0.0), axis=-1, keepdims=True)
            dk = jnp.sum(jnp.where(lane_f == ek, dfull, 0.0), axis=-1, keepdims=True)
            dest = jnp.where(lane == kk, dk, dest)
        dest_ref[...] = dest.astype(jnp.int32)
        run_scr[...] += colsum

    meta_ref[...] = jnp.concatenate([off_scr[...], cnt_scr[...],
                                     jnp.zeros((6, LANES), F32)], axis=0)


def _positions(onehot, ei):
    t = onehot.shape[0]
    tm = TM_DENSE
    tile = pl.BlockSpec((tm, LANES), lambda p, i: (i, 0))
    return pl.pallas_call(
        _pos_kernel,
        grid=(2, t // tm),
        in_specs=[tile, tile],
        out_specs=[pl.BlockSpec((tm, LANES), lambda p, i: (i * p, 0)),
                   pl.BlockSpec((8, LANES), lambda p, i: (0, 0))],
        out_shape=[jax.ShapeDtypeStruct((t, LANES), jnp.int32),
                   jax.ShapeDtypeStruct((8, LANES), F32)],
        scratch_shapes=[pltpu.VMEM((1, LANES), F32)] * 3,
        compiler_params=_cparams(2, 16),
    )(onehot, ei)


def _src_kernel(dest_ref, src_ref):
    i = pl.program_id(0)
    tb = dest_ref.shape[0] // TOP_K

    @pl.when(i == 0)
    def _():
        def zero(j, carry):
            src_ref[j] = 0
            return carry
        lax.fori_loop(0, src_ref.shape[0], zero, 0, unroll=16)

    def body(tt, carry):
        for kk in range(TOP_K):
            src_ref[dest_ref[tt * TOP_K + kk]] = i * tb + tt
        return carry

    lax.fori_loop(0, tb, body, 0, unroll=4)


def _sources(dest_flat, n_rows):
    n = dest_flat.shape[0]
    tb = TM_SRC
    return pl.pallas_call(
        _src_kernel,
        grid=(n // (tb * TOP_K),),
        in_specs=[pl.BlockSpec((tb * TOP_K,), lambda i: (i,), memory_space=pltpu.SMEM)],
        out_specs=pl.BlockSpec(memory_space=pltpu.SMEM),
        out_shape=jax.ShapeDtypeStruct((n_rows,), jnp.int32),
        compiler_params=_cparams(1, 16),
    )(dest_flat)


def _expert_kernel(te_ref, tv_ref, src_cur, src_nxt, h_hbm, wgu_ref, bgu_ref, wd_ref, bd_ref,
                   o_ref, xbuf, sem, wgu_bf, wd_bf):
    i = pl.program_id(0)
    nt = pl.num_programs(0)
    tm = o_ref.shape[0]
    slot = lax.rem(i, 2)
    e = te_ref[i]
    prev = te_ref[jnp.maximum(i - 1, 0)]
    dff = wd_ref.shape[1]
    valid = tv_ref[i] > 0
    has_rows = (i == 0) | (tv_ref[jnp.maximum(i - 1, 0)] > 0)

    def row_copy(src_ref, r, s):
        return pltpu.make_async_copy(h_hbm.at[pl.ds(src_ref[r], 1)],
                                     xbuf.at[s, pl.ds(r, 1)], sem.at[s])

    def wait_tile(s):
        pltpu.make_async_copy(h_hbm.at[pl.ds(0, tm)], xbuf.at[s], sem.at[s]).wait()

    @pl.when(i == 0)
    def _():
        def body(r8, carry):
            for uu in range(8):
                row_copy(src_cur, r8 * 8 + uu, 0).start()
            return carry
        lax.fori_loop(0, tm // 8, body, 0)

    @pl.when((i == 0) | (e != prev))
    def _():
        wgu_bf[...] = wgu_ref[0].astype(BF16)
        wd_bf[...] = wd_ref[0].astype(BF16)

    @pl.when(valid)
    def _():
        wait_tile(slot)
        x = _unpack_rows(xbuf[slot]).astype(BF16)
        n_chunk = EXPERT_CHUNKS
        cw = dff // n_chunk
        rows = tm // n_chunk
        acc = None
        for cc in range(n_chunk):
            for r in range(cc * rows, (cc + 1) * rows):
                row_copy(src_nxt, r, 1 - slot).start()
            gcol = slice(cc * cw, (cc + 1) * cw)
            ucol = slice(dff + cc * cw, dff + (cc + 1) * cw)
            g = jnp.dot(x, wgu_bf[:, gcol], preferred_element_type=F32) + bgu_ref[0][:, gcol]
            up = jnp.dot(x, wgu_bf[:, ucol], preferred_element_type=F32) + bgu_ref[0][:, ucol]
            g = jnp.minimum(g, SWIGLU_LIMIT)
            up = jnp.clip(up, -SWIGLU_LIMIT, SWIGLU_LIMIT)
            act = (up + 1.0) * g * jax.nn.sigmoid(SWIGLU_ALPHA * g)
            part = jnp.dot(act.astype(BF16), wd_bf[gcol, :], preferred_element_type=F32)
            acc = part if acc is None else acc + part
        o_ref[...] = _pack_rows(acc + bd_ref[0])

        @pl.when(i == nt - 1)
        def _():
            wait_tile(1 - slot)

    @pl.when(jnp.logical_not(valid))
    def _():
        @pl.when(has_rows)
        def _():
            wait_tile(slot)
        o_ref[...] = jnp.zeros_like(o_ref)


def _experts(tile_e, tile_v, src, h2p, w_gu, b_gu, w_d, b_d):
    n_rows = src.shape[0]
    dp = h2p.shape[1]
    ne, d, f2 = w_gu.shape
    dff = w_d.shape[1]
    tm = TM_EXPERT
    nt = n_rows // tm
    grid_spec = pltpu.PrefetchScalarGridSpec(
        num_scalar_prefetch=2,
        grid=(nt,),
        in_specs=[pl.BlockSpec((tm,), lambda i, te, tv: (i,), memory_space=pltpu.SMEM),
                  pl.BlockSpec((tm,), lambda i, te, tv: (jnp.minimum(i + 1, nt - 1),),
                               memory_space=pltpu.SMEM),
                  pl.BlockSpec(memory_space=pl.ANY),
                  pl.BlockSpec((1, d, f2), lambda i, te, tv: (te[i], 0, 0)),
                  pl.BlockSpec((1, 1, f2), lambda i, te, tv: (te[i], 0, 0)),
                  pl.BlockSpec((1, dff, d), lambda i, te, tv: (te[i], 0, 0)),
                  pl.BlockSpec((1, 1, d), lambda i, te, tv: (te[i], 0, 0))],
        out_specs=pl.BlockSpec((tm, dp), lambda i, te, tv: (i, 0)),
        scratch_shapes=[pltpu.VMEM((2, tm, dp), jnp.uint32), pltpu.SemaphoreType.DMA((2,)),
                        pltpu.VMEM((d, f2), BF16), pltpu.VMEM((dff, d), BF16)],
    )
    return pl.pallas_call(
        _expert_kernel,
        grid_spec=grid_spec,
        out_shape=jax.ShapeDtypeStruct((n_rows, dp), jnp.uint32),
        compiler_params=_cparams(1, 56),
    )(tile_e, tile_v, src, src, h2p, w_gu, b_gu.reshape(ne, 1, f2), w_d, b_d.reshape(ne, 1, d))


def _combine_kernel(dcur, dnxt, gt_ref, x_ref, g_ref, gf_ref, ys_hbm, o_ref, buf, sem, *, final):
    i = pl.program_id(0)
    nt = pl.num_programs(0)
    td = x_ref.shape[0]
    slot = lax.rem(i, 2)

    def gather(dref, s):
        def body(t2, carry):
            for uu in range(2):
                tt = t2 * 2 + uu
                for kk in range(TOP_K):
                    pltpu.make_async_copy(ys_hbm.at[pl.ds(dref[tt * TOP_K + kk], 1)],
                                          buf.at[s, kk, pl.ds(tt, 1)], sem.at[s]).start()
            return carry
        lax.fori_loop(0, td // 2, body, 0)

    @pl.when(i == 0)
    def _():
        gather(dcur, 0)

    @pl.when(i + 1 < nt)
    def _():
        gather(dnxt, 1 - slot)

    for kk in range(TOP_K):
        pltpu.make_async_copy(ys_hbm.at[pl.ds(0, td)], buf.at[slot, kk], sem.at[slot]).wait()
    gt = gt_ref[...]
    y = gt[:, 0:1] * _unpack_rows(buf[slot, 0])
    for kk in range(1, TOP_K):
        y = y + gt[:, kk:kk + 1] * _unpack_rows(buf[slot, kk])
    xn = x_ref[...] + g_ref[0] * y
    if final:
        ms = jnp.mean(xn * xn, axis=-1, keepdims=True)
        xn = (xn * lax.rsqrt(ms + EPS)) * gf_ref[...]
    o_ref[...] = xn


def _combine(dest_flat, gates, x2, gate_vec, gf, ys, seq, final):
    t, d = x2.shape
    td = TD_ROWS
    tpb = seq // td
    nt = t // td
    return pl.pallas_call(
        functools.partial(_combine_kernel, final=final),
        grid=(nt,),
        in_specs=[pl.BlockSpec((td * TOP_K,), lambda i: (i,), memory_space=pltpu.SMEM),
                  pl.BlockSpec((td * TOP_K,), lambda i: (jnp.minimum(i + 1, nt - 1),),
                               memory_space=pltpu.SMEM),
                  pl.BlockSpec((td, LANES), lambda i: (i, 0)),
                  pl.BlockSpec((td, d), lambda i: (i, 0)),
                  pl.BlockSpec((1, 1, d), lambda i: (i // tpb, 0, 0)),
                  pl.BlockSpec((1, d), lambda i: (0, 0)),
                  pl.BlockSpec(memory_space=pl.ANY)],
        out_specs=pl.BlockSpec((td, d), lambda i: (i, 0)),
        out_shape=jax.ShapeDtypeStruct((t, d), F32),
        scratch_shapes=[pltpu.VMEM((2, TOP_K, td, d // 2), jnp.uint32), pltpu.SemaphoreType.DMA((2,))],
        compiler_params=_cparams(1, 32),
    )(dest_flat, dest_flat, gates, x2, gate_vec, gf.reshape(1, d), ys)


def _moe(x2, norm_g, sc, sh, gate_vec, w_router, b_router, w_gu, b_gu, w_d, b_d, gf, seq, final):
    t, d = x2.shape
    ne = w_router.shape[1]
    wr_pad = jnp.zeros((d, LANES), F32).at[:, :ne].set(w_router)
    br_pad = jnp.full((1, LANES), NEG, F32).at[0, :ne].set(b_router)
    h2, onehot, ei, gates = _router(x2, norm_g, sc, sh, wr_pad, br_pad, seq)
    dest, meta = _positions(onehot, ei)
    dest_flat = dest[:, :TOP_K].reshape(t * TOP_K)

    tm = TM_EXPERT
    n_rows = t * TOP_K + ne * tm
    n_tiles = n_rows // tm
    off = meta[0, :ne].astype(jnp.int32)
    cnt = meta[1, :ne].astype(jnp.int32)
    ends = off + ((cnt + tm - 1) // tm) * tm
    starts = jnp.arange(n_tiles, dtype=jnp.int32) * tm
    valid = starts < ends[-1]
    te = jnp.sum((starts[:, None] >= ends[None, :]).astype(jnp.int32), axis=1)
    te_last = jnp.sum((ends[-1] - 1 >= ends).astype(jnp.int32))
    tile_e = jnp.minimum(jnp.where(valid, te, te_last), ne - 1)
    tile_v = valid.astype(jnp.int32)

    src = _sources(dest_flat, n_rows)
    ys = _experts(tile_e, tile_v, src, h2, w_gu, b_gu, w_d, b_d)
    return _combine(dest_flat, gates, x2, gate_vec, gf, ys, seq, final)


def kernel(x, c, positions, l0_norm1_g, l0_ada_w, l0_ada_b, l0_s5_w_in, l0_s5_b_re, l0_s5_b_im, l0_s5_c_re, l0_s5_c_im, l0_s5_lam_re, l0_s5_lam_im, l0_s5_log_dt, l0_s5_d, l0_s5_w_glu, l0_s5_w_out, l0_norm2_g, l0_moe_w_router, l0_moe_b_router, l0_moe_w_gate_up, l0_moe_b_gate_up, l0_moe_w_down, l0_moe_b_down, l1_norm1_g, l1_ada_w, l1_ada_b, l1_moba_w_qkv, l1_moba_w_o, l1_norm2_g, l1_moe_w_router, l1_moe_b_router, l1_moe_w_gate_up, l1_moe_b_gate_up, l1_moe_w_down, l1_moe_b_down, final_norm_g):
    bsz, seq, d = x.shape
    t = bsz * seq
    assert seq % TM_DENSE == 0 and seq % MOBA_BLOCK == 0 and seq % S5_CHUNK == 0
    assert seq // MOBA_BLOCK <= LANES and d == N_HEADS * HEAD_DIM
    x2 = x.reshape(t, d)
    c_pad = jnp.zeros((8, d), F32).at[:bsz].set(c)

    def ada_parts(w, b):
        a = _ada(c_pad, w, b)[:bsz]
        return [a[:, i * d:(i + 1) * d].reshape(bsz, 1, d) for i in range(6)]

    sh1, sc1, g1, sh2, sc2, g2 = ada_parts(l0_ada_w, l0_ada_b)
    u = _nml(x2, l0_norm1_g, sc1, sh1, l0_s5_w_in.astype(BF16), seq)
    rate, theta, bblk, cblk = _s5_params(l0_s5_b_re, l0_s5_b_im, l0_s5_c_re, l0_s5_c_im,
                                         l0_s5_lam_re, l0_s5_lam_im, l0_s5_log_dt)
    z = _s5_scan(u, rate, theta, bblk, cblk, l0_s5_d, bsz, seq)
    x2 = _row_call(_glu_out_kernel, z, x2, g1,
                   [l0_s5_w_glu.astype(BF16), l0_s5_w_out.astype(BF16)], seq)
    x2 = _moe(x2, l0_norm2_g, sc2, sh2, g2, l0_moe_w_router, l0_moe_b_router,
              l0_moe_w_gate_up, l0_moe_b_gate_up, l0_moe_w_down, l0_moe_b_down,
              final_norm_g, seq, final=False)

    sh1, sc1, g1, sh2, sc2, g2 = ada_parts(l1_ada_w, l1_ada_b)
    qkv = _nml(x2, l1_norm1_g, sc1, sh1, l1_moba_w_qkv.astype(BF16), seq)
    cos, sin = _rope_tables(positions.reshape(t, 1))
    q, k, v, km = _rope(qkv, cos, sin, bsz, seq)
    nb = seq // MOBA_BLOCK
    km_pad = jnp.zeros((bsz, N_HEADS, LANES, HEAD_DIM), F32).at[:, :, :nb].set(km[:, :, :, 0])
    o = _moba(q, k, v, km_pad, bsz, seq)
    x2 = _row_call(_lin_res_kernel, o, x2, g1, [l1_moba_w_o.astype(BF16)], seq)
    x2 = _moe(x2, l1_norm2_g, sc2, sh2, g2, l1_moe_w_router, l1_moe_b_router,
              l1_moe_w_gate_up, l1_moe_b_gate_up, l1_moe_w_down, l1_moe_b_down,
              final_norm_g, seq, final=True)
    return x2.reshape(bsz, seq, d)
```

```python
import functools
import math

import jax
import jax.numpy as jnp
from jax import lax
from jax.experimental import pallas as pl
from jax.experimental.pallas import tpu as pltpu

F32 = jnp.float32
BF16 = jnp.bfloat16
HI = lax.Precision.HIGHEST

EPS = 1e-6
NEG = -1e30
LOG2E = math.log2(math.e)
LANES = 128
MIB = 1024 * 1024

S5_GROUP = 16
S5_STATE = 64
N_HEADS = 8
HEAD_DIM = 128
MOBA_BLOCK = 256
MOBA_TOPK = 3
ROPE_THETA = 10000.0
N_EXPERTS = 32
TOP_K = 4
SWIGLU_LIMIT = 7.0
SWIGLU_ALPHA = 1.702

TM_DENSE = 512
S5_CHUNK = 256
S5_SUPER = 8
MOBA_HEADS_PER_STEP = 4
TM_EXPERT = 256
TD_ROWS = 128
TM_SRC = 2048


def _cparams(n_axes, vmem_mib):
    return pltpu.CompilerParams(dimension_semantics=("arbitrary",) * n_axes,
                                vmem_limit_bytes=vmem_mib * MIB)


def _pack_rows(x):
    n = x.shape[1] // 2
    xb = x.astype(BF16).astype(F32)
    lo = lax.bitcast_convert_type(xb[:, :n], jnp.uint32) >> 16
    hi = lax.bitcast_convert_type(xb[:, n:], jnp.uint32) & jnp.uint32(0xFFFF0000)
    return lo | hi


def _unpack_rows(p):
    lo = lax.bitcast_convert_type(p << 16, F32)
    hi = lax.bitcast_convert_type(p & jnp.uint32(0xFFFF0000), F32)
    return jnp.concatenate([lo, hi], axis=1)


def _normmod(x, g, sc, sh):
    ms = jnp.mean(x * x, axis=-1, keepdims=True)
    return (x * lax.rsqrt(ms + EPS)) * g * (1.0 + sc) + sh


def _ada_kernel(c_ref, w_ref, b_ref, o_ref):
    c = c_ref[...]
    ca = c * jax.nn.sigmoid(c)
    o_ref[...] = jnp.dot(ca, w_ref[...], preferred_element_type=F32, precision=HI) + b_ref[...]


def _ada(c_pad, w, b):
    d, n = w.shape
    tn = 1536
    return pl.pallas_call(
        _ada_kernel,
        grid=(n // tn,),
        in_specs=[pl.BlockSpec((8, d), lambda j: (0, 0)),
                  pl.BlockSpec((d, tn), lambda j: (0, j)),
                  pl.BlockSpec((1, tn), lambda j: (0, j))],
        out_specs=pl.BlockSpec((8, tn), lambda j: (0, j)),
        out_shape=jax.ShapeDtypeStruct((8, n), F32),
        compiler_params=_cparams(1, 32),
    )(c_pad, w, b.reshape(1, n))


def _nml_kernel(x_ref, g_ref, sc_ref, sh_ref, w_ref, o_ref):
    h = _normmod(x_ref[...], g_ref[...], sc_ref[0], sh_ref[0])
    o_ref[...] = jnp.dot(h.astype(BF16), w_ref[...], preferred_element_type=F32)


def _nml(x2, g, sc, sh, w_bf, seq):
    t, d = x2.shape
    n = w_bf.shape[1]
    tm = TM_DENSE
    tpb = seq // tm
    return pl.pallas_call(
        _nml_kernel,
        grid=(t // tm,),
        in_specs=[pl.BlockSpec((tm, d), lambda i: (i, 0)),
                  pl.BlockSpec((1, d), lambda i: (0, 0)),
                  pl.BlockSpec((1, 1, d), lambda i: (i // tpb, 0, 0)),
                  pl.BlockSpec((1, 1, d), lambda i: (i // tpb, 0, 0)),
                  pl.BlockSpec((d, n), lambda i: (0, 0))],
        out_specs=pl.BlockSpec((tm, n), lambda i: (i, 0)),
        out_shape=jax.ShapeDtypeStruct((t, n), F32),
        compiler_params=_cparams(1, 48),
    )(x2, g.reshape(1, d), sc, sh, w_bf)


def _gelu_tanh(y):
    c = math.sqrt(2.0 / math.pi)
    return y * (0.5 * (1.0 + jnp.tanh(c * (y + 0.044715 * (y * y * y)))))


def _s5_kernel(u_ref, rate_ref, theta_ref, bblk_ref, cblk_ref, d_ref, z_ref,
               apr, api, air, aii, xr, xi):
    L = u_ref.shape[0]
    n_super = bblk_ref.shape[0]
    half = apr.shape[1] // n_super
    nch = u_ref.shape[1] // n_super
    b = pl.program_id(0)
    c = pl.program_id(1)

    @pl.when((b == 0) & (c == 0))
    def _():
        tt = lax.broadcasted_iota(jnp.int32, (L, 1), 0).astype(F32)
        ph = theta_ref[...] * tt
        rt = rate_ref[...] * tt
        cs = jnp.cos(ph)
        sn = jnp.sin(ph)
        mag = jnp.exp(rt)
        inv = jnp.exp(-rt)
        apr[...] = mag * cs
        api[...] = mag * sn
        air[...] = inv * cs
        aii[...] = -(inv * sn)

    @pl.when(c == 0)
    def _():
        xr[...] = jnp.zeros_like(xr)
        xi[...] = jnp.zeros_like(xi)

    row = lax.broadcasted_iota(jnp.int32, (L, L), 0)
    col = lax.broadcasted_iota(jnp.int32, (L, L), 1)
    tril = (row >= col).astype(BF16)

    for sg in range(n_super):
        cols = slice(sg * half, (sg + 1) * half)
        ch = slice(sg * nch, (sg + 1) * nch)
        u = u_ref[:, ch]
        bu = jnp.dot(u.astype(BF16), bblk_ref[sg], preferred_element_type=F32)
        bur = bu[:, :half]
        bui = bu[:, half:]
        ir = air[:, cols]
        ii = aii[:, cols]
        zc = jnp.concatenate([bur * ir - bui * ii, bur * ii + bui * ir], axis=1)
        cum = jnp.dot(tril, zc.astype(BF16), preferred_element_type=F32)
        a_r = apr[1:2, cols]
        a_i = api[1:2, cols]
        p_r = xr[:, cols]
        p_i = xi[:, cols]
        cr = cum[:, :half] + (a_r * p_r - a_i * p_i)
        ci = cum[:, half:] + (a_r * p_i + a_i * p_r)
        pw_r = apr[:, cols]
        pw_i = api[:, cols]
        x_re = cr * pw_r - ci * pw_i
        x_im = cr * pw_i + ci * pw_r
        xr[:, cols] = x_re[L - 1:L, :]
        xi[:, cols] = x_im[L - 1:L, :]
        xc = jnp.concatenate([x_re, x_im], axis=1).astype(BF16)
        y = jnp.dot(xc, cblk_ref[sg], preferred_element_type=F32) + d_ref[:, ch] * u
        z_ref[:, ch] = _gelu_tanh(y)


def _s5_scan(u, rate, theta, bblk, cblk, d_skip, bsz, seq):
    t, w = u.shape
    L = S5_CHUNK
    nc = seq // L
    ns = rate.shape[1]
    return pl.pallas_call(
        _s5_kernel,
        grid=(bsz, nc),
        in_specs=[pl.BlockSpec((L, w), lambda b, c: (b * nc + c, 0)),
                  pl.BlockSpec((1, ns), lambda b, c: (0, 0)),
                  pl.BlockSpec((1, ns), lambda b, c: (0, 0)),
                  pl.BlockSpec(bblk.shape, lambda b, c: (0, 0, 0)),
                  pl.BlockSpec(cblk.shape, lambda b, c: (0, 0, 0)),
                  pl.BlockSpec((1, w), lambda b, c: (0, 0))],
        out_specs=pl.BlockSpec((L, w), lambda b, c: (b * nc + c, 0)),
        out_shape=jax.ShapeDtypeStruct((t, w), F32),
        scratch_shapes=[pltpu.VMEM((L, ns), F32)] * 4 + [pltpu.VMEM((1, ns), F32)] * 2,
        compiler_params=_cparams(2, 58),
    )(u, rate, theta, bblk, cblk, d_skip.reshape(1, w))


def _s5_params(b_re, b_im, c_re, c_im, lam_re, lam_im, log_dt):
    g, p, ch = b_re.shape
    dt = jnp.exp(log_dt)[:, None]
    rate = lam_re * dt
    theta = lam_im * dt
    mag = jnp.exp(rate)
    a_re = mag * jnp.cos(theta)
    a_im = mag * jnp.sin(theta)
    den = lam_re * lam_re + lam_im * lam_im
    f_re = ((a_re - 1.0) * lam_re + a_im * lam_im) / den
    f_im = (a_im * lam_re - (a_re - 1.0) * lam_im) / den
    bb_re = f_re[..., None] * b_re - f_im[..., None] * b_im
    bb_im = f_re[..., None] * b_im + f_im[..., None] * b_re
    ns = S5_SUPER
    nsg = g // ns
    eye = jnp.eye(ns, dtype=F32)

    def blk_b(bb):
        return jnp.einsum('sgpc,gh->sgchp', bb.reshape(nsg, ns, p, ch), eye).reshape(nsg, ns * ch, ns * p)

    def blk_c(cc):
        return jnp.einsum('sgcp,gh->sgphc', cc.reshape(nsg, ns, ch, p), eye).reshape(nsg, ns * p, ns * ch)

    bblk = jnp.concatenate([blk_b(bb_re), blk_b(bb_im)], axis=2).astype(BF16)
    cblk = jnp.concatenate([blk_c(c_re), -blk_c(c_im)], axis=1).astype(BF16)
    return rate.reshape(1, g * p), theta.reshape(1, g * p), bblk, cblk


def _glu_out_kernel(z_ref, x_ref, g_ref, wg_ref, wo_ref, o_ref):
    z = z_ref[...]
    gate = jax.nn.sigmoid(jnp.dot(z.astype(BF16), wg_ref[...], preferred_element_type=F32))
    m = jnp.dot((z * gate).astype(BF16), wo_ref[...], preferred_element_type=F32)
    o_ref[...] = x_ref[...] + g_ref[0] * m


def _lin_res_kernel(a_ref, x_ref, g_ref, w_ref, o_ref):
    m = jnp.dot(a_ref[...].astype(BF16), w_ref[...], preferred_element_type=F32)
    o_ref[...] = x_ref[...] + g_ref[0] * m


def _row_call(kernel, acts, x2, gate, weights, seq):
    t, d = x2.shape
    tm = TM_DENSE
    tpb = seq // tm
    row = pl.BlockSpec((tm, d), lambda i: (i, 0))
    return pl.pallas_call(
        kernel,
        grid=(t // tm,),
        in_specs=[row, row, pl.BlockSpec((1, 1, d), lambda i: (i // tpb, 0, 0))]
                 + [pl.BlockSpec(w.shape, lambda i: (0, 0)) for w in weights],
        out_specs=row,
        out_shape=jax.ShapeDtypeStruct((t, d), F32),
        compiler_params=_cparams(1, 48),
    )(acts, x2, gate, *weights)


def _rope_tab_kernel(pos_ref, inv_ref, sgn_ref, cos_ref, sin_ref):
    ang = pos_ref[...].astype(F32) * inv_ref[...]
    cos_ref[...] = jnp.cos(ang)
    sin_ref[...] = jnp.sin(ang) * sgn_ref[...]


def _rope_tables(pos_col):
    t = pos_col.shape[0]
    ts = 512
    half = HEAD_DIM // 2
    inv = ROPE_THETA ** (-jnp.arange(0, HEAD_DIM, 2, dtype=F32) / HEAD_DIM)
    inv = jnp.concatenate([inv, inv]).reshape(1, HEAD_DIM)
    sgn = jnp.concatenate([-jnp.ones((half,), F32), jnp.ones((half,), F32)]).reshape(1, HEAD_DIM)
    const = pl.BlockSpec((1, HEAD_DIM), lambda i: (0, 0))
    tile = pl.BlockSpec((ts, HEAD_DIM), lambda i: (i, 0))
    return pl.pallas_call(
        _rope_tab_kernel,
        grid=(t // ts,),
        in_specs=[pl.BlockSpec((ts, 1), lambda i: (i, 0)), const, const],
        out_specs=[tile, tile],
        out_shape=[jax.ShapeDtypeStruct((t, HEAD_DIM), F32)] * 2,
        compiler_params=_cparams(1, 16),
    )(pos_col, inv, sgn)


def _rope_kernel(q_ref, k_ref, v_ref, cos_ref, sin_ref, qo_ref, ko_ref, vo_ref, km_ref):
    cs = cos_ref[...]
    sn = sin_ref[...]
    dh = cs.shape[1]
    half = dh // 2
    for hh in range(qo_ref.shape[1]):
        cols = slice(hh * dh, (hh + 1) * dh)
        q = q_ref[:, cols]
        k = k_ref[:, cols]
        qr = q * cs + pltpu.roll(q, half, 1) * sn
        kr = k * cs + pltpu.roll(k, half, 1) * sn
        qo_ref[0, hh] = qr
        ko_ref[0, hh] = kr.astype(BF16)
        vo_ref[0, hh, 0] = v_ref[:, cols].T.astype(BF16)
        km_ref[0, hh, 0] = jnp.mean(kr, axis=0, keepdims=True)


def _rope(qkv, cos, sin, bsz, seq):
    t = qkv.shape[0]
    h, dh, blk = N_HEADS, HEAD_DIM, MOBA_BLOCK
    nb = seq // blk
    part = lambda p: pl.BlockSpec((blk, h * dh), lambda i: (i, p))
    tab = pl.BlockSpec((blk, dh), lambda i: (i, 0))
    head = pl.BlockSpec((1, h, blk, dh), lambda i: (i // nb, 0, i % nb, 0))
    return pl.pallas_call(
        _rope_kernel,
        grid=(t // blk,),
        in_specs=[part(0), part(1), part(2), tab, tab],
        out_specs=[head, head,
                   pl.BlockSpec((1, h, 1, dh, blk), lambda i: (i // nb, 0, i % nb, 0, 0)),
                   pl.BlockSpec((1, h, 1, 1, dh), lambda i: (i // nb, 0, i % nb, 0, 0))],
        out_shape=[jax.ShapeDtypeStruct((bsz, h, seq, dh), F32),
                   jax.ShapeDtypeStruct((bsz, h, seq, dh), BF16),
                   jax.ShapeDtypeStruct((bsz, h, nb, dh, blk), BF16),
                   jax.ShapeDtypeStruct((bsz, h, nb, 1, dh), F32)],
        compiler_params=_cparams(1, 32),
    )(qkv, qkv, qkv, cos, sin)


def _moba_kernel(q_ref, k_ref, vt_ref, km_ref, o_ref, sel_scr):
    nh = q_ref.shape[1]
    blk = q_ref.shape[2]
    dh = q_ref.shape[3]
    j = pl.program_id(2)
    nt = (((1,), (1,)), ((), ()))
    heads = range(nh)

    qs = []
    for hh in heads:
        q = q_ref[0, hh]
        gate = lax.dot_general(km_ref[0, hh], q, nt, preferred_element_type=F32, precision=HI)
        bidx = lax.broadcasted_iota(jnp.int32, gate.shape, 0)
        bidx_f = bidx.astype(F32)
        work = jnp.where(bidx < j, gate, NEG)
        sel = jnp.zeros(gate.shape, F32)
        for _ in range(MOBA_TOPK):
            m = jnp.max(work, axis=0, keepdims=True)
            idx = jnp.min(jnp.where(work == m, bidx_f, float(LANES)), axis=0, keepdims=True)
            hit = bidx_f == idx
            sel = jnp.where(hit & (m > 0.5 * NEG), 1.0, sel)
            work = jnp.where(hit, 2.0 * NEG, work)
        sel_scr[hh] = sel
        qs.append((q * (dh ** -0.5 * LOG2E)).astype(BF16))

    def scores(hh, n):
        start = pl.multiple_of(n * blk, blk)
        kb = k_ref[0, hh, pl.ds(start, blk), :]
        return lax.dot_general(kb, qs[hh], nt, preferred_element_type=F32)

    def past_pair(hh, nn, m_i, l_i, acc):
        n0 = 2 * nn
        n1 = n0 + 1
        s0 = scores(hh, n0)
        s1 = scores(hh, n1)
        r0 = sel_scr[hh, pl.ds(n0, 1), :] > 0.5
        r1 = sel_scr[hh, pl.ds(n1, 1), :] > 0.5
        rm0 = jnp.where(r0, jnp.max(s0, axis=0, keepdims=True), NEG)
        rm1 = jnp.where(r1, jnp.max(s1, axis=0, keepdims=True), NEG)
        m_new = jnp.maximum(m_i, jnp.maximum(rm0, rm1))
        alpha = jnp.exp2(m_i - m_new)
        p0 = jnp.exp2(s0 - jnp.where(r0, m_new, -NEG))
        p1 = jnp.exp2(s1 - jnp.where(r1, m_new, -NEG))
        l_new = alpha * l_i + (jnp.sum(p0, axis=0, keepdims=True) + jnp.sum(p1, axis=0, keepdims=True))
        pv = (jnp.dot(vt_ref[0, hh, n0], p0.astype(BF16), preferred_element_type=F32)
              + jnp.dot(vt_ref[0, hh, n1], p1.astype(BF16), preferred_element_type=F32))
        return m_new, l_new, alpha * acc + pv

    def body(nn, carry):
        return tuple(past_pair(hh, nn, *carry[hh]) for hh in heads)

    init = tuple((jnp.full((1, blk), NEG, F32), jnp.zeros((1, blk), F32), jnp.zeros((dh, blk), F32))
                 for _ in heads)
    carry = lax.fori_loop(0, (j + 1) // 2, body, init)

    for hh in heads:
        m_i, l_i, acc = carry[hh]
        s = scores(hh, j)
        kk = lax.broadcasted_iota(jnp.int32, s.shape, 0)
        qq = lax.broadcasted_iota(jnp.int32, s.shape, 1)
        s = jnp.where(kk <= qq, s, NEG)
        m_new = jnp.maximum(m_i, jnp.max(s, axis=0, keepdims=True))
        alpha = jnp.exp2(m_i - m_new)
        p = jnp.exp2(s - m_new)
        l_i = alpha * l_i + jnp.sum(p, axis=0, keepdims=True)
        acc = alpha * acc + jnp.dot(vt_ref[0, hh, j], p.astype(BF16), preferred_element_type=F32)
        o_ref[:, hh * dh:(hh + 1) * dh] = (acc / l_i).T.astype(o_ref.dtype)


def _moba(q, k, vt, km_pad, bsz, seq):
    h, dh, blk = N_HEADS, HEAD_DIM, MOBA_BLOCK
    nb = seq // blk
    nbp = km_pad.shape[2]
    nh = MOBA_HEADS_PER_STEP
    return pl.pallas_call(
        _moba_kernel,
        grid=(bsz, h // nh, nb),
        in_specs=[pl.BlockSpec((1, nh, blk, dh), lambda b, j, i: (b, j, i, 0)),
                  pl.BlockSpec((1, nh, seq, dh), lambda b, j, i: (b, j, 0, 0)),
                  pl.BlockSpec((1, nh, nb, dh, blk), lambda b, j, i: (b, j, 0, 0, 0)),
                  pl.BlockSpec((1, nh, nbp, dh), lambda b, j, i: (b, j, 0, 0))],
        out_specs=pl.BlockSpec((blk, nh * dh), lambda b, j, i: (b * nb + i, j)),
        out_shape=jax.ShapeDtypeStruct((bsz * seq, h * dh), BF16),
        scratch_shapes=[pltpu.VMEM((nh, nbp, blk), F32)],
        compiler_params=_cparams(3, 48),
    )(q, k, vt, km_pad)


def _router_kernel(x_ref, g_ref, sc_ref, sh_ref, wr_ref, br_ref, h_ref, oh_ref, ei_ref, gt_ref):
    h = _normmod(x_ref[...], g_ref[...], sc_ref[0], sh_ref[0])
    h_ref[...] = _pack_rows(h)
    logits = jnp.dot(h, wr_ref[...], preferred_element_type=F32, precision=HI) + br_ref[...]
    lane = lax.broadcasted_iota(jnp.int32, logits.shape, 1)
    lane_f = lane.astype(F32)
    work = logits
    onehot = jnp.zeros(logits.shape, F32)
    ei = jnp.zeros(logits.shape, F32)
    vals = []
    for kk in range(TOP_K):
        m = jnp.max(work, axis=-1, keepdims=True)
        idx = jnp.min(jnp.where(work == m, lane_f, float(LANES)), axis=-1, keepdims=True)
        hit = lane_f == idx
        onehot = jnp.where(hit, 1.0, onehot)
        ei = jnp.where(lane == kk, idx, ei)
        vals.append(m)
        work = jnp.where(hit, 2.0 * NEG, work)
    ex = [jnp.exp(vv - vals[0]) for vv in vals]
    tot = ex[0] + ex[1] + ex[2] + ex[3]
    gt = jnp.zeros(logits.shape, F32)
    for kk in range(TOP_K):
        gt = jnp.where(lane == kk, ex[kk] / tot, gt)
    oh_ref[...] = onehot
    ei_ref[...] = ei
    gt_ref[...] = gt


def _router(x2, g, sc, sh, wr_pad, br_pad, seq):
    t, d = x2.shape
    tm = TM_DENSE
    tpb = seq // tm
    lane_tile = pl.BlockSpec((tm, LANES), lambda i: (i, 0))
    return pl.pallas_call(
        _router_kernel,
        grid=(t // tm,),
        in_specs=[pl.BlockSpec((tm, d), lambda i: (i, 0)),
                  pl.BlockSpec((1, d), lambda i: (0, 0)),
                  pl.BlockSpec((1, 1, d), lambda i: (i // tpb, 0, 0)),
                  pl.BlockSpec((1, 1, d), lambda i: (i // tpb, 0, 0)),
                  pl.BlockSpec((d, LANES), lambda i: (0, 0)),
                  pl.BlockSpec((1, LANES), lambda i: (0, 0))],
        out_specs=[pl.BlockSpec((tm, d // 2), lambda i: (i, 0)), lane_tile, lane_tile, lane_tile],
        out_shape=[jax.ShapeDtypeStruct((t, d // 2), jnp.uint32)]
                  + [jax.ShapeDtypeStruct((t, LANES), F32)] * 3,
        compiler_params=_cparams(1, 32),
    )(x2, g.reshape(1, d), sc, sh, wr_pad, br_pad)


def _pos_kernel(oh_ref, ei_ref, dest_ref, meta_ref, cnt_scr, off_scr, run_scr):
    p = pl.program_id(0)
    i = pl.program_id(1)
    tm = oh_ref.shape[0]
    oh = oh_ref[...]
    colsum = jnp.sum(oh, axis=0, keepdims=True)

    @pl.when((p == 0) & (i == 0))
    def _():
        cnt_scr[...] = jnp.zeros_like(cnt_scr)

    @pl.when(p == 0)
    def _():
        cnt_scr[...] += colsum
        dest_ref[...] = jnp.zeros_like(dest_ref)

    @pl.when((p == 1) & (i == 0))
    def _():
        cnt = cnt_scr[...]
        padded = jnp.ceil(cnt * (1.0 / TM_EXPERT)) * float(TM_EXPERT)
        r = lax.broadcasted_iota(jnp.int32, (LANES, LANES), 0)
        cidx = lax.broadcasted_iota(jnp.int32, (LANES, LANES), 1)
        tri = (r < cidx).astype(F32)
        off = jnp.dot(jnp.broadcast_to(padded, (8, LANES)), tri,
                      preferred_element_type=F32, precision=HI)
        off_scr[...] = off[0:1]
        run_scr[...] = jnp.zeros_like(run_scr)

    @pl.when(p == 1)
    def _():
        r = lax.broadcasted_iota(jnp.int32, (tm, tm), 0)
        cidx = lax.broadcasted_iota(jnp.int32, (tm, tm), 1)
        ltri = (r > cidx).astype(BF16)
        excl = jnp.dot(ltri, oh.astype(BF16), preferred_element_type=F32)
        dfull = off_scr[...] + run_scr[...] + excl
        ei = ei_ref[...]
        lane = lax.broadcasted_iota(jnp.int32, oh.shape, 1)
        lane_f = lane.astype(F32)
        dest = jnp.zeros(oh.shape, F32)
        for kk in range(TOP_K):
            ek = jnp.sum(jnp.where(lane == kk, ei, 0.0), axis=-1, keepdims=True)
            dk = jnp.sum(jnp.where(lane_f == ek, dfull, 0.0), axis=-1, keepdims=True)
            dest = jnp.where(lane == kk, dk, dest)
        dest_ref[...] = dest.astype(jnp.int32)
        run_scr[...] += colsum

    meta_ref[...] = jnp.concatenate([off_scr[...], cnt_scr[...],
                                     jnp.zeros((6, LANES), F32)], axis=0)


def _positions(onehot, ei):
    t = onehot.shape[0]
    tm = TM_DENSE
    tile = pl.BlockSpec((tm, LANES), lambda p, i: (i, 0))
    return pl.pallas_call(
        _pos_kernel,
        grid=(2, t // tm),
        in_specs=[tile, tile],
        out_specs=[pl.BlockSpec((tm, LANES), lambda p, i: (i * p, 0)),
                   pl.BlockSpec((8, LANES), lambda p, i: (0, 0))],
        out_shape=[jax.ShapeDtypeStruct((t, LANES), jnp.int32),
                   jax.ShapeDtypeStruct((8, LANES), F32)],
        scratch_shapes=[pltpu.VMEM((1, LANES), F32)] * 3,
        compiler_params=_cparams(2, 16),
    )(onehot, ei)


def _src_kernel(dest_ref, src_ref):
    i = pl.program_id(0)
    tb = dest_ref.shape[0] // TOP_K

    @pl.when(i == 0)
    def _():
        def zero(j, carry):
            src_ref[j] = 0
            return carry
        lax.fori_loop(0, src_ref.shape[0], zero, 0, unroll=16)

    def body(tt, carry):
        for kk in range(TOP_K):
            src_ref[dest_ref[tt * TOP_K + kk]] = i * tb + tt
        return carry

    lax.fori_loop(0, tb, body, 0, unroll=4)


def _sources(dest_flat, n_rows):
    n = dest_flat.shape[0]
    tb = TM_SRC
    return pl.pallas_call(
        _src_kernel,
        grid=(n // (tb * TOP_K),),
        in_specs=[pl.BlockSpec((tb * TOP_K,), lambda i: (i,), memory_space=pltpu.SMEM)],
        out_specs=pl.BlockSpec(memory_space=pltpu.SMEM),
        out_shape=jax.ShapeDtypeStruct((n_rows,), jnp.int32),
        compiler_params=_cparams(1, 16),
    )(dest_flat)


def _expert_kernel(te_ref, tv_ref, src_cur, src_nxt, h_hbm, wgu_ref, bgu_ref, wd_ref, bd_ref,
                   o_ref, xbuf, sem, wgu_bf, wd_bf):
    i = pl.program_id(0)
    nt = pl.num_programs(0)
    tm = o_ref.shape[0]
    slot = lax.rem(i, 2)
    e = te_ref[i]
    prev = te_ref[jnp.maximum(i - 1, 0)]
    dff = wd_ref.shape[1]
    def gather(src_ref, s):
        def body(r8, carry):
            for uu in range(8):
                r = r8 * 8 + uu
                pltpu.make_async_copy(h_hbm.at[pl.ds(src_ref[r], 1)],
                                      xbuf.at[s, pl.ds(r, 1)], sem.at[s]).start()
            return carry
        lax.fori_loop(0, tm // 8, body, 0)

    @pl.when((i == 0) & (tv_ref[0] > 0))
    def _():
        gather(src_cur, 0)

    nxt = jnp.minimum(i + 1, nt - 1)

    @pl.when((i + 1 < nt) & (tv_ref[nxt] > 0))
    def _():
        gather(src_nxt, 1 - slot)

    @pl.when((i == 0) | (e != prev))
    def _():
        wgu_bf[...] = wgu_ref[0].astype(BF16)
        wd_bf[...] = wd_ref[0].astype(BF16)

    @pl.when(tv_ref[i] > 0)
    def _():
        pltpu.make_async_copy(h_hbm.at[pl.ds(0, tm)], xbuf.at[slot], sem.at[slot]).wait()
        x = _unpack_rows(xbuf[slot]).astype(BF16)
        hu = jnp.dot(x, wgu_bf[...], preferred_element_type=F32) + bgu_ref[0]
        g = jnp.minimum(hu[:, :dff], SWIGLU_LIMIT)
        up = jnp.clip(hu[:, dff:], -SWIGLU_LIMIT, SWIGLU_LIMIT)
        act = (up + 1.0) * g * jax.nn.sigmoid(SWIGLU_ALPHA * g)
        y = jnp.dot(act.astype(BF16), wd_bf[...], preferred_element_type=F32) + bd_ref[0]
        o_ref[...] = _pack_rows(y)

    @pl.when(tv_ref[i] == 0)
    def _():
        o_ref[...] = jnp.zeros_like(o_ref)


def _experts(tile_e, tile_v, src, h2p, w_gu, b_gu, w_d, b_d):
    n_rows = src.shape[0]
    dp = h2p.shape[1]
    ne, d, f2 = w_gu.shape
    dff = w_d.shape[1]
    tm = TM_EXPERT
    nt = n_rows // tm
    grid_spec = pltpu.PrefetchScalarGridSpec(
        num_scalar_prefetch=2,
        grid=(nt,),
        in_specs=[pl.BlockSpec((tm,), lambda i, te, tv: (i,), memory_space=pltpu.SMEM),
                  pl.BlockSpec((tm,), lambda i, te, tv: (jnp.minimum(i + 1, nt - 1),),
                               memory_space=pltpu.SMEM),
                  pl.BlockSpec(memory_space=pl.ANY),
                  pl.BlockSpec((1, d, f2), lambda i, te, tv: (te[i], 0, 0)),
                  pl.BlockSpec((1, 1, f2), lambda i, te, tv: (te[i], 0, 0)),
                  pl.BlockSpec((1, dff, d), lambda i, te, tv: (te[i], 0, 0)),
                  pl.BlockSpec((1, 1, d), lambda i, te, tv: (te[i], 0, 0))],
        out_specs=pl.BlockSpec((tm, dp), lambda i, te, tv: (i, 0)),
        scratch_shapes=[pltpu.VMEM((2, tm, dp), jnp.uint32), pltpu.SemaphoreType.DMA((2,)),
                        pltpu.VMEM((d, f2), BF16), pltpu.VMEM((dff, d), BF16)],
    )
    return pl.pallas_call(
        _expert_kernel,
        grid_spec=grid_spec,
        out_shape=jax.ShapeDtypeStruct((n_rows, dp), jnp.uint32),
        compiler_params=_cparams(1, 56),
    )(tile_e, tile_v, src, src, h2p, w_gu, b_gu.reshape(ne, 1, f2), w_d, b_d.reshape(ne, 1, d))


def _combine_kernel(dcur, dnxt, gt_ref, x_ref, g_ref, gf_ref, ys_hbm, o_ref, buf, sem, *, final):
    i = pl.program_id(0)
    nt = pl.num_programs(0)
    td = x_ref.shape[0]
    slot = lax.rem(i, 2)

    def gather(dref, s):
        def body(t2, carry):
            for uu in range(2):
                tt = t2 * 2 + uu
                for kk in range(TOP_K):
                    pltpu.make_async_copy(ys_hbm.at[pl.ds(dref[tt * TOP_K + kk], 1)],
                                          buf.at[s, kk, pl.ds(tt, 1)], sem.at[s]).start()
            return carry
        lax.fori_loop(0, td // 2, body, 0)

    @pl.when(i == 0)
    def _():
        gather(dcur, 0)

    @pl.when(i + 1 < nt)
    def _():
        gather(dnxt, 1 - slot)

    for kk in range(TOP_K):
        pltpu.make_async_copy(ys_hbm.at[pl.ds(0, td)], buf.at[slot, kk], sem.at[slot]).wait()
    gt = gt_ref[...]
    y = gt[:, 0:1] * _unpack_rows(buf[slot, 0])
    for kk in range(1, TOP_K):
        y = y + gt[:, kk:kk + 1] * _unpack_rows(buf[slot, kk])
    xn = x_ref[...] + g_ref[0] * y
    if final:
        ms = jnp.mean(xn * xn, axis=-1, keepdims=True)
        xn = (xn * lax.rsqrt(ms + EPS)) * gf_ref[...]
    o_ref[...] = xn


def _combine(dest_flat, gates, x2, gate_vec, gf, ys, seq, final):
    t, d = x2.shape
    td = TD_ROWS
    tpb = seq // td
    nt = t // td
    return pl.pallas_call(
        functools.partial(_combine_kernel, final=final),
        grid=(nt,),
        in_specs=[pl.BlockSpec((td * TOP_K,), lambda i: (i,), memory_space=pltpu.SMEM),
                  pl.BlockSpec((td * TOP_K,), lambda i: (jnp.minimum(i + 1, nt - 1),),
                               memory_space=pltpu.SMEM),
                  pl.BlockSpec((td, LANES), lambda i: (i, 0)),
                  pl.BlockSpec((td, d), lambda i: (i, 0)),
                  pl.BlockSpec((1, 1, d), lambda i: (i // tpb, 0, 0)),
                  pl.BlockSpec((1, d), lambda i: (0, 0)),
                  pl.BlockSpec(memory_space=pl.ANY)],
        out_specs=pl.BlockSpec((td, d), lambda i: (i, 0)),
        out_shape=jax.ShapeDtypeStruct((t, d), F32),
        scratch_shapes=[pltpu.VMEM((2, TOP_K, td, d // 2), jnp.uint32), pltpu.SemaphoreType.DMA((2,))],
        compiler_params=_cparams(1, 32),
    )(dest_flat, dest_flat, gates, x2, gate_vec, gf.reshape(1, d), ys)


def _moe(x2, norm_g, sc, sh, gate_vec, w_router, b_router, w_gu, b_gu, w_d, b_d, gf, seq, final):
    t, d = x2.shape
    ne = w_router.shape[1]
    wr_pad = jnp.zeros((d, LANES), F32).at[:, :ne].set(w_router)
    br_pad = jnp.full((1, LANES), NEG, F32).at[0, :ne].set(b_router)
    h2, onehot, ei, gates = _router(x2, norm_g, sc, sh, wr_pad, br_pad, seq)
    dest, meta = _positions(onehot, ei)
    dest_flat = dest[:, :TOP_K].reshape(t * TOP_K)

    tm = TM_EXPERT
    n_rows = t * TOP_K + ne * tm
    n_tiles = n_rows // tm
    off = meta[0, :ne].astype(jnp.int32)
    cnt = meta[1, :ne].astype(jnp.int32)
    ends = off + ((cnt + tm - 1) // tm) * tm
    starts = jnp.arange(n_tiles, dtype=jnp.int32) * tm
    valid = starts < ends[-1]
    te = jnp.sum((starts[:, None] >= ends[None, :]).astype(jnp.int32), axis=1)
    te_last = jnp.sum((ends[-1] - 1 >= ends).astype(jnp.int32))
    tile_e = jnp.minimum(jnp.where(valid, te, te_last), ne - 1)
    tile_v = valid.astype(jnp.int32)

    src = _sources(dest_flat, n_rows)
    ys = _experts(tile_e, tile_v, src, h2, w_gu, b_gu, w_d, b_d)
    return _combine(dest_flat, gates, x2, gate_vec, gf, ys, seq, final)


def kernel(x, c, positions, l0_norm1_g, l0_ada_w, l0_ada_b, l0_s5_w_in, l0_s5_b_re, l0_s5_b_im, l0_s5_c_re, l0_s5_c_im, l0_s5_lam_re, l0_s5_lam_im, l0_s5_log_dt, l0_s5_d, l0_s5_w_glu, l0_s5_w_out, l0_norm2_g, l0_moe_w_router, l0_moe_b_router, l0_moe_w_gate_up, l0_moe_b_gate_up, l0_moe_w_down, l0_moe_b_down, l1_norm1_g, l1_ada_w, l1_ada_b, l1_moba_w_qkv, l1_moba_w_o, l1_norm2_g, l1_moe_w_router, l1_moe_b_router, l1_moe_w_gate_up, l1_moe_b_gate_up, l1_moe_w_down, l1_moe_b_down, final_norm_g):
    bsz, seq, d = x.shape
    t = bsz * seq
    assert seq % TM_DENSE == 0 and seq % MOBA_BLOCK == 0 and seq % S5_CHUNK == 0
    assert seq // MOBA_BLOCK <= LANES and d == N_HEADS * HEAD_DIM
    x2 = x.reshape(t, d)
    c_pad = jnp.zeros((8, d), F32).at[:bsz].set(c)

    def ada_parts(w, b):
        a = _ada(c_pad, w, b)[:bsz]
        return [a[:, i * d:(i + 1) * d].reshape(bsz, 1, d) for i in range(6)]

    sh1, sc1, g1, sh2, sc2, g2 = ada_parts(l0_ada_w, l0_ada_b)
    u = _nml(x2, l0_norm1_g, sc1, sh1, l0_s5_w_in.astype(BF16), seq)
    rate, theta, bblk, cblk = _s5_params(l0_s5_b_re, l0_s5_b_im, l0_s5_c_re, l0_s5_c_im,
                                         l0_s5_lam_re, l0_s5_lam_im, l0_s5_log_dt)
    z = _s5_scan(u, rate, theta, bblk, cblk, l0_s5_d, bsz, seq)
    x2 = _row_call(_glu_out_kernel, z, x2, g1,
                   [l0_s5_w_glu.astype(BF16), l0_s5_w_out.astype(BF16)], seq)
    x2 = _moe(x2, l0_norm2_g, sc2, sh2, g2, l0_moe_w_router, l0_moe_b_router,
              l0_moe_w_gate_up, l0_moe_b_gate_up, l0_moe_w_down, l0_moe_b_down,
              final_norm_g, seq, final=False)

    sh1, sc1, g1, sh2, sc2, g2 = ada_parts(l1_ada_w, l1_ada_b)
    qkv = _nml(x2, l1_norm1_g, sc1, sh1, l1_moba_w_qkv.astype(BF16), seq)
    cos, sin = _rope_tables(positions.reshape(t, 1))
    q, k, v, km = _rope(qkv, cos, sin, bsz, seq)
    nb = seq // MOBA_BLOCK
    nbp = -(-nb // 8) * 8
    km_pad = jnp.zeros((bsz, N_HEADS, nbp, HEAD_DIM), F32).at[:, :, :nb].set(km[:, :, :, 0])
    o = _moba(q, k, v, km_pad, bsz, seq)
    x2 = _row_call(_lin_res_kernel, o, x2, g1, [l1_moba_w_o.astype(BF16)], seq)
    x2 = _moe(x2, l1_norm2_g, sc2, sh2, g2, l1_moe_w_router, l1_moe_b_router,
              l1_moe_w_gate_up, l1_moe_b_gate_up, l1_moe_w_down, l1_moe_b_down,
              final_norm_g, seq, final=True)
    return x2.reshape(bsz, seq, d)
```

```python
import functools
import math

import jax
import jax.numpy as jnp
from jax import lax
from jax.experimental import pallas as pl
from jax.experimental.pallas import tpu as pltpu

F32 = jnp.float32
BF16 = jnp.bfloat16
HI = lax.Precision.HIGHEST

EPS = 1e-6
NEG = -1e30
LOG2E = math.log2(math.e)
LANES = 128
MIB = 1024 * 1024

S5_GROUP = 16
S5_STATE = 64
N_HEADS = 8
HEAD_DIM = 128
MOBA_BLOCK = 256
MOBA_TOPK = 3
ROPE_THETA = 10000.0
N_EXPERTS = 32
TOP_K = 4
SWIGLU_LIMIT = 7.0
SWIGLU_ALPHA = 1.702

TM_DENSE = 512
S5_CHUNK = 256
S5_SUPER = 8
MOBA_HEADS_PER_STEP = 2
TM_EXPERT = 256
TD_ROWS = 128
TM_SRC = 2048


def _cparams(n_axes, vmem_mib):
    return pltpu.CompilerParams(dimension_semantics=("arbitrary",) * n_axes,
                                vmem_limit_bytes=vmem_mib * MIB)


def _pack_rows(x):
    n = x.shape[1] // 2
    xb = x.astype(BF16).astype(F32)
    lo = lax.bitcast_convert_type(xb[:, :n], jnp.uint32) >> 16
    hi = lax.bitcast_convert_type(xb[:, n:], jnp.uint32) & jnp.uint32(0xFFFF0000)
    return lo | hi


def _unpack_rows(p):
    lo = lax.bitcast_convert_type(p << 16, F32)
    hi = lax.bitcast_convert_type(p & jnp.uint32(0xFFFF0000), F32)
    return jnp.concatenate([lo, hi], axis=1)


def _normmod(x, g, sc, sh):
    ms = jnp.mean(x * x, axis=-1, keepdims=True)
    return (x * lax.rsqrt(ms + EPS)) * g * (1.0 + sc) + sh


def _ada_kernel(c_ref, w_ref, b_ref, o_ref):
    c = c_ref[...]
    ca = c * jax.nn.sigmoid(c)
    o_ref[...] = jnp.dot(ca, w_ref[...], preferred_element_type=F32, precision=HI) + b_ref[...]


def _ada(c_pad, w, b):
    d, n = w.shape
    tn = 1536
    return pl.pallas_call(
        _ada_kernel,
        grid=(n // tn,),
        in_specs=[pl.BlockSpec((8, d), lambda j: (0, 0)),
                  pl.BlockSpec((d, tn), lambda j: (0, j)),
                  pl.BlockSpec((1, tn), lambda j: (0, j))],
        out_specs=pl.BlockSpec((8, tn), lambda j: (0, j)),
        out_shape=jax.ShapeDtypeStruct((8, n), F32),
        compiler_params=_cparams(1, 32),
    )(c_pad, w, b.reshape(1, n))


def _nml_kernel(x_ref, g_ref, sc_ref, sh_ref, w_ref, o_ref):
    h = _normmod(x_ref[...], g_ref[...], sc_ref[0], sh_ref[0])
    o_ref[...] = jnp.dot(h.astype(BF16), w_ref[...], preferred_element_type=F32)


def _nml(x2, g, sc, sh, w_bf, seq):
    t, d = x2.shape
    n = w_bf.shape[1]
    tm = TM_DENSE
    tpb = seq // tm
    return pl.pallas_call(
        _nml_kernel,
        grid=(t // tm,),
        in_specs=[pl.BlockSpec((tm, d), lambda i: (i, 0)),
                  pl.BlockSpec((1, d), lambda i: (0, 0)),
                  pl.BlockSpec((1, 1, d), lambda i: (i // tpb, 0, 0)),
                  pl.BlockSpec((1, 1, d), lambda i: (i // tpb, 0, 0)),
                  pl.BlockSpec((d, n), lambda i: (0, 0))],
        out_specs=pl.BlockSpec((tm, n), lambda i: (i, 0)),
        out_shape=jax.ShapeDtypeStruct((t, n), F32),
        compiler_params=_cparams(1, 48),
    )(x2, g.reshape(1, d), sc, sh, w_bf)


def _gelu_tanh(y):
    c = math.sqrt(2.0 / math.pi)
    return y * (0.5 * (1.0 + jnp.tanh(c * (y + 0.044715 * (y * y * y)))))


def _s5_kernel(u_ref, rate_ref, theta_ref, bblk_ref, cblk_ref, d_ref, z_ref,
               apr, api, air, aii, xr, xi):
    L = u_ref.shape[0]
    n_super = bblk_ref.shape[0]
    half = apr.shape[1] // n_super
    nch = u_ref.shape[1] // n_super
    b = pl.program_id(0)
    c = pl.program_id(1)

    @pl.when((b == 0) & (c == 0))
    def _():
        tt = lax.broadcasted_iota(jnp.int32, (L, 1), 0).astype(F32)
        ph = theta_ref[...] * tt
        rt = rate_ref[...] * tt
        cs = jnp.cos(ph)
        sn = jnp.sin(ph)
        mag = jnp.exp(rt)
        inv = jnp.exp(-rt)
        apr[...] = mag * cs
        api[...] = mag * sn
        air[...] = inv * cs
        aii[...] = -(inv * sn)

    @pl.when(c == 0)
    def _():
        xr[...] = jnp.zeros_like(xr)
        xi[...] = jnp.zeros_like(xi)

    row = lax.broadcasted_iota(jnp.int32, (L, L), 0)
    col = lax.broadcasted_iota(jnp.int32, (L, L), 1)
    tril = (row >= col).astype(BF16)

    for sg in range(n_super):
        cols = slice(sg * half, (sg + 1) * half)
        ch = slice(sg * nch, (sg + 1) * nch)
        u = u_ref[:, ch]
        bu = jnp.dot(u.astype(BF16), bblk_ref[sg], preferred_element_type=F32)
        bur = bu[:, :half]
        bui = bu[:, half:]
        ir = air[:, cols]
        ii = aii[:, cols]
        zc = jnp.concatenate([bur * ir - bui * ii, bur * ii + bui * ir], axis=1)
        cum = jnp.dot(tril, zc.astype(BF16), preferred_element_type=F32)
        a_r = apr[1:2, cols]
        a_i = api[1:2, cols]
        p_r = xr[:, cols]
        p_i = xi[:, cols]
        cr = cum[:, :half] + (a_r * p_r - a_i * p_i)
        ci = cum[:, half:] + (a_r * p_i + a_i * p_r)
        pw_r = apr[:, cols]
        pw_i = api[:, cols]
        x_re = cr * pw_r - ci * pw_i
        x_im = cr * pw_i + ci * pw_r
        xr[:, cols] = x_re[L - 1:L, :]
        xi[:, cols] = x_im[L - 1:L, :]
        xc = jnp.concatenate([x_re, x_im], axis=1).astype(BF16)
        y = jnp.dot(xc, cblk_ref[sg], preferred_element_type=F32) + d_ref[:, ch] * u
        z_ref[:, ch] = _gelu_tanh(y)


def _s5_scan(u, rate, theta, bblk, cblk, d_skip, bsz, seq):
    t, w = u.shape
    L = S5_CHUNK
    nc = seq // L
    ns = rate.shape[1]
    return pl.pallas_call(
        _s5_kernel,
        grid=(bsz, nc),
        in_specs=[pl.BlockSpec((L, w), lambda b, c: (b * nc + c, 0)),
                  pl.BlockSpec((1, ns), lambda b, c: (0, 0)),
                  pl.BlockSpec((1, ns), lambda b, c: (0, 0)),
                  pl.BlockSpec(bblk.shape, lambda b, c: (0, 0, 0)),
                  pl.BlockSpec(cblk.shape, lambda b, c: (0, 0, 0)),
                  pl.BlockSpec((1, w), lambda b, c: (0, 0))],
        out_specs=pl.BlockSpec((L, w), lambda b, c: (b * nc + c, 0)),
        out_shape=jax.ShapeDtypeStruct((t, w), F32),
        scratch_shapes=[pltpu.VMEM((L, ns), F32)] * 4 + [pltpu.VMEM((1, ns), F32)] * 2,
        compiler_params=_cparams(2, 58),
    )(u, rate, theta, bblk, cblk, d_skip.reshape(1, w))


def _s5_params(b_re, b_im, c_re, c_im, lam_re, lam_im, log_dt):
    g, p, ch = b_re.shape
    dt = jnp.exp(log_dt)[:, None]
    rate = lam_re * dt
    theta = lam_im * dt
    mag = jnp.exp(rate)
    a_re = mag * jnp.cos(theta)
    a_im = mag * jnp.sin(theta)
    den = lam_re * lam_re + lam_im * lam_im
    f_re = ((a_re - 1.0) * lam_re + a_im * lam_im) / den
    f_im = (a_im * lam_re - (a_re - 1.0) * lam_im) / den
    bb_re = f_re[..., None] * b_re - f_im[..., None] * b_im
    bb_im = f_re[..., None] * b_im + f_im[..., None] * b_re
    ns = S5_SUPER
    nsg = g // ns
    eye = jnp.eye(ns, dtype=F32)

    def blk_b(bb):
        return jnp.einsum('sgpc,gh->sgchp', bb.reshape(nsg, ns, p, ch), eye).reshape(nsg, ns * ch, ns * p)

    def blk_c(cc):
        return jnp.einsum('sgcp,gh->sgphc', cc.reshape(nsg, ns, ch, p), eye).reshape(nsg, ns * p, ns * ch)

    bblk = jnp.concatenate([blk_b(bb_re), blk_b(bb_im)], axis=2).astype(BF16)
    cblk = jnp.concatenate([blk_c(c_re), -blk_c(c_im)], axis=1).astype(BF16)
    return rate.reshape(1, g * p), theta.reshape(1, g * p), bblk, cblk


def _glu_out_kernel(z_ref, x_ref, g_ref, wg_ref, wo_ref, o_ref):
    z = z_ref[...]
    gate = jax.nn.sigmoid(jnp.dot(z.astype(BF16), wg_ref[...], preferred_element_type=F32))
    m = jnp.dot((z * gate).astype(BF16), wo_ref[...], preferred_element_type=F32)
    o_ref[...] = x_ref[...] + g_ref[0] * m


def _lin_res_kernel(a_ref, x_ref, g_ref, w_ref, o_ref):
    m = jnp.dot(a_ref[...].astype(BF16), w_ref[...], preferred_element_type=F32)
    o_ref[...] = x_ref[...] + g_ref[0] * m


def _row_call(kernel, acts, x2, gate, weights, seq):
    t, d = x2.shape
    tm = TM_DENSE
    tpb = seq // tm
    row = pl.BlockSpec((tm, d), lambda i: (i, 0))
    return pl.pallas_call(
        kernel,
        grid=(t // tm,),
        in_specs=[row, row, pl.BlockSpec((1, 1, d), lambda i: (i // tpb, 0, 0))]
                 + [pl.BlockSpec(w.shape, lambda i: (0, 0)) for w in weights],
        out_specs=row,
        out_shape=jax.ShapeDtypeStruct((t, d), F32),
        compiler_params=_cparams(1, 48),
    )(acts, x2, gate, *weights)


def _rope_tab_kernel(pos_ref, inv_ref, sgn_ref, cos_ref, sin_ref):
    ang = pos_ref[...].astype(F32) * inv_ref[...]
    cos_ref[...] = jnp.cos(ang)
    sin_ref[...] = jnp.sin(ang) * sgn_ref[...]


def _rope_tables(pos_col):
    t = pos_col.shape[0]
    ts = 512
    half = HEAD_DIM // 2
    inv = ROPE_THETA ** (-jnp.arange(0, HEAD_DIM, 2, dtype=F32) / HEAD_DIM)
    inv = jnp.concatenate([inv, inv]).reshape(1, HEAD_DIM)
    sgn = jnp.concatenate([-jnp.ones((half,), F32), jnp.ones((half,), F32)]).reshape(1, HEAD_DIM)
    const = pl.BlockSpec((1, HEAD_DIM), lambda i: (0, 0))
    tile = pl.BlockSpec((ts, HEAD_DIM), lambda i: (i, 0))
    return pl.pallas_call(
        _rope_tab_kernel,
        grid=(t // ts,),
        in_specs=[pl.BlockSpec((ts, 1), lambda i: (i, 0)), const, const],
        out_specs=[tile, tile],
        out_shape=[jax.ShapeDtypeStruct((t, HEAD_DIM), F32)] * 2,
        compiler_params=_cparams(1, 16),
    )(pos_col, inv, sgn)


def _rope_kernel(q_ref, k_ref, v_ref, cos_ref, sin_ref, qo_ref, ko_ref, vo_ref, km_ref):
    cs = cos_ref[...]
    sn = sin_ref[...]
    dh = cs.shape[1]
    half = dh // 2
    for hh in range(qo_ref.shape[1]):
        cols = slice(hh * dh, (hh + 1) * dh)
        q = q_ref[:, cols]
        k = k_ref[:, cols]
        qr = q * cs + pltpu.roll(q, half, 1) * sn
        kr = k * cs + pltpu.roll(k, half, 1) * sn
        qo_ref[0, hh] = qr
        ko_ref[0, hh] = kr.astype(BF16)
        vo_ref[0, hh, 0] = v_ref[:, cols].T.astype(BF16)
        km_ref[0, hh, 0] = jnp.mean(kr, axis=0, keepdims=True)


def _rope(qkv, cos, sin, bsz, seq):
    t = qkv.shape[0]
    h, dh, blk = N_HEADS, HEAD_DIM, MOBA_BLOCK
    nb = seq // blk
    part = lambda p: pl.BlockSpec((blk, h * dh), lambda i: (i, p))
    tab = pl.BlockSpec((blk, dh), lambda i: (i, 0))
    head = pl.BlockSpec((1, h, blk, dh), lambda i: (i // nb, 0, i % nb, 0))
    return pl.pallas_call(
        _rope_kernel,
        grid=(t // blk,),
        in_specs=[part(0), part(1), part(2), tab, tab],
        out_specs=[head, head,
                   pl.BlockSpec((1, h, 1, dh, blk), lambda i: (i // nb, 0, i % nb, 0, 0)),
                   pl.BlockSpec((1, h, 1, 1, dh), lambda i: (i // nb, 0, i % nb, 0, 0))],
        out_shape=[jax.ShapeDtypeStruct((bsz, h, seq, dh), F32),
                   jax.ShapeDtypeStruct((bsz, h, seq, dh), BF16),
                   jax.ShapeDtypeStruct((bsz, h, nb, dh, blk), BF16),
                   jax.ShapeDtypeStruct((bsz, h, nb, 1, dh), F32)],
        compiler_params=_cparams(1, 32),
    )(qkv, qkv, qkv, cos, sin)


def _moba_kernel(q_ref, k_ref, vt_ref, km_ref, o_ref, sel_scr):
    nh = q_ref.shape[1]
    blk = q_ref.shape[2]
    dh = q_ref.shape[3]
    j = pl.program_id(2)
    nt = (((1,), (1,)), ((), ()))
    heads = range(nh)

    qs = []
    for hh in heads:
        q = q_ref[0, hh]
        gate = lax.dot_general(km_ref[0, hh], q, nt, preferred_element_type=F32, precision=HI)
        bidx = lax.broadcasted_iota(jnp.int32, gate.shape, 0)
        bidx_f = bidx.astype(F32)
        work = jnp.where(bidx < j, gate, NEG)
        sel = jnp.zeros(gate.shape, F32)
        for _ in range(MOBA_TOPK):
            m = jnp.max(work, axis=0, keepdims=True)
            idx = jnp.min(jnp.where(work == m, bidx_f, float(LANES)), axis=0, keepdims=True)
            hit = bidx_f == idx
            sel = jnp.where(hit & (m > 0.5 * NEG), 1.0, sel)
            work = jnp.where(hit, 2.0 * NEG, work)
        sel_scr[hh] = sel
        qs.append((q * (dh ** -0.5 * LOG2E)).astype(BF16))

    def scores(hh, n):
        start = pl.multiple_of(n * blk, blk)
        kb = k_ref[0, hh, pl.ds(start, blk), :]
        return lax.dot_general(kb, qs[hh], nt, preferred_element_type=F32)

    nb_last = vt_ref.shape[2] - 1

    def score_pair(hh, nn):
        return (scores(hh, jnp.minimum(2 * nn, nb_last)), scores(hh, jnp.minimum(2 * nn + 1, nb_last)))

    def past_pair(hh, nn, s_pair, m_i, l_i, acc):
        n0 = 2 * nn
        n1 = n0 + 1
        s0, s1 = s_pair
        r0 = sel_scr[hh, pl.ds(n0, 1), :] > 0.5
        r1 = sel_scr[hh, pl.ds(n1, 1), :] > 0.5
        rm0 = jnp.where(r0, jnp.max(s0, axis=0, keepdims=True), NEG)
        rm1 = jnp.where(r1, jnp.max(s1, axis=0, keepdims=True), NEG)
        m_new = jnp.maximum(m_i, jnp.maximum(rm0, rm1))
        alpha = jnp.exp2(m_i - m_new)
        p0 = jnp.exp2(s0 - jnp.where(r0, m_new, -NEG))
        p1 = jnp.exp2(s1 - jnp.where(r1, m_new, -NEG))
        l_new = alpha * l_i + (jnp.sum(p0, axis=0, keepdims=True) + jnp.sum(p1, axis=0, keepdims=True))
        pv = (jnp.dot(vt_ref[0, hh, n0], p0.astype(BF16), preferred_element_type=F32)
              + jnp.dot(vt_ref[0, hh, n1], p1.astype(BF16), preferred_element_type=F32))
        return m_new, l_new, alpha * acc + pv

    def body(nn, carry):
        state, s_cur = carry
        s_nxt = tuple(score_pair(hh, nn + 1) for hh in heads)
        state = tuple(past_pair(hh, nn, s_cur[hh], *state[hh]) for hh in heads)
        return state, s_nxt

    init = tuple((jnp.full((1, blk), NEG, F32), jnp.zeros((1, blk), F32), jnp.zeros((dh, blk), F32))
                 for _ in heads)
    carry, _ = lax.fori_loop(0, (j + 1) // 2, body, (init, tuple(score_pair(hh, 0) for hh in heads)))

    for hh in heads:
        m_i, l_i, acc = carry[hh]
        s = scores(hh, j)
        kk = lax.broadcasted_iota(jnp.int32, s.shape, 0)
        qq = lax.broadcasted_iota(jnp.int32, s.shape, 1)
        s = jnp.where(kk <= qq, s, NEG)
        m_new = jnp.maximum(m_i, jnp.max(s, axis=0, keepdims=True))
        alpha = jnp.exp2(m_i - m_new)
        p = jnp.exp2(s - m_new)
        l_i = alpha * l_i + jnp.sum(p, axis=0, keepdims=True)
        acc = alpha * acc + jnp.dot(vt_ref[0, hh, j], p.astype(BF16), preferred_element_type=F32)
        o_ref[:, hh * dh:(hh + 1) * dh] = (acc / l_i).T.astype(o_ref.dtype)


def _moba(q, k, vt, km_pad, bsz, seq):
    h, dh, blk = N_HEADS, HEAD_DIM, MOBA_BLOCK
    nb = seq // blk
    nbp = km_pad.shape[2]
    nh = MOBA_HEADS_PER_STEP
    return pl.pallas_call(
        _moba_kernel,
        grid=(bsz, h // nh, nb),
        in_specs=[pl.BlockSpec((1, nh, blk, dh), lambda b, j, i: (b, j, i, 0)),
                  pl.BlockSpec((1, nh, seq, dh), lambda b, j, i: (b, j, 0, 0)),
                  pl.BlockSpec((1, nh, nb, dh, blk), lambda b, j, i: (b, j, 0, 0, 0)),
                  pl.BlockSpec((1, nh, nbp, dh), lambda b, j, i: (b, j, 0, 0))],
        out_specs=pl.BlockSpec((blk, nh * dh), lambda b, j, i: (b * nb + i, j)),
        out_shape=jax.ShapeDtypeStruct((bsz * seq, h * dh), BF16),
        scratch_shapes=[pltpu.VMEM((nh, nbp, blk), F32)],
        compiler_params=_cparams(3, 48),
    )(q, k, vt, km_pad)


def _router_kernel(x_ref, g_ref, sc_ref, sh_ref, wr_ref, br_ref, h_ref, oh_ref, ei_ref, gt_ref):
    h = _normmod(x_ref[...], g_ref[...], sc_ref[0], sh_ref[0])
    h_ref[...] = _pack_rows(h)
    logits = jnp.dot(h, wr_ref[...], preferred_element_type=F32, precision=HI) + br_ref[...]
    lane = lax.broadcasted_iota(jnp.int32, logits.shape, 1)
    lane_f = lane.astype(F32)
    work = logits
    onehot = jnp.zeros(logits.shape, F32)
    ei = jnp.zeros(logits.shape, F32)
    vals = []
    for kk in range(TOP_K):
        m = jnp.max(work, axis=-1, keepdims=True)
        idx = jnp.min(jnp.where(work == m, lane_f, float(LANES)), axis=-1, keepdims=True)
        hit = lane_f == idx
        onehot = jnp.where(hit, 1.0, onehot)
        ei = jnp.where(lane == kk, idx, ei)
        vals.append(m)
        work = jnp.where(hit, 2.0 * NEG, work)
    ex = [jnp.exp(vv - vals[0]) for vv in vals]
    tot = ex[0] + ex[1] + ex[2] + ex[3]
    gt = jnp.zeros(logits.shape, F32)
    for kk in range(TOP_K):
        gt = jnp.where(lane == kk, ex[kk] / tot, gt)
    oh_ref[...] = onehot
    ei_ref[...] = ei
    gt_ref[...] = gt


def _router(x2, g, sc, sh, wr_pad, br_pad, seq):
    t, d = x2.shape
    tm = TM_DENSE
    tpb = seq // tm
    lane_tile = pl.BlockSpec((tm, LANES), lambda i: (i, 0))
    return pl.pallas_call(
        _router_kernel,
        grid=(t // tm,),
        in_specs=[pl.BlockSpec((tm, d), lambda i: (i, 0)),
                  pl.BlockSpec((1, d), lambda i: (0, 0)),
                  pl.BlockSpec((1, 1, d), lambda i: (i // tpb, 0, 0)),
                  pl.BlockSpec((1, 1, d), lambda i: (i // tpb, 0, 0)),
                  pl.BlockSpec((d, LANES), lambda i: (0, 0)),
                  pl.BlockSpec((1, LANES), lambda i: (0, 0))],
        out_specs=[pl.BlockSpec((tm, d // 2), lambda i: (i, 0)), lane_tile, lane_tile, lane_tile],
        out_shape=[jax.ShapeDtypeStruct((t, d // 2), jnp.uint32)]
                  + [jax.ShapeDtypeStruct((t, LANES), F32)] * 3,
        compiler_params=_cparams(1, 32),
    )(x2, g.reshape(1, d), sc, sh, wr_pad, br_pad)


def _pos_kernel(oh_ref, ei_ref, dest_ref, meta_ref, cnt_scr, off_scr, run_scr):
    p = pl.program_id(0)
    i = pl.program_id(1)
    tm = oh_ref.shape[0]
    oh = oh_ref[...]
    colsum = jnp.sum(oh, axis=0, keepdims=True)

    @pl.when((p == 0) & (i == 0))
    def _():
        cnt_scr[...] = jnp.zeros_like(cnt_scr)

    @pl.when(p == 0)
    def _():
        cnt_scr[...] += colsum
        dest_ref[...] = jnp.zeros_like(dest_ref)

    @pl.when((p == 1) & (i == 0))
    def _():
        cnt = cnt_scr[...]
        padded = jnp.ceil(cnt * (1.0 / TM_EXPERT)) * float(TM_EXPERT)
        r = lax.broadcasted_iota(jnp.int32, (LANES, LANES), 0)
        cidx = lax.broadcasted_iota(jnp.int32, (LANES, LANES), 1)
        tri = (r < cidx).astype(F32)
        off = jnp.dot(jnp.broadcast_to(padded, (8, LANES)), tri,
                      preferred_element_type=F32, precision=HI)
        off_scr[...] = off[0:1]
        run_scr[...] = jnp.zeros_like(run_scr)

    @pl.when(p == 1)
    def _():
        r = lax.broadcasted_iota(jnp.int32, (tm, tm), 0)
        cidx = lax.broadcasted_iota(jnp.int32, (tm, tm), 1)
        ltri = (r > cidx).astype(BF16)
        excl = jnp.dot(ltri, oh.astype(BF16), preferred_element_type=F32)
        dfull = off_scr[...] + run_scr[...] + excl
        ei = ei_ref[...]
        lane = lax.broadcasted_iota(jnp.int32, oh.shape, 1)
        lane_f = lane.astype(F32)
        dest = jnp.zeros(oh.shape, F32)
        for kk in range(TOP_K):
            ek = jnp.sum(jnp.where(lane == kk, ei, 0.0), axis=-1, keepdims=True)
            dk = jnp.sum(jnp.where(lane_f == ek, dfull, 0.0), axis=-1, keepdims=True)
            dest = jnp.where(lane == kk, dk, dest)
        dest_ref[...] = dest.astype(jnp.int32)
        run_scr[...] += colsum

    meta_ref[...] = jnp.concatenate([off_scr[...], cnt_scr[...],
                                     jnp.zeros((6, LANES), F32)], axis=0)


def _positions(onehot, ei):
    t = onehot.shape[0]
    tm = TM_DENSE
    tile = pl.BlockSpec((tm, LANES), lambda p, i: (i, 0))
    return pl.pallas_call(
        _pos_kernel,
        grid=(2, t // tm),
        in_specs=[tile, tile],
        out_specs=[pl.BlockSpec((tm, LANES), lambda p, i: (i * p, 0)),
                   pl.BlockSpec((8, LANES), lambda p, i: (0, 0))],
        out_shape=[jax.ShapeDtypeStruct((t, LANES), jnp.int32),
                   jax.ShapeDtypeStruct((8, LANES), F32)],
        scratch_shapes=[pltpu.VMEM((1, LANES), F32)] * 3,
        compiler_params=_cparams(2, 16),
    )(onehot, ei)


def _src_kernel(dest_ref, src_ref):
    i = pl.program_id(0)
    tb = dest_ref.shape[0] // TOP_K

    @pl.when(i == 0)
    def _():
        def zero(j, carry):
            src_ref[j] = 0
            return carry
        lax.fori_loop(0, src_ref.shape[0], zero, 0, unroll=16)

    def body(tt, carry):
        for kk in range(TOP_K):
            src_ref[dest_ref[tt * TOP_K + kk]] = i * tb + tt
        return carry

    lax.fori_loop(0, tb, body, 0, unroll=4)


def _sources(dest_flat, n_rows):
    n = dest_flat.shape[0]
    tb = TM_SRC
    return pl.pallas_call(
        _src_kernel,
        grid=(n // (tb * TOP_K),),
        in_specs=[pl.BlockSpec((tb * TOP_K,), lambda i: (i,), memory_space=pltpu.SMEM)],
        out_specs=pl.BlockSpec(memory_space=pltpu.SMEM),
        out_shape=jax.ShapeDtypeStruct((n_rows,), jnp.int32),
        compiler_params=_cparams(1, 16),
    )(dest_flat)


def _expert_kernel(te_ref, tv_ref, src_cur, src_nxt, h_hbm, wgu_ref, bgu_ref, wd_ref, bd_ref,
                   o_ref, xbuf, sem, wgu_bf, wd_bf):
    i = pl.program_id(0)
    nt = pl.num_programs(0)
    tm = o_ref.shape[0]
    slot = lax.rem(i, 2)
    e = te_ref[i]
    prev = te_ref[jnp.maximum(i - 1, 0)]
    dff = wd_ref.shape[1]
    def gather(src_ref, s):
        def body(r8, carry):
            for uu in range(8):
                pltpu.make_async_copy(h_hbm.at[pl.ds(src_ref[r8 * 8 + uu], 1)],
                                      xbuf.at[s, r8, pl.ds(uu, 1)], sem.at[s]).start()
            return carry
        lax.fori_loop(0, tm // 8, body, 0)

    @pl.when((i == 0) & (tv_ref[0] > 0))
    def _():
        gather(src_cur, 0)

    nxt = jnp.minimum(i + 1, nt - 1)

    @pl.when((i + 1 < nt) & (tv_ref[nxt] > 0))
    def _():
        gather(src_nxt, 1 - slot)

    @pl.when((i == 0) | (e != prev))
    def _():
        wgu_bf[...] = wgu_ref[0].astype(BF16)
        wd_bf[...] = wd_ref[0].astype(BF16)

    @pl.when(tv_ref[i] > 0)
    def _():
        for r8 in range(tm // 8):
            pltpu.make_async_copy(h_hbm.at[pl.ds(0, 8)], xbuf.at[slot, r8], sem.at[slot]).wait()
        x = _unpack_rows(xbuf[slot].reshape(tm, xbuf.shape[3])).astype(BF16)
        hu = jnp.dot(x, wgu_bf[...], preferred_element_type=F32) + bgu_ref[0]
        g = jnp.minimum(hu[:, :dff], SWIGLU_LIMIT)
        up = jnp.clip(hu[:, dff:], -SWIGLU_LIMIT, SWIGLU_LIMIT)
        act = (up + 1.0) * g * jax.nn.sigmoid(SWIGLU_ALPHA * g)
        y = jnp.dot(act.astype(BF16), wd_bf[...], preferred_element_type=F32) + bd_ref[0]
        o_ref[...] = _pack_rows(y)

    @pl.when(tv_ref[i] == 0)
    def _():
        o_ref[...] = jnp.zeros_like(o_ref)


def _experts(tile_e, tile_v, src, h2p, w_gu, b_gu, w_d, b_d):
    n_rows = src.shape[0]
    dp = h2p.shape[1]
    ne, d, f2 = w_gu.shape
    dff = w_d.shape[1]
    tm = TM_EXPERT
    nt = n_rows // tm
    grid_spec = pltpu.PrefetchScalarGridSpec(
        num_scalar_prefetch=2,
        grid=(nt,),
        in_specs=[pl.BlockSpec((tm,), lambda i, te, tv: (i,), memory_space=pltpu.SMEM),
                  pl.BlockSpec((tm,), lambda i, te, tv: (jnp.minimum(i + 1, nt - 1),),
                               memory_space=pltpu.SMEM),
                  pl.BlockSpec(memory_space=pl.ANY),
                  pl.BlockSpec((1, d, f2), lambda i, te, tv: (te[i], 0, 0)),
                  pl.BlockSpec((1, 1, f2), lambda i, te, tv: (te[i], 0, 0)),
                  pl.BlockSpec((1, dff, d), lambda i, te, tv: (te[i], 0, 0)),
                  pl.BlockSpec((1, 1, d), lambda i, te, tv: (te[i], 0, 0))],
        out_specs=pl.BlockSpec((tm, dp), lambda i, te, tv: (i, 0)),
        scratch_shapes=[pltpu.VMEM((2, tm // 8, 8, dp), jnp.uint32), pltpu.SemaphoreType.DMA((2,)),
                        pltpu.VMEM((d, f2), BF16), pltpu.VMEM((dff, d), BF16)],
    )
    return pl.pallas_call(
        _expert_kernel,
        grid_spec=grid_spec,
        out_shape=jax.ShapeDtypeStruct((n_rows, dp), jnp.uint32),
        compiler_params=_cparams(1, 56),
    )(tile_e, tile_v, src, src, h2p, w_gu, b_gu.reshape(ne, 1, f2), w_d, b_d.reshape(ne, 1, d))


def _combine_kernel(dcur, dnxt, gt_ref, x_ref, g_ref, gf_ref, ys_hbm, o_ref, buf, sem, *, final):
    i = pl.program_id(0)
    nt = pl.num_programs(0)
    td = x_ref.shape[0]
    slot = lax.rem(i, 2)

    def gather(dref, s):
        def body(t8, carry):
            for uu in range(8):
                for kk in range(TOP_K):
                    pltpu.make_async_copy(ys_hbm.at[pl.ds(dref[(t8 * 8 + uu) * TOP_K + kk], 1)],
                                          buf.at[s, kk, t8, pl.ds(uu, 1)], sem.at[s]).start()
            return carry
        lax.fori_loop(0, td // 8, body, 0)

    @pl.when(i == 0)
    def _():
        gather(dcur, 0)

    @pl.when(i + 1 < nt)
    def _():
        gather(dnxt, 1 - slot)

    for kk in range(TOP_K):
        for t8 in range(td // 8):
            pltpu.make_async_copy(ys_hbm.at[pl.ds(0, 8)], buf.at[slot, kk, t8], sem.at[slot]).wait()
    gt = gt_ref[...]
    rows = lambda kk: _unpack_rows(buf[slot, kk].reshape(td, buf.shape[4]))
    y = gt[:, 0:1] * rows(0)
    for kk in range(1, TOP_K):
        y = y + gt[:, kk:kk + 1] * rows(kk)
    xn = x_ref[...] + g_ref[0] * y
    if final:
        ms = jnp.mean(xn * xn, axis=-1, keepdims=True)
        xn = (xn * lax.rsqrt(ms + EPS)) * gf_ref[...]
    o_ref[...] = xn


def _combine(dest_flat, gates, x2, gate_vec, gf, ys, seq, final):
    t, d = x2.shape
    td = TD_ROWS
    tpb = seq // td
    nt = t // td
    return pl.pallas_call(
        functools.partial(_combine_kernel, final=final),
        grid=(nt,),
        in_specs=[pl.BlockSpec((td * TOP_K,), lambda i: (i,), memory_space=pltpu.SMEM),
                  pl.BlockSpec((td * TOP_K,), lambda i: (jnp.minimum(i + 1, nt - 1),),
                               memory_space=pltpu.SMEM),
                  pl.BlockSpec((td, LANES), lambda i: (i, 0)),
                  pl.BlockSpec((td, d), lambda i: (i, 0)),
                  pl.BlockSpec((1, 1, d), lambda i: (i // tpb, 0, 0)),
                  pl.BlockSpec((1, d), lambda i: (0, 0)),
                  pl.BlockSpec(memory_space=pl.ANY)],
        out_specs=pl.BlockSpec((td, d), lambda i: (i, 0)),
        out_shape=jax.ShapeDtypeStruct((t, d), F32),
        scratch_shapes=[pltpu.VMEM((2, TOP_K, td // 8, 8, d // 2), jnp.uint32),
                        pltpu.SemaphoreType.DMA((2,))],
        compiler_params=_cparams(1, 32),
    )(dest_flat, dest_flat, gates, x2, gate_vec, gf.reshape(1, d), ys)


def _moe(x2, norm_g, sc, sh, gate_vec, w_router, b_router, w_gu, b_gu, w_d, b_d, gf, seq, final):
    t, d = x2.shape
    ne = w_router.shape[1]
    wr_pad = jnp.zeros((d, LANES), F32).at[:, :ne].set(w_router)
    br_pad = jnp.full((1, LANES), NEG, F32).at[0, :ne].set(b_router)
    h2, onehot, ei, gates = _router(x2, norm_g, sc, sh, wr_pad, br_pad, seq)
    dest, meta = _positions(onehot, ei)
    dest_flat = dest[:, :TOP_K].reshape(t * TOP_K)

    tm = TM_EXPERT
    n_rows = t * TOP_K + ne * tm
    n_tiles = n_rows // tm
    off = meta[0, :ne].astype(jnp.int32)
    cnt = meta[1, :ne].astype(jnp.int32)
    ends = off + ((cnt + tm - 1) // tm) * tm
    starts = jnp.arange(n_tiles, dtype=jnp.int32) * tm
    valid = starts < ends[-1]
    te = jnp.sum((starts[:, None] >= ends[None, :]).astype(jnp.int32), axis=1)
    te_last = jnp.sum((ends[-1] - 1 >= ends).astype(jnp.int32))
    tile_e = jnp.minimum(jnp.where(valid, te, te_last), ne - 1)
    tile_v = valid.astype(jnp.int32)

    src = _sources(dest_flat, n_rows)
    ys = _experts(tile_e, tile_v, src, h2, w_gu, b_gu, w_d, b_d)
    return _combine(dest_flat, gates, x2, gate_vec, gf, ys, seq, final)


def kernel(x, c, positions, l0_norm1_g, l0_ada_w, l0_ada_b, l0_s5_w_in, l0_s5_b_re, l0_s5_b_im, l0_s5_c_re, l0_s5_c_im, l0_s5_lam_re, l0_s5_lam_im, l0_s5_log_dt, l0_s5_d, l0_s5_w_glu, l0_s5_w_out, l0_norm2_g, l0_moe_w_router, l0_moe_b_router, l0_moe_w_gate_up, l0_moe_b_gate_up, l0_moe_w_down, l0_moe_b_down, l1_norm1_g, l1_ada_w, l1_ada_b, l1_moba_w_qkv, l1_moba_w_o, l1_norm2_g, l1_moe_w_router, l1_moe_b_router, l1_moe_w_gate_up, l1_moe_b_gate_up, l1_moe_w_down, l1_moe_b_down, final_norm_g):
    bsz, seq, d = x.shape
    t = bsz * seq
    assert seq % TM_DENSE == 0 and seq % MOBA_BLOCK == 0 and seq % S5_CHUNK == 0
    assert seq // MOBA_BLOCK <= LANES and d == N_HEADS * HEAD_DIM
    x2 = x.reshape(t, d)
    c_pad = jnp.zeros((8, d), F32).at[:bsz].set(c)

    def ada_parts(w, b):
        a = _ada(c_pad, w, b)[:bsz]
        return [a[:, i * d:(i + 1) * d].reshape(bsz, 1, d) for i in range(6)]

    sh1, sc1, g1, sh2, sc2, g2 = ada_parts(l0_ada_w, l0_ada_b)
    u = _nml(x2, l0_norm1_g, sc1, sh1, l0_s5_w_in.astype(BF16), seq)
    rate, theta, bblk, cblk = _s5_params(l0_s5_b_re, l0_s5_b_im, l0_s5_c_re, l0_s5_c_im,
                                         l0_s5_lam_re, l0_s5_lam_im, l0_s5_log_dt)
    z = _s5_scan(u, rate, theta, bblk, cblk, l0_s5_d, bsz, seq)
    x2 = _row_call(_glu_out_kernel, z, x2, g1,
                   [l0_s5_w_glu.astype(BF16), l0_s5_w_out.astype(BF16)], seq)
    x2 = _moe(x2, l0_norm2_g, sc2, sh2, g2, l0_moe_w_router, l0_moe_b_router,
              l0_moe_w_gate_up, l0_moe_b_gate_up, l0_moe_w_down, l0_moe_b_down,
              final_norm_g, seq, final=False)

    sh1, sc1, g1, sh2, sc2, g2 = ada_parts(l1_ada_w, l1_ada_b)
    qkv = _nml(x2, l1_norm1_g, sc1, sh1, l1_moba_w_qkv.astype(BF16), seq)
    cos, sin = _rope_tables(positions.reshape(t, 1))
    q, k, v, km = _rope(qkv, cos, sin, bsz, seq)
    nb = seq // MOBA_BLOCK
    nbp = -(-nb // 8) * 8
    km_pad = jnp.zeros((bsz, N_HEADS, nbp, HEAD_DIM), F32).at[:, :, :nb].set(km[:, :, :, 0])
    o = _moba(q, k, v, km_pad, bsz, seq)
    x2 = _row_call(_lin_res_kernel, o, x2, g1, [l1_moba_w_o.astype(BF16)], seq)
    x2 = _moe(x2, l1_norm2_g, sc2, sh2, g2, l1_moe_w_router, l1_moe_b_router,
              l1_moe_w_gate_up, l1_moe_b_gate_up, l1_moe_w_down, l1_moe_b_down,
              final_norm_g, seq, final=True)
    return x2.reshape(bsz, seq, d)
```

```python
import functools
import math

import jax
import jax.numpy as jnp
from jax import lax
from jax.experimental import pallas as pl
from jax.experimental.pallas import tpu as pltpu

F32 = jnp.float32
BF16 = jnp.bfloat16
HI = lax.Precision.HIGHEST

EPS = 1e-6
NEG = -1e30
LOG2E = math.log2(math.e)
LANES = 128
MIB = 1024 * 1024

S5_GROUP = 16
S5_STATE = 64
N_HEADS = 8
HEAD_DIM = 128
MOBA_BLOCK = 256
MOBA_TOPK = 3
ROPE_THETA = 10000.0
N_EXPERTS = 32
TOP_K = 4
SWIGLU_LIMIT = 7.0
SWIGLU_ALPHA = 1.702

TM_DENSE = 512
S5_CHUNK = 256
S5_SUPER = 8
MOBA_HEADS_PER_STEP = 2
TM_EXPERT = 256
TD_ROWS = 128
TM_SRC = 2048


def _cparams(n_axes, vmem_mib):
    return pltpu.CompilerParams(dimension_semantics=("arbitrary",) * n_axes,
                                vmem_limit_bytes=vmem_mib * MIB)


def _pack_rows(x):
    n = x.shape[1] // 2
    xb = x.astype(BF16).astype(F32)
    lo = lax.bitcast_convert_type(xb[:, :n], jnp.uint32) >> 16
    hi = lax.bitcast_convert_type(xb[:, n:], jnp.uint32) & jnp.uint32(0xFFFF0000)
    return lo | hi


def _unpack_rows(p):
    lo = lax.bitcast_convert_type(p << 16, F32)
    hi = lax.bitcast_convert_type(p & jnp.uint32(0xFFFF0000), F32)
    return jnp.concatenate([lo, hi], axis=1)


def _normmod(x, g, sc, sh):
    ms = jnp.mean(x * x, axis=-1, keepdims=True)
    return (x * lax.rsqrt(ms + EPS)) * g * (1.0 + sc) + sh


def _ada_kernel(c_ref, w_ref, b_ref, o_ref):
    c = c_ref[...]
    ca = c * jax.nn.sigmoid(c)
    o_ref[...] = jnp.dot(ca, w_ref[...], preferred_element_type=F32, precision=HI) + b_ref[...]


def _ada(c_pad, w, b):
    d, n = w.shape
    tn = 1536
    return pl.pallas_call(
        _ada_kernel,
        grid=(n // tn,),
        in_specs=[pl.BlockSpec((8, d), lambda j: (0, 0)),
                  pl.BlockSpec((d, tn), lambda j: (0, j)),
                  pl.BlockSpec((1, tn), lambda j: (0, j))],
        out_specs=pl.BlockSpec((8, tn), lambda j: (0, j)),
        out_shape=jax.ShapeDtypeStruct((8, n), F32),
        compiler_params=_cparams(1, 32),
    )(c_pad, w, b.reshape(1, n))


def _nml_kernel(x_ref, g_ref, sc_ref, sh_ref, w_ref, o_ref):
    h = _normmod(x_ref[...], g_ref[...], sc_ref[0], sh_ref[0])
    o_ref[...] = jnp.dot(h.astype(BF16), w_ref[...], preferred_element_type=F32).astype(o_ref.dtype)


def _nml(x2, g, sc, sh, w_bf, seq, out_dtype):
    t, d = x2.shape
    n = w_bf.shape[1]
    tm = TM_DENSE
    tpb = seq // tm
    return pl.pallas_call(
        _nml_kernel,
        grid=(t // tm,),
        in_specs=[pl.BlockSpec((tm, d), lambda i: (i, 0)),
                  pl.BlockSpec((1, d), lambda i: (0, 0)),
                  pl.BlockSpec((1, 1, d), lambda i: (i // tpb, 0, 0)),
                  pl.BlockSpec((1, 1, d), lambda i: (i // tpb, 0, 0)),
                  pl.BlockSpec((d, n), lambda i: (0, 0))],
        out_specs=pl.BlockSpec((tm, n), lambda i: (i, 0)),
        out_shape=jax.ShapeDtypeStruct((t, n), out_dtype),
        compiler_params=_cparams(1, 48),
    )(x2, g.reshape(1, d), sc, sh, w_bf)


def _gelu_tanh(y):
    c = math.sqrt(2.0 / math.pi)
    return y * (0.5 * (1.0 + jnp.tanh(c * (y + 0.044715 * (y * y * y)))))


def _s5_kernel(u_ref, rate_ref, theta_ref, bblk_ref, cblk_ref, d_ref, z_ref,
               apr, api, air, aii, xr, xi):
    L = u_ref.shape[0]
    n_super = bblk_ref.shape[0]
    half = apr.shape[1] // n_super
    nch = u_ref.shape[1] // n_super
    b = pl.program_id(0)
    c = pl.program_id(1)

    @pl.when((b == 0) & (c == 0))
    def _():
        tt = lax.broadcasted_iota(jnp.int32, (L, 1), 0).astype(F32)
        ph = theta_ref[...] * tt
        rt = rate_ref[...] * tt
        cs = jnp.cos(ph)
        sn = jnp.sin(ph)
        mag = jnp.exp(rt)
        inv = jnp.exp(-rt)
        apr[...] = mag * cs
        api[...] = mag * sn
        air[...] = inv * cs
        aii[...] = -(inv * sn)

    @pl.when(c == 0)
    def _():
        xr[...] = jnp.zeros_like(xr)
        xi[...] = jnp.zeros_like(xi)

    row = lax.broadcasted_iota(jnp.int32, (L, L), 0)
    col = lax.broadcasted_iota(jnp.int32, (L, L), 1)
    tril = (row >= col).astype(BF16)

    for sg in range(n_super):
        cols = slice(sg * half, (sg + 1) * half)
        ch = slice(sg * nch, (sg + 1) * nch)
        u = u_ref[:, ch]
        bu = jnp.dot(u.astype(BF16), bblk_ref[sg], preferred_element_type=F32)
        bur = bu[:, :half]
        bui = bu[:, half:]
        ir = air[:, cols]
        ii = aii[:, cols]
        zc = jnp.concatenate([bur * ir - bui * ii, bur * ii + bui * ir], axis=1)
        cum = jnp.dot(tril, zc.astype(BF16), preferred_element_type=F32)
        a_r = apr[1:2, cols]
        a_i = api[1:2, cols]
        p_r = xr[:, cols]
        p_i = xi[:, cols]
        cr = cum[:, :half] + (a_r * p_r - a_i * p_i)
        ci = cum[:, half:] + (a_r * p_i + a_i * p_r)
        pw_r = apr[:, cols]
        pw_i = api[:, cols]
        x_re = cr * pw_r - ci * pw_i
        x_im = cr * pw_i + ci * pw_r
        xr[:, cols] = x_re[L - 1:L, :]
        xi[:, cols] = x_im[L - 1:L, :]
        xc = jnp.concatenate([x_re, x_im], axis=1).astype(BF16)
        y = jnp.dot(xc, cblk_ref[sg], preferred_element_type=F32) + d_ref[:, ch] * u
        z_ref[:, ch] = _gelu_tanh(y)


def _s5_scan(u, rate, theta, bblk, cblk, d_skip, bsz, seq):
    t, w = u.shape
    L = S5_CHUNK
    nc = seq // L
    ns = rate.shape[1]
    return pl.pallas_call(
        _s5_kernel,
        grid=(bsz, nc),
        in_specs=[pl.BlockSpec((L, w), lambda b, c: (b * nc + c, 0)),
                  pl.BlockSpec((1, ns), lambda b, c: (0, 0)),
                  pl.BlockSpec((1, ns), lambda b, c: (0, 0)),
                  pl.BlockSpec(bblk.shape, lambda b, c: (0, 0, 0)),
                  pl.BlockSpec(cblk.shape, lambda b, c: (0, 0, 0)),
                  pl.BlockSpec((1, w), lambda b, c: (0, 0))],
        out_specs=pl.BlockSpec((L, w), lambda b, c: (b * nc + c, 0)),
        out_shape=jax.ShapeDtypeStruct((t, w), F32),
        scratch_shapes=[pltpu.VMEM((L, ns), F32)] * 4 + [pltpu.VMEM((1, ns), F32)] * 2,
        compiler_params=_cparams(2, 58),
    )(u, rate, theta, bblk, cblk, d_skip.reshape(1, w))


def _s5_params(b_re, b_im, c_re, c_im, lam_re, lam_im, log_dt):
    g, p, ch = b_re.shape
    dt = jnp.exp(log_dt)[:, None]
    rate = lam_re * dt
    theta = lam_im * dt
    mag = jnp.exp(rate)
    a_re = mag * jnp.cos(theta)
    a_im = mag * jnp.sin(theta)
    den = lam_re * lam_re + lam_im * lam_im
    f_re = ((a_re - 1.0) * lam_re + a_im * lam_im) / den
    f_im = (a_im * lam_re - (a_re - 1.0) * lam_im) / den
    bb_re = f_re[..., None] * b_re - f_im[..., None] * b_im
    bb_im = f_re[..., None] * b_im + f_im[..., None] * b_re
    ns = S5_SUPER
    nsg = g // ns
    eye = jnp.eye(ns, dtype=F32)

    def blk_b(bb):
        return jnp.einsum('sgpc,gh->sgchp', bb.reshape(nsg, ns, p, ch), eye).reshape(nsg, ns * ch, ns * p)

    def blk_c(cc):
        return jnp.einsum('sgcp,gh->sgphc', cc.reshape(nsg, ns, ch, p), eye).reshape(nsg, ns * p, ns * ch)

    bblk = jnp.concatenate([blk_b(bb_re), blk_b(bb_im)], axis=2).astype(BF16)
    cblk = jnp.concatenate([blk_c(c_re), -blk_c(c_im)], axis=1).astype(BF16)
    return rate.reshape(1, g * p), theta.reshape(1, g * p), bblk, cblk


def _route(x, g, sc, sh, wr, br):
    h = _normmod(x, g, sc, sh)
    logits = jnp.dot(h, wr, preferred_element_type=F32, precision=HI) + br
    lane = lax.broadcasted_iota(jnp.int32, logits.shape, 1)
    lane_f = lane.astype(F32)
    work = logits
    onehot = jnp.zeros(logits.shape, F32)
    ei = jnp.zeros(logits.shape, F32)
    vals = []
    for kk in range(TOP_K):
        m = jnp.max(work, axis=-1, keepdims=True)
        idx = jnp.min(jnp.where(work == m, lane_f, float(LANES)), axis=-1, keepdims=True)
        hit = lane_f == idx
        onehot = jnp.where(hit, 1.0, onehot)
        ei = jnp.where(lane == kk, idx, ei)
        vals.append(m)
        work = jnp.where(hit, 2.0 * NEG, work)
    ex = [jnp.exp(vv - vals[0]) for vv in vals]
    tot = ex[0] + ex[1] + ex[2] + ex[3]
    gt = jnp.zeros(logits.shape, F32)
    for kk in range(TOP_K):
        gt = jnp.where(lane == kk, ex[kk] / tot, gt)
    return _pack_rows(h), onehot, ei, gt


def _glu_out_kernel(z_ref, x_ref, g_ref, wg_ref, wo_ref, n2_ref, sc_ref, sh_ref, wr_ref, br_ref,
                    o_ref, h_ref, oh_ref, ei_ref, gt_ref):
    z = z_ref[...]
    gate = jax.nn.sigmoid(jnp.dot(z.astype(BF16), wg_ref[...], preferred_element_type=F32))
    m = jnp.dot((z * gate).astype(BF16), wo_ref[...], preferred_element_type=F32)
    xn = x_ref[...] + g_ref[0] * m
    o_ref[...] = xn
    h_ref[...], oh_ref[...], ei_ref[...], gt_ref[...] = _route(
        xn, n2_ref[...], sc_ref[0], sh_ref[0], wr_ref[...], br_ref[...])


def _lin_res_kernel(a_ref, x_ref, g_ref, w_ref, n2_ref, sc_ref, sh_ref, wr_ref, br_ref,
                    o_ref, h_ref, oh_ref, ei_ref, gt_ref):
    m = jnp.dot(a_ref[...].astype(BF16), w_ref[...], preferred_element_type=F32)
    xn = x_ref[...] + g_ref[0] * m
    o_ref[...] = xn
    h_ref[...], oh_ref[...], ei_ref[...], gt_ref[...] = _route(
        xn, n2_ref[...], sc_ref[0], sh_ref[0], wr_ref[...], br_ref[...])


def _mixer_out_and_route(kernel, acts, x2, gate, weights, norm2_g, sc2, sh2, w_router, b_router, seq):
    t, d = x2.shape
    ne = w_router.shape[1]
    wr_pad = jnp.zeros((d, LANES), F32).at[:, :ne].set(w_router)
    br_pad = jnp.full((1, LANES), NEG, F32).at[0, :ne].set(b_router)
    tm = TM_DENSE
    tpb = seq // tm
    row = pl.BlockSpec((tm, d), lambda i: (i, 0))
    arow = pl.BlockSpec((tm, acts.shape[1]), lambda i: (i, 0))
    per_batch = pl.BlockSpec((1, 1, d), lambda i: (i // tpb, 0, 0))
    lane_tile = pl.BlockSpec((tm, LANES), lambda i: (i, 0))
    whole = lambda a: pl.BlockSpec(a.shape, lambda i: (0,) * a.ndim)
    g2 = norm2_g.reshape(1, d)
    return pl.pallas_call(
        kernel,
        grid=(t // tm,),
        in_specs=[arow, row, per_batch] + [whole(w) for w in weights]
                 + [whole(g2), per_batch, per_batch, whole(wr_pad), whole(br_pad)],
        out_specs=[row, pl.BlockSpec((tm, d // 2), lambda i: (i, 0)), lane_tile, lane_tile, lane_tile],
        out_shape=[jax.ShapeDtypeStruct((t, d), F32), jax.ShapeDtypeStruct((t, d // 2), jnp.uint32)]
                  + [jax.ShapeDtypeStruct((t, LANES), F32)] * 3,
        compiler_params=_cparams(1, 56),
    )(acts, x2, gate, *weights, g2, sc2, sh2, wr_pad, br_pad)


def _rope_tab_kernel(pos_ref, inv_ref, sgn_ref, cos_ref, sin_ref):
    ang = pos_ref[...].astype(F32) * inv_ref[...]
    cos_ref[...] = jnp.cos(ang)
    sin_ref[...] = jnp.sin(ang) * sgn_ref[...]


def _rope_tables(pos_col):
    t = pos_col.shape[0]
    ts = 512
    half = HEAD_DIM // 2
    inv = ROPE_THETA ** (-jnp.arange(0, HEAD_DIM, 2, dtype=F32) / HEAD_DIM)
    inv = jnp.concatenate([inv, inv]).reshape(1, HEAD_DIM)
    sgn = jnp.concatenate([-jnp.ones((half,), F32), jnp.ones((half,), F32)]).reshape(1, HEAD_DIM)
    const = pl.BlockSpec((1, HEAD_DIM), lambda i: (0, 0))
    tile = pl.BlockSpec((ts, HEAD_DIM), lambda i: (i, 0))
    return pl.pallas_call(
        _rope_tab_kernel,
        grid=(t // ts,),
        in_specs=[pl.BlockSpec((ts, 1), lambda i: (i, 0)), const, const],
        out_specs=[tile, tile],
        out_shape=[jax.ShapeDtypeStruct((t, HEAD_DIM), F32)] * 2,
        compiler_params=_cparams(1, 16),
    )(pos_col, inv, sgn)


def _rope_kernel(q_ref, k_ref, v_ref, cos_ref, sin_ref, qo_ref, ko_ref, vo_ref, km_ref):
    cs = cos_ref[...]
    sn = sin_ref[...]
    dh = cs.shape[1]
    half = dh // 2
    for hh in range(qo_ref.shape[1]):
        cols = slice(hh * dh, (hh + 1) * dh)
        q = q_ref[:, cols].astype(F32)
        k = k_ref[:, cols].astype(F32)
        qr = q * cs + pltpu.roll(q, half, 1) * sn
        kr = k * cs + pltpu.roll(k, half, 1) * sn
        qo_ref[0, hh] = qr
        ko_ref[0, hh] = kr.astype(BF16)
        vo_ref[0, hh, 0] = v_ref[:, cols].astype(F32).T.astype(BF16)
        km_ref[0, hh, 0] = jnp.mean(kr, axis=0, keepdims=True)


def _rope(qkv, cos, sin, bsz, seq):
    t = qkv.shape[0]
    h, dh, blk = N_HEADS, HEAD_DIM, MOBA_BLOCK
    nb = seq // blk
    part = lambda p: pl.BlockSpec((blk, h * dh), lambda i: (i, p))
    tab = pl.BlockSpec((blk, dh), lambda i: (i, 0))
    head = pl.BlockSpec((1, h, blk, dh), lambda i: (i // nb, 0, i % nb, 0))
    return pl.pallas_call(
        _rope_kernel,
        grid=(t // blk,),
        in_specs=[part(0), part(1), part(2), tab, tab],
        out_specs=[head, head,
                   pl.BlockSpec((1, h, 1, dh, blk), lambda i: (i // nb, 0, i % nb, 0, 0)),
                   pl.BlockSpec((1, h, 1, 1, dh), lambda i: (i // nb, 0, i % nb, 0, 0))],
        out_shape=[jax.ShapeDtypeStruct((bsz, h, seq, dh), F32),
                   jax.ShapeDtypeStruct((bsz, h, seq, dh), BF16),
                   jax.ShapeDtypeStruct((bsz, h, nb, dh, blk), BF16),
                   jax.ShapeDtypeStruct((bsz, h, nb, 1, dh), F32)],
        compiler_params=_cparams(1, 32),
    )(qkv, qkv, qkv, cos, sin)


def _moba_kernel(q_ref, k_ref, vt_ref, km_ref, o_ref, sel_scr, s_a, s_b):
    nh = q_ref.shape[1]
    blk = q_ref.shape[2]
    dh = q_ref.shape[3]
    j = pl.program_id(2)
    nt = (((1,), (1,)), ((), ()))
    heads = range(nh)

    qs = []
    for hh in heads:
        q = q_ref[0, hh]
        gate = lax.dot_general(km_ref[0, hh], q, nt, preferred_element_type=F32, precision=HI)
        bidx = lax.broadcasted_iota(jnp.int32, gate.shape, 0)
        bidx_f = bidx.astype(F32)
        work = jnp.where(bidx < j, gate, NEG)
        sel = jnp.zeros(gate.shape, F32)
        for _ in range(MOBA_TOPK):
            m = jnp.max(work, axis=0, keepdims=True)
            idx = jnp.min(jnp.where(work == m, bidx_f, float(LANES)), axis=0, keepdims=True)
            hit = bidx_f == idx
            sel = jnp.where(hit & (m > 0.5 * NEG), 1.0, sel)
            work = jnp.where(hit, 2.0 * NEG, work)
        sel_scr[hh] = sel
        qs.append((q * (dh ** -0.5 * LOG2E)).astype(BF16))

    def scores(hh, n):
        start = pl.multiple_of(n * blk, blk)
        kb = k_ref[0, hh, pl.ds(start, blk), :]
        return lax.dot_general(kb, qs[hh], nt, preferred_element_type=F32)

    nb_last = vt_ref.shape[2] - 1

    def score_trip(scr, nn):
        for hh in heads:
            for bb in range(2):
                scr[hh, bb] = scores(hh, jnp.minimum(2 * nn + bb, nb_last))

    def past_pair(hh, nn, scr, m_i, l_i, acc):
        n0 = jnp.minimum(2 * nn, nb_last)
        n1 = jnp.minimum(2 * nn + 1, nb_last)
        s0 = scr[hh, 0]
        s1 = scr[hh, 1]
        r0 = (sel_scr[hh, pl.ds(n0, 1), :] > 0.5) & (2 * nn < j)
        r1 = (sel_scr[hh, pl.ds(n1, 1), :] > 0.5) & (2 * nn + 1 < j)
        rm0 = jnp.where(r0, jnp.max(s0, axis=0, keepdims=True), NEG)
        rm1 = jnp.where(r1, jnp.max(s1, axis=0, keepdims=True), NEG)
        m_new = jnp.maximum(m_i, jnp.maximum(rm0, rm1))
        alpha = jnp.exp2(m_i - m_new)
        p0 = jnp.exp2(s0 - jnp.where(r0, m_new, -NEG))
        p1 = jnp.exp2(s1 - jnp.where(r1, m_new, -NEG))
        l_new = alpha * l_i + (jnp.sum(p0, axis=0, keepdims=True) + jnp.sum(p1, axis=0, keepdims=True))
        pv = (jnp.dot(vt_ref[0, hh, n0], p0.astype(BF16), preferred_element_type=F32)
              + jnp.dot(vt_ref[0, hh, n1], p1.astype(BF16), preferred_element_type=F32))
        return m_new, l_new, alpha * acc + pv

    def body(mm, state):
        score_trip(s_b, 2 * mm + 1)
        state = tuple(past_pair(hh, 2 * mm, s_a, *state[hh]) for hh in heads)
        score_trip(s_a, 2 * mm + 2)
        return tuple(past_pair(hh, 2 * mm + 1, s_b, *state[hh]) for hh in heads)

    init = tuple((jnp.full((1, blk), NEG, F32), jnp.zeros((1, blk), F32), jnp.zeros((dh, blk), F32))
                 for _ in heads)
    score_trip(s_a, 0)
    n_trips = (j + 1) // 2
    carry = lax.fori_loop(0, (n_trips + 1) // 2, body, init)

    for hh in heads:
        m_i, l_i, acc = carry[hh]
        s = scores(hh, j)
        kk = lax.broadcasted_iota(jnp.int32, s.shape, 0)
        qq = lax.broadcasted_iota(jnp.int32, s.shape, 1)
        s = jnp.where(kk <= qq, s, NEG)
        m_new = jnp.maximum(m_i, jnp.max(s, axis=0, keepdims=True))
        alpha = jnp.exp2(m_i - m_new)
        p = jnp.exp2(s - m_new)
        l_i = alpha * l_i + jnp.sum(p, axis=0, keepdims=True)
        acc = alpha * acc + jnp.dot(vt_ref[0, hh, j], p.astype(BF16), preferred_element_type=F32)
        o_ref[:, hh * dh:(hh + 1) * dh] = (acc / l_i).T.astype(o_ref.dtype)


def _moba(q, k, vt, km_pad, bsz, seq):
    h, dh, blk = N_HEADS, HEAD_DIM, MOBA_BLOCK
    nb = seq // blk
    nbp = km_pad.shape[2]
    nh = MOBA_HEADS_PER_STEP
    return pl.pallas_call(
        _moba_kernel,
        grid=(bsz, h // nh, nb),
        in_specs=[pl.BlockSpec((1, nh, blk, dh), lambda b, j, i: (b, j, i, 0)),
                  pl.BlockSpec((1, nh, seq, dh), lambda b, j, i: (b, j, 0, 0)),
                  pl.BlockSpec((1, nh, nb, dh, blk), lambda b, j, i: (b, j, 0, 0, 0)),
                  pl.BlockSpec((1, nh, nbp, dh), lambda b, j, i: (b, j, 0, 0))],
        out_specs=pl.BlockSpec((blk, nh * dh), lambda b, j, i: (b * nb + i, j)),
        out_shape=jax.ShapeDtypeStruct((bsz * seq, h * dh), BF16),
        scratch_shapes=[pltpu.VMEM((nh, nbp, blk), F32)] + [pltpu.VMEM((nh, 2, blk, blk), F32)] * 2,
        compiler_params=_cparams(3, 48),
    )(q, k, vt, km_pad)


def _pos_kernel(oh_ref, ei_ref, dest_ref, meta_ref, cnt_scr, off_scr, run_scr):
    p = pl.program_id(0)
    i = pl.program_id(1)
    tm = oh_ref.shape[0]
    oh = oh_ref[...]
    colsum = jnp.sum(oh, axis=0, keepdims=True)

    @pl.when((p == 0) & (i == 0))
    def _():
        cnt_scr[...] = jnp.zeros_like(cnt_scr)

    @pl.when(p == 0)
    def _():
        cnt_scr[...] += colsum
        dest_ref[...] = jnp.zeros_like(dest_ref)

    @pl.when((p == 1) & (i == 0))
    def _():
        cnt = cnt_scr[...]
        padded = jnp.ceil(cnt * (1.0 / TM_EXPERT)) * float(TM_EXPERT)
        r = lax.broadcasted_iota(jnp.int32, (LANES, LANES), 0)
        cidx = lax.broadcasted_iota(jnp.int32, (LANES, LANES), 1)
        tri = (r < cidx).astype(F32)
        off = jnp.dot(jnp.broadcast_to(padded, (8, LANES)), tri,
                      preferred_element_type=F32, precision=HI)
        off_scr[...] = off[0:1]
        run_scr[...] = jnp.zeros_like(run_scr)

    @pl.when(p == 1)
    def _():
        r = lax.broadcasted_iota(jnp.int32, (tm, tm), 0)
        cidx = lax.broadcasted_iota(jnp.int32, (tm, tm), 1)
        ltri = (r > cidx).astype(BF16)
        excl = jnp.dot(ltri, oh.astype(BF16), preferred_element_type=F32)
        dfull = off_scr[...] + run_scr[...] + excl
        ei = ei_ref[...]
        lane = lax.broadcasted_iota(jnp.int32, oh.shape, 1)
        lane_f = lane.astype(F32)
        dest = jnp.zeros(oh.shape, F32)
        for kk in range(TOP_K):
            ek = jnp.sum(jnp.where(lane == kk, ei, 0.0), axis=-1, keepdims=True)
            dk = jnp.sum(jnp.where(lane_f == ek, dfull, 0.0), axis=-1, keepdims=True)
            dest = jnp.where(lane == kk, dk, dest)
        dest_ref[...] = dest.astype(jnp.int32)
        run_scr[...] += colsum

    meta_ref[...] = jnp.concatenate([off_scr[...], cnt_scr[...],
                                     jnp.zeros((6, LANES), F32)], axis=0)


def _positions(onehot, ei):
    t = onehot.shape[0]
    tm = TM_DENSE
    tile = pl.BlockSpec((tm, LANES), lambda p, i: (i, 0))
    return pl.pallas_call(
        _pos_kernel,
        grid=(2, t // tm),
        in_specs=[tile, tile],
        out_specs=[pl.BlockSpec((tm, LANES), lambda p, i: (i * p, 0)),
                   pl.BlockSpec((8, LANES), lambda p, i: (0, 0))],
        out_shape=[jax.ShapeDtypeStruct((t, LANES), jnp.int32),
                   jax.ShapeDtypeStruct((8, LANES), F32)],
        scratch_shapes=[pltpu.VMEM((1, LANES), F32)] * 3,
        compiler_params=_cparams(2, 16),
    )(onehot, ei)


def _src_kernel(dest_ref, src_ref):
    i = pl.program_id(0)
    tb = dest_ref.shape[0] // TOP_K

    @pl.when(i == 0)
    def _():
        def zero(j, carry):
            src_ref[j] = 0
            return carry
        lax.fori_loop(0, src_ref.shape[0], zero, 0, unroll=16)

    def body(tt, carry):
        for kk in range(TOP_K):
            src_ref[dest_ref[tt * TOP_K + kk]] = i * tb + tt
        return carry

    lax.fori_loop(0, tb, body, 0, unroll=4)


def _sources(dest_flat, n_rows):
    n = dest_flat.shape[0]
    tb = TM_SRC
    return pl.pallas_call(
        _src_kernel,
        grid=(n // (tb * TOP_K),),
        in_specs=[pl.BlockSpec((tb * TOP_K,), lambda i: (i,), memory_space=pltpu.SMEM)],
        out_specs=pl.BlockSpec(memory_space=pltpu.SMEM),
        out_shape=jax.ShapeDtypeStruct((n_rows,), jnp.int32),
        compiler_params=_cparams(1, 16),
    )(dest_flat)


def _expert_kernel(te_ref, tv_ref, src_cur, src_nxt, h_hbm, wgu_ref, bgu_ref, wd_ref, bd_ref,
                   o_ref, xbuf, sem, wgu_bf, wd_bf):
    i = pl.program_id(0)
    nt = pl.num_programs(0)
    tm = o_ref.shape[0]
    slot = lax.rem(i, 2)
    e = te_ref[i]
    prev = te_ref[jnp.maximum(i - 1, 0)]
    dff = wd_ref.shape[1]
    def gather(src_ref, s):
        def body(r8, carry):
            for uu in range(8):
                pltpu.make_async_copy(h_hbm.at[pl.ds(src_ref[r8 * 8 + uu], 1)],
                                      xbuf.at[s, r8, pl.ds(uu, 1)], sem.at[s]).start()
            return carry
        lax.fori_loop(0, tm // 8, body, 0)

    @pl.when((i == 0) & (tv_ref[0] > 0))
    def _():
        gather(src_cur, 0)

    nxt = jnp.minimum(i + 1, nt - 1)

    @pl.when((i + 1 < nt) & (tv_ref[nxt] > 0))
    def _():
        gather(src_nxt, 1 - slot)

    @pl.when((i == 0) | (e != prev))
    def _():
        wgu_bf[...] = wgu_ref[0].astype(BF16)
        wd_bf[...] = wd_ref[0].astype(BF16)

    @pl.when(tv_ref[i] > 0)
    def _():
        for r8 in range(tm // 8):
            pltpu.make_async_copy(h_hbm.at[pl.ds(0, 8)], xbuf.at[slot, r8], sem.at[slot]).wait()
        x = _unpack_rows(xbuf[slot].reshape(tm, xbuf.shape[3])).astype(BF16)
        hu = jnp.dot(x, wgu_bf[...], preferred_element_type=F32) + bgu_ref[0]
        g = jnp.minimum(hu[:, :dff], SWIGLU_LIMIT)
        up = jnp.clip(hu[:, dff:], -SWIGLU_LIMIT, SWIGLU_LIMIT)
        act = (up + 1.0) * g * jax.nn.sigmoid(SWIGLU_ALPHA * g)
        y = jnp.dot(act.astype(BF16), wd_bf[...], preferred_element_type=F32) + bd_ref[0]
        o_ref[...] = _pack_rows(y)

    @pl.when(tv_ref[i] == 0)
    def _():
        o_ref[...] = jnp.zeros_like(o_ref)


def _experts(tile_e, tile_v, src, h2p, w_gu, b_gu, w_d, b_d):
    n_rows = src.shape[0]
    dp = h2p.shape[1]
    ne, d, f2 = w_gu.shape
    dff = w_d.shape[1]
    tm = TM_EXPERT
    nt = n_rows // tm
    grid_spec = pltpu.PrefetchScalarGridSpec(
        num_scalar_prefetch=2,
        grid=(nt,),
        in_specs=[pl.BlockSpec((tm,), lambda i, te, tv: (i,), memory_space=pltpu.SMEM),
                  pl.BlockSpec((tm,), lambda i, te, tv: (jnp.minimum(i + 1, nt - 1),),
                               memory_space=pltpu.SMEM),
                  pl.BlockSpec(memory_space=pl.ANY),
                  pl.BlockSpec((1, d, f2), lambda i, te, tv: (te[i], 0, 0)),
                  pl.BlockSpec((1, 1, f2), lambda i, te, tv: (te[i], 0, 0)),
                  pl.BlockSpec((1, dff, d), lambda i, te, tv: (te[i], 0, 0)),
                  pl.BlockSpec((1, 1, d), lambda i, te, tv: (te[i], 0, 0))],
        out_specs=pl.BlockSpec((tm, dp), lambda i, te, tv: (i, 0)),
        scratch_shapes=[pltpu.VMEM((2, tm // 8, 8, dp), jnp.uint32), pltpu.SemaphoreType.DMA((2,)),
                        pltpu.VMEM((d, f2), BF16), pltpu.VMEM((dff, d), BF16)],
    )
    return pl.pallas_call(
        _expert_kernel,
        grid_spec=grid_spec,
        out_shape=jax.ShapeDtypeStruct((n_rows, dp), jnp.uint32),
        compiler_params=_cparams(1, 56),
    )(tile_e, tile_v, src, src, h2p, w_gu, b_gu.reshape(ne, 1, f2), w_d, b_d.reshape(ne, 1, d))


def _combine_kernel(dcur, dnxt, gt_ref, x_ref, g_ref, gf_ref, ys_hbm, o_ref, buf, sem, *, final):
    i = pl.program_id(0)
    nt = pl.num_programs(0)
    td = x_ref.shape[0]
    slot = lax.rem(i, 2)

    def gather(dref, s):
        def body(t8, carry):
            for uu in range(8):
                for kk in range(TOP_K):
                    pltpu.make_async_copy(ys_hbm.at[pl.ds(dref[(t8 * 8 + uu) * TOP_K + kk], 1)],
                                          buf.at[s, kk, t8, pl.ds(uu, 1)], sem.at[s]).start()
            return carry
        lax.fori_loop(0, td // 8, body, 0)

    @pl.when(i == 0)
    def _():
        gather(dcur, 0)

    @pl.when(i + 1 < nt)
    def _():
        gather(dnxt, 1 - slot)

    for kk in range(TOP_K):
        for t8 in range(td // 8):
            pltpu.make_async_copy(ys_hbm.at[pl.ds(0, 8)], buf.at[slot, kk, t8], sem.at[slot]).wait()
    gt = gt_ref[...]
    rows = lambda kk: _unpack_rows(buf[slot, kk].reshape(td, buf.shape[4]))
    y = gt[:, 0:1] * rows(0)
    for kk in range(1, TOP_K):
        y = y + gt[:, kk:kk + 1] * rows(kk)
    xn = x_ref[...] + g_ref[0] * y
    if final:
        ms = jnp.mean(xn * xn, axis=-1, keepdims=True)
        xn = (xn * lax.rsqrt(ms + EPS)) * gf_ref[...]
    o_ref[...] = xn


def _combine(dest_flat, gates, x2, gate_vec, gf, ys, seq, final):
    t, d = x2.shape
    td = TD_ROWS
    tpb = seq // td
    nt = t // td
    return pl.pallas_call(
        functools.partial(_combine_kernel, final=final),
        grid=(nt,),
        in_specs=[pl.BlockSpec((td * TOP_K,), lambda i: (i,), memory_space=pltpu.SMEM),
                  pl.BlockSpec((td * TOP_K,), lambda i: (jnp.minimum(i + 1, nt - 1),),
                               memory_space=pltpu.SMEM),
                  pl.BlockSpec((td, LANES), lambda i: (i, 0)),
                  pl.BlockSpec((td, d), lambda i: (i, 0)),
                  pl.BlockSpec((1, 1, d), lambda i: (i // tpb, 0, 0)),
                  pl.BlockSpec((1, d), lambda i: (0, 0)),
                  pl.BlockSpec(memory_space=pl.ANY)],
        out_specs=pl.BlockSpec((td, d), lambda i: (i, 0)),
        out_shape=jax.ShapeDtypeStruct((t, d), F32),
        scratch_shapes=[pltpu.VMEM((2, TOP_K, td // 8, 8, d // 2), jnp.uint32),
                        pltpu.SemaphoreType.DMA((2,))],
        compiler_params=_cparams(1, 32),
    )(dest_flat, dest_flat, gates, x2, gate_vec, gf.reshape(1, d), ys)


def _moe(x2, routed, gate_vec, w_gu, b_gu, w_d, b_d, gf, seq, final):
    t, d = x2.shape
    ne = w_gu.shape[0]
    h2, onehot, ei, gates = routed
    dest, meta = _positions(onehot, ei)
    dest_flat = dest[:, :TOP_K].reshape(t * TOP_K)

    tm = TM_EXPERT
    n_rows = t * TOP_K + ne * tm
    n_tiles = n_rows // tm
    off = meta[0, :ne].astype(jnp.int32)
    cnt = meta[1, :ne].astype(jnp.int32)
    ends = off + ((cnt + tm - 1) // tm) * tm
    starts = jnp.arange(n_tiles, dtype=jnp.int32) * tm
    valid = starts < ends[-1]
    te = jnp.sum((starts[:, None] >= ends[None, :]).astype(jnp.int32), axis=1)
    te_last = jnp.sum((ends[-1] - 1 >= ends).astype(jnp.int32))
    tile_e = jnp.minimum(jnp.where(valid, te, te_last), ne - 1)
    tile_v = valid.astype(jnp.int32)

    src = _sources(dest_flat, n_rows)
    ys = _experts(tile_e, tile_v, src, h2, w_gu, b_gu, w_d, b_d)
    return _combine(dest_flat, gates, x2, gate_vec, gf, ys, seq, final)


def kernel(x, c, positions, l0_norm1_g, l0_ada_w, l0_ada_b, l0_s5_w_in, l0_s5_b_re, l0_s5_b_im, l0_s5_c_re, l0_s5_c_im, l0_s5_lam_re, l0_s5_lam_im, l0_s5_log_dt, l0_s5_d, l0_s5_w_glu, l0_s5_w_out, l0_norm2_g, l0_moe_w_router, l0_moe_b_router, l0_moe_w_gate_up, l0_moe_b_gate_up, l0_moe_w_down, l0_moe_b_down, l1_norm1_g, l1_ada_w, l1_ada_b, l1_moba_w_qkv, l1_moba_w_o, l1_norm2_g, l1_moe_w_router, l1_moe_b_router, l1_moe_w_gate_up, l1_moe_b_gate_up, l1_moe_w_down, l1_moe_b_down, final_norm_g):
    bsz, seq, d = x.shape
    t = bsz * seq
    assert seq % TM_DENSE == 0 and seq % MOBA_BLOCK == 0 and seq % S5_CHUNK == 0
    assert seq // MOBA_BLOCK <= LANES and d == N_HEADS * HEAD_DIM
    x2 = x.reshape(t, d)
    c_pad = jnp.zeros((8, d), F32).at[:bsz].set(c)

    def ada_parts(w, b):
        a = _ada(c_pad, w, b)[:bsz]
        return [a[:, i * d:(i + 1) * d].reshape(bsz, 1, d) for i in range(6)]

    sh1, sc1, g1, sh2, sc2, g2 = ada_parts(l0_ada_w, l0_ada_b)
    u = _nml(x2, l0_norm1_g, sc1, sh1, l0_s5_w_in.astype(BF16), seq, F32)
    rate, theta, bblk, cblk = _s5_params(l0_s5_b_re, l0_s5_b_im, l0_s5_c_re, l0_s5_c_im,
                                         l0_s5_lam_re, l0_s5_lam_im, l0_s5_log_dt)
    z = _s5_scan(u, rate, theta, bblk, cblk, l0_s5_d, bsz, seq)
    x2, *routed = _mixer_out_and_route(
        _glu_out_kernel, z, x2, g1, [l0_s5_w_glu.astype(BF16), l0_s5_w_out.astype(BF16)],
        l0_norm2_g, sc2, sh2, l0_moe_w_router, l0_moe_b_router, seq)
    x2 = _moe(x2, routed, g2, l0_moe_w_gate_up, l0_moe_b_gate_up, l0_moe_w_down, l0_moe_b_down,
              final_norm_g, seq, final=False)

    sh1, sc1, g1, sh2, sc2, g2 = ada_parts(l1_ada_w, l1_ada_b)
    qkv = _nml(x2, l1_norm1_g, sc1, sh1, l1_moba_w_qkv.astype(BF16), seq, BF16)
    cos, sin = _rope_tables(positions.reshape(t, 1))
    q, k, v, km = _rope(qkv, cos, sin, bsz, seq)
    nb = seq // MOBA_BLOCK
    nbp = -(-nb // 8) * 8
    km_pad = jnp.zeros((bsz, N_HEADS, nbp, HEAD_DIM), F32).at[:, :, :nb].set(km[:, :, :, 0])
    o = _moba(q, k, v, km_pad, bsz, seq)
    x2, *routed = _mixer_out_and_route(
        _lin_res_kernel, o, x2, g1, [l1_moba_w_o.astype(BF16)],
        l1_norm2_g, sc2, sh2, l1_moe_w_router, l1_moe_b_router, seq)
    x2 = _moe(x2, routed, g2, l1_moe_w_gate_up, l1_moe_b_gate_up, l1_moe_w_down, l1_moe_b_down,
              final_norm_g, seq, final=True)
    return x2.reshape(bsz, seq, d)
```

```python
import functools
import math

import jax
import jax.numpy as jnp
from jax import lax
from jax.experimental import pallas as pl
from jax.experimental.pallas import tpu as pltpu

F32 = jnp.float32
BF16 = jnp.bfloat16
HI = lax.Precision.HIGHEST

EPS = 1e-6
NEG = -1e30
LOG2E = math.log2(math.e)
LANES = 128
MIB = 1024 * 1024

S5_GROUP = 16
S5_STATE = 64
N_HEADS = 8
HEAD_DIM = 128
MOBA_BLOCK = 256
MOBA_TOPK = 3
ROPE_THETA = 10000.0
N_EXPERTS = 32
TOP_K = 4
SWIGLU_LIMIT = 7.0
SWIGLU_ALPHA = 1.702

TM_DENSE = 512
S5_CHUNK = 256
S5_SUPER = 8
MOBA_HEADS_PER_STEP = 2
TM_EXPERT = 256
TD_ROWS = 128
TM_SRC = 2048


def _cparams(n_axes, vmem_mib):
    return pltpu.CompilerParams(dimension_semantics=("arbitrary",) * n_axes,
                                vmem_limit_bytes=vmem_mib * MIB)


def _pack_rows(x):
    n = x.shape[1] // 2
    xb = x.astype(BF16).astype(F32)
    lo = lax.bitcast_convert_type(xb[:, :n], jnp.uint32) >> 16
    hi = lax.bitcast_convert_type(xb[:, n:], jnp.uint32) & jnp.uint32(0xFFFF0000)
    return lo | hi


def _unpack_rows(p):
    lo = lax.bitcast_convert_type(p << 16, F32)
    hi = lax.bitcast_convert_type(p & jnp.uint32(0xFFFF0000), F32)
    return jnp.concatenate([lo, hi], axis=1)


def _normmod(x, g, sc, sh):
    ms = jnp.mean(x * x, axis=-1, keepdims=True)
    return (x * lax.rsqrt(ms + EPS)) * g * (1.0 + sc) + sh


def _ada_kernel(c_ref, w_ref, b_ref, o_ref):
    c = c_ref[...]
    ca = c * jax.nn.sigmoid(c)
    o_ref[...] = jnp.dot(ca, w_ref[...], preferred_element_type=F32, precision=HI) + b_ref[...]


def _ada(c_pad, w, b):
    d, n = w.shape
    tn = 1536
    return pl.pallas_call(
        _ada_kernel,
        grid=(n // tn,),
        in_specs=[pl.BlockSpec((8, d), lambda j: (0, 0)),
                  pl.BlockSpec((d, tn), lambda j: (0, j)),
                  pl.BlockSpec((1, tn), lambda j: (0, j))],
        out_specs=pl.BlockSpec((8, tn), lambda j: (0, j)),
        out_shape=jax.ShapeDtypeStruct((8, n), F32),
        compiler_params=_cparams(1, 32),
    )(c_pad, w, b.reshape(1, n))


def _nml_kernel(x_ref, g_ref, sc_ref, sh_ref, w_ref, o_ref):
    h = _normmod(x_ref[...], g_ref[...], sc_ref[0], sh_ref[0])
    o_ref[...] = jnp.dot(h.astype(BF16), w_ref[...], preferred_element_type=F32).astype(o_ref.dtype)


def _nml(x2, g, sc, sh, w_bf, seq, out_dtype):
    t, d = x2.shape
    n = w_bf.shape[1]
    tm = TM_DENSE
    tpb = seq // tm
    return pl.pallas_call(
        _nml_kernel,
        grid=(t // tm,),
        in_specs=[pl.BlockSpec((tm, d), lambda i: (i, 0)),
                  pl.BlockSpec((1, d), lambda i: (0, 0)),
                  pl.BlockSpec((1, 1, d), lambda i: (i // tpb, 0, 0)),
                  pl.BlockSpec((1, 1, d), lambda i: (i // tpb, 0, 0)),
                  pl.BlockSpec((d, n), lambda i: (0, 0))],
        out_specs=pl.BlockSpec((tm, n), lambda i: (i, 0)),
        out_shape=jax.ShapeDtypeStruct((t, n), out_dtype),
        compiler_params=_cparams(1, 48),
    )(x2, g.reshape(1, d), sc, sh, w_bf)


def _gelu_tanh(y):
    c = math.sqrt(2.0 / math.pi)
    return y * (0.5 * (1.0 + jnp.tanh(c * (y + 0.044715 * (y * y * y)))))


def _s5_kernel(u_ref, rate_ref, theta_ref, bblk_ref, cblk_ref, d_ref, z_ref,
               apr, api, air, aii, xr, xi):
    L = u_ref.shape[0]
    n_super = bblk_ref.shape[0]
    half = apr.shape[1] // n_super
    nch = u_ref.shape[1] // n_super
    b = pl.program_id(0)
    c = pl.program_id(1)

    @pl.when((b == 0) & (c == 0))
    def _():
        tt = lax.broadcasted_iota(jnp.int32, (L, 1), 0).astype(F32)
        ph = theta_ref[...] * tt
        rt = rate_ref[...] * tt
        cs = jnp.cos(ph)
        sn = jnp.sin(ph)
        mag = jnp.exp(rt)
        inv = jnp.exp(-rt)
        apr[...] = mag * cs
        api[...] = mag * sn
        air[...] = inv * cs
        aii[...] = -(inv * sn)

    @pl.when(c == 0)
    def _():
        xr[...] = jnp.zeros_like(xr)
        xi[...] = jnp.zeros_like(xi)

    row = lax.broadcasted_iota(jnp.int32, (L, L), 0)
    col = lax.broadcasted_iota(jnp.int32, (L, L), 1)
    tril = (row >= col).astype(BF16)

    for sg in range(n_super):
        cols = slice(sg * half, (sg + 1) * half)
        ch = slice(sg * nch, (sg + 1) * nch)
        u = u_ref[:, ch]
        bu = jnp.dot(u.astype(BF16), bblk_ref[sg], preferred_element_type=F32)
        bur = bu[:, :half]
        bui = bu[:, half:]
        ir = air[:, cols]
        ii = aii[:, cols]
        zc = jnp.concatenate([bur * ir - bui * ii, bur * ii + bui * ir], axis=1)
        cum = jnp.dot(tril, zc.astype(BF16), preferred_element_type=F32)
        a_r = apr[1:2, cols]
        a_i = api[1:2, cols]
        p_r = xr[:, cols]
        p_i = xi[:, cols]
        cr = cum[:, :half] + (a_r * p_r - a_i * p_i)
        ci = cum[:, half:] + (a_r * p_i + a_i * p_r)
        pw_r = apr[:, cols]
        pw_i = api[:, cols]
        x_re = cr * pw_r - ci * pw_i
        x_im = cr * pw_i + ci * pw_r
        xr[:, cols] = x_re[L - 1:L, :]
        xi[:, cols] = x_im[L - 1:L, :]
        xc = jnp.concatenate([x_re, x_im], axis=1).astype(BF16)
        y = jnp.dot(xc, cblk_ref[sg], preferred_element_type=F32) + d_ref[:, ch] * u
        z_ref[:, ch] = _gelu_tanh(y)


def _s5_scan(u, rate, theta, bblk, cblk, d_skip, bsz, seq):
    t, w = u.shape
    L = S5_CHUNK
    nc = seq // L
    ns = rate.shape[1]
    return pl.pallas_call(
        _s5_kernel,
        grid=(bsz, nc),
        in_specs=[pl.BlockSpec((L, w), lambda b, c: (b * nc + c, 0)),
                  pl.BlockSpec((1, ns), lambda b, c: (0, 0)),
                  pl.BlockSpec((1, ns), lambda b, c: (0, 0)),
                  pl.BlockSpec(bblk.shape, lambda b, c: (0, 0, 0)),
                  pl.BlockSpec(cblk.shape, lambda b, c: (0, 0, 0)),
                  pl.BlockSpec((1, w), lambda b, c: (0, 0))],
        out_specs=pl.BlockSpec((L, w), lambda b, c: (b * nc + c, 0)),
        out_shape=jax.ShapeDtypeStruct((t, w), F32),
        scratch_shapes=[pltpu.VMEM((L, ns), F32)] * 4 + [pltpu.VMEM((1, ns), F32)] * 2,
        compiler_params=_cparams(2, 58),
    )(u, rate, theta, bblk, cblk, d_skip.reshape(1, w))


def _s5_params(b_re, b_im, c_re, c_im, lam_re, lam_im, log_dt):
    g, p, ch = b_re.shape
    dt = jnp.exp(log_dt)[:, None]
    rate = lam_re * dt
    theta = lam_im * dt
    mag = jnp.exp(rate)
    a_re = mag * jnp.cos(theta)
    a_im = mag * jnp.sin(theta)
    den = lam_re * lam_re + lam_im * lam_im
    f_re = ((a_re - 1.0) * lam_re + a_im * lam_im) / den
    f_im = (a_im * lam_re - (a_re - 1.0) * lam_im) / den
    bb_re = f_re[..., None] * b_re - f_im[..., None] * b_im
    bb_im = f_re[..., None] * b_im + f_im[..., None] * b_re
    ns = S5_SUPER
    nsg = g // ns
    eye = jnp.eye(ns, dtype=F32)

    def blk_b(bb):
        return jnp.einsum('sgpc,gh->sgchp', bb.reshape(nsg, ns, p, ch), eye).reshape(nsg, ns * ch, ns * p)

    def blk_c(cc):
        return jnp.einsum('sgcp,gh->sgphc', cc.reshape(nsg, ns, ch, p), eye).reshape(nsg, ns * p, ns * ch)

    bblk = jnp.concatenate([blk_b(bb_re), blk_b(bb_im)], axis=2).astype(BF16)
    cblk = jnp.concatenate([blk_c(c_re), -blk_c(c_im)], axis=1).astype(BF16)
    return rate.reshape(1, g * p), theta.reshape(1, g * p), bblk, cblk


def _route(x, g, sc, sh, wr, br):
    h = _normmod(x, g, sc, sh)
    logits = jnp.dot(h, wr, preferred_element_type=F32, precision=HI) + br
    lane = lax.broadcasted_iota(jnp.int32, logits.shape, 1)
    lane_f = lane.astype(F32)
    work = logits
    onehot = jnp.zeros(logits.shape, F32)
    ei = jnp.zeros(logits.shape, F32)
    vals = []
    for kk in range(TOP_K):
        m = jnp.max(work, axis=-1, keepdims=True)
        idx = jnp.min(jnp.where(work == m, lane_f, float(LANES)), axis=-1, keepdims=True)
        hit = lane_f == idx
        onehot = jnp.where(hit, 1.0, onehot)
        ei = jnp.where(lane == kk, idx, ei)
        vals.append(m)
        work = jnp.where(hit, 2.0 * NEG, work)
    ex = [jnp.exp(vv - vals[0]) for vv in vals]
    tot = ex[0] + ex[1] + ex[2] + ex[3]
    gt = jnp.zeros(logits.shape, F32)
    for kk in range(TOP_K):
        gt = jnp.where(lane == kk, ex[kk] / tot, gt)
    return _pack_rows(h), onehot, ei, gt


def _glu_out_kernel(z_ref, x_ref, g_ref, wg_ref, wo_ref, o_ref):
    z = z_ref[...]
    gate = jax.nn.sigmoid(jnp.dot(z.astype(BF16), wg_ref[...], preferred_element_type=F32))
    m = jnp.dot((z * gate).astype(BF16), wo_ref[...], preferred_element_type=F32)
    o_ref[...] = x_ref[...] + g_ref[0] * m


def _lin_res_kernel(a_ref, x_ref, g_ref, w_ref, o_ref):
    m = jnp.dot(a_ref[...].astype(BF16), w_ref[...], preferred_element_type=F32)
    o_ref[...] = x_ref[...] + g_ref[0] * m


def _row_call(kernel, acts, x2, gate, weights, seq):
    t, d = x2.shape
    tm = TM_DENSE
    tpb = seq // tm
    row = pl.BlockSpec((tm, d), lambda i: (i, 0))
    return pl.pallas_call(
        kernel,
        grid=(t // tm,),
        in_specs=[row, row, pl.BlockSpec((1, 1, d), lambda i: (i // tpb, 0, 0))]
                 + [pl.BlockSpec(w.shape, lambda i: (0, 0)) for w in weights],
        out_specs=row,
        out_shape=jax.ShapeDtypeStruct((t, d), F32),
        compiler_params=_cparams(1, 48),
    )(acts, x2, gate, *weights)


def _router_kernel(x_ref, g_ref, sc_ref, sh_ref, wr_ref, br_ref, h_ref, oh_ref, ei_ref, gt_ref):
    h_ref[...], oh_ref[...], ei_ref[...], gt_ref[...] = _route(
        x_ref[...], g_ref[...], sc_ref[0], sh_ref[0], wr_ref[...], br_ref[...])


def _router(x2, g, sc, sh, w_router, b_router, seq):
    t, d = x2.shape
    ne = w_router.shape[1]
    wr_pad = jnp.zeros((d, LANES), F32).at[:, :ne].set(w_router)
    br_pad = jnp.full((1, LANES), NEG, F32).at[0, :ne].set(b_router)
    tm = TM_DENSE
    tpb = seq // tm
    lane_tile = pl.BlockSpec((tm, LANES), lambda i: (i, 0))
    per_batch = pl.BlockSpec((1, 1, d), lambda i: (i // tpb, 0, 0))
    return pl.pallas_call(
        _router_kernel,
        grid=(t // tm,),
        in_specs=[pl.BlockSpec((tm, d), lambda i: (i, 0)),
                  pl.BlockSpec((1, d), lambda i: (0, 0)), per_batch, per_batch,
                  pl.BlockSpec((d, LANES), lambda i: (0, 0)),
                  pl.BlockSpec((1, LANES), lambda i: (0, 0))],
        out_specs=[pl.BlockSpec((tm, d // 2), lambda i: (i, 0)), lane_tile, lane_tile, lane_tile],
        out_shape=[jax.ShapeDtypeStruct((t, d // 2), jnp.uint32)]
                  + [jax.ShapeDtypeStruct((t, LANES), F32)] * 3,
        compiler_params=_cparams(1, 32),
    )(x2, g.reshape(1, d), sc, sh, wr_pad, br_pad)


def _rope_tab_kernel(pos_ref, inv_ref, sgn_ref, cos_ref, sin_ref):
    ang = pos_ref[...].astype(F32) * inv_ref[...]
    cos_ref[...] = jnp.cos(ang)
    sin_ref[...] = jnp.sin(ang) * sgn_ref[...]


def _rope_tables(pos_col):
    t = pos_col.shape[0]
    ts = 512
    half = HEAD_DIM // 2
    inv = ROPE_THETA ** (-jnp.arange(0, HEAD_DIM, 2, dtype=F32) / HEAD_DIM)
    inv = jnp.concatenate([inv, inv]).reshape(1, HEAD_DIM)
    sgn = jnp.concatenate([-jnp.ones((half,), F32), jnp.ones((half,), F32)]).reshape(1, HEAD_DIM)
    const = pl.BlockSpec((1, HEAD_DIM), lambda i: (0, 0))
    tile = pl.BlockSpec((ts, HEAD_DIM), lambda i: (i, 0))
    return pl.pallas_call(
        _rope_tab_kernel,
        grid=(t // ts,),
        in_specs=[pl.BlockSpec((ts, 1), lambda i: (i, 0)), const, const],
        out_specs=[tile, tile],
        out_shape=[jax.ShapeDtypeStruct((t, HEAD_DIM), F32)] * 2,
        compiler_params=_cparams(1, 16),
    )(pos_col, inv, sgn)


def _rope_kernel(q_ref, k_ref, v_ref, cos_ref, sin_ref, qo_ref, ko_ref, vo_ref, km_ref):
    cs = cos_ref[...]
    sn = sin_ref[...]
    dh = cs.shape[1]
    half = dh // 2
    for hh in range(qo_ref.shape[1]):
        cols = slice(hh * dh, (hh + 1) * dh)
        q = q_ref[:, cols].astype(F32)
        k = k_ref[:, cols].astype(F32)
        qr = q * cs + pltpu.roll(q, half, 1) * sn
        kr = k * cs + pltpu.roll(k, half, 1) * sn
        qo_ref[0, hh] = qr
        ko_ref[0, hh] = kr.astype(BF16)
        vo_ref[0, hh, 0] = v_ref[:, cols].astype(F32).T.astype(BF16)
        km_ref[0, hh, 0] = jnp.mean(kr, axis=0, keepdims=True)


def _rope(qkv, cos, sin, bsz, seq):
    t = qkv.shape[0]
    h, dh, blk = N_HEADS, HEAD_DIM, MOBA_BLOCK
    nb = seq // blk
    part = lambda p: pl.BlockSpec((blk, h * dh), lambda i: (i, p))
    tab = pl.BlockSpec((blk, dh), lambda i: (i, 0))
    head = pl.BlockSpec((1, h, blk, dh), lambda i: (i // nb, 0, i % nb, 0))
    return pl.pallas_call(
        _rope_kernel,
        grid=(t // blk,),
        in_specs=[part(0), part(1), part(2), tab, tab],
        out_specs=[head, head,
                   pl.BlockSpec((1, h, 1, dh, blk), lambda i: (i // nb, 0, i % nb, 0, 0)),
                   pl.BlockSpec((1, h, 1, 1, dh), lambda i: (i // nb, 0, i % nb, 0, 0))],
        out_shape=[jax.ShapeDtypeStruct((bsz, h, seq, dh), F32),
                   jax.ShapeDtypeStruct((bsz, h, seq, dh), BF16),
                   jax.ShapeDtypeStruct((bsz, h, nb, dh, blk), BF16),
                   jax.ShapeDtypeStruct((bsz, h, nb, 1, dh), F32)],
        compiler_params=_cparams(1, 32),
    )(qkv, qkv, qkv, cos, sin)


def _moba_kernel(q_ref, k_ref, vt_ref, km_ref, o_ref, sel_scr, s_a, s_b):
    nh = q_ref.shape[1]
    blk = q_ref.shape[2]
    dh = q_ref.shape[3]
    j = pl.program_id(2)
    nt = (((1,), (1,)), ((), ()))
    heads = range(nh)

    qs = []
    for hh in heads:
        q = q_ref[0, hh]
        gate = lax.dot_general(km_ref[0, hh], q, nt, preferred_element_type=F32, precision=HI)
        bidx = lax.broadcasted_iota(jnp.int32, gate.shape, 0)
        bidx_f = bidx.astype(F32)
        work = jnp.where(bidx < j, gate, NEG)
        sel = jnp.zeros(gate.shape, F32)
        for _ in range(MOBA_TOPK):
            m = jnp.max(work, axis=0, keepdims=True)
            idx = jnp.min(jnp.where(work == m, bidx_f, float(LANES)), axis=0, keepdims=True)
            hit = bidx_f == idx
            sel = jnp.where(hit & (m > 0.5 * NEG), 1.0, sel)
            work = jnp.where(hit, 2.0 * NEG, work)
        sel_scr[hh] = sel
        qs.append((q * (dh ** -0.5 * LOG2E)).astype(BF16))

    def scores(hh, n):
        start = pl.multiple_of(n * blk, blk)
        kb = k_ref[0, hh, pl.ds(start, blk), :]
        return lax.dot_general(kb, qs[hh], nt, preferred_element_type=F32)

    nb_last = vt_ref.shape[2] - 1

    def score_trip(scr, nn):
        for hh in heads:
            for bb in range(2):
                scr[hh, bb] = scores(hh, jnp.minimum(2 * nn + bb, nb_last))

    def past_pair(hh, nn, scr, m_i, l_i, acc):
        n0 = jnp.minimum(2 * nn, nb_last)
        n1 = jnp.minimum(2 * nn + 1, nb_last)
        s0 = scr[hh, 0]
        s1 = scr[hh, 1]
        r0 = (sel_scr[hh, pl.ds(n0, 1), :] > 0.5) & (2 * nn < j)
        r1 = (sel_scr[hh, pl.ds(n1, 1), :] > 0.5) & (2 * nn + 1 < j)
        rm0 = jnp.where(r0, jnp.max(s0, axis=0, keepdims=True), NEG)
        rm1 = jnp.where(r1, jnp.max(s1, axis=0, keepdims=True), NEG)
        m_new = jnp.maximum(m_i, jnp.maximum(rm0, rm1))
        alpha = jnp.exp2(m_i - m_new)
        p0 = jnp.exp2(s0 - jnp.where(r0, m_new, -NEG))
        p1 = jnp.exp2(s1 - jnp.where(r1, m_new, -NEG))
        l_new = alpha * l_i + (jnp.sum(p0, axis=0, keepdims=True) + jnp.sum(p1, axis=0, keepdims=True))
        pv = (jnp.dot(vt_ref[0, hh, n0], p0.astype(BF16), preferred_element_type=F32)
              + jnp.dot(vt_ref[0, hh, n1], p1.astype(BF16), preferred_element_type=F32))
        return m_new, l_new, alpha * acc + pv

    def body(mm, state):
        score_trip(s_b, 2 * mm + 1)
        state = tuple(past_pair(hh, 2 * mm, s_a, *state[hh]) for hh in heads)
        score_trip(s_a, 2 * mm + 2)
        return tuple(past_pair(hh, 2 * mm + 1, s_b, *state[hh]) for hh in heads)

    init = tuple((jnp.full((1, blk), NEG, F32), jnp.zeros((1, blk), F32), jnp.zeros((dh, blk), F32))
                 for _ in heads)
    score_trip(s_a, 0)
    n_trips = (j + 1) // 2
    carry = lax.fori_loop(0, (n_trips + 1) // 2, body, init)

    for hh in heads:
        m_i, l_i, acc = carry[hh]
        s = scores(hh, j)
        kk = lax.broadcasted_iota(jnp.int32, s.shape, 0)
        qq = lax.broadcasted_iota(jnp.int32, s.shape, 1)
        s = jnp.where(kk <= qq, s, NEG)
        m_new = jnp.maximum(m_i, jnp.max(s, axis=0, keepdims=True))
        alpha = jnp.exp2(m_i - m_new)
        p = jnp.exp2(s - m_new)
        l_i = alpha * l_i + jnp.sum(p, axis=0, keepdims=True)
        acc = alpha * acc + jnp.dot(vt_ref[0, hh, j], p.astype(BF16), preferred_element_type=F32)
        o_ref[:, hh * dh:(hh + 1) * dh] = (acc / l_i).T.astype(o_ref.dtype)


def _moba(q, k, vt, km_pad, bsz, seq):
    h, dh, blk = N_HEADS, HEAD_DIM, MOBA_BLOCK
    nb = seq // blk
    nbp = km_pad.shape[2]
    nh = MOBA_HEADS_PER_STEP
    return pl.pallas_call(
        _moba_kernel,
        grid=(bsz, h // nh, nb),
        in_specs=[pl.BlockSpec((1, nh, blk, dh), lambda b, j, i: (b, j, i, 0)),
                  pl.BlockSpec((1, nh, seq, dh), lambda b, j, i: (b, j, 0, 0)),
                  pl.BlockSpec((1, nh, nb, dh, blk), lambda b, j, i: (b, j, 0, 0, 0)),
                  pl.BlockSpec((1, nh, nbp, dh), lambda b, j, i: (b, j, 0, 0))],
        out_specs=pl.BlockSpec((blk, nh * dh), lambda b, j, i: (b * nb + i, j)),
        out_shape=jax.ShapeDtypeStruct((bsz * seq, h * dh), BF16),
        scratch_shapes=[pltpu.VMEM((nh, nbp, blk), F32)] + [pltpu.VMEM((nh, 2, blk, blk), F32)] * 2,
        compiler_params=_cparams(3, 48),
    )(q, k, vt, km_pad)


def _pos_kernel(oh_ref, ei_ref, dest_ref, meta_ref, cnt_scr, off_scr, run_scr):
    p = pl.program_id(0)
    i = pl.program_id(1)
    tm = oh_ref.shape[0]
    oh = oh_ref[...]
    colsum = jnp.sum(oh, axis=0, keepdims=True)

    @pl.when((p == 0) & (i == 0))
    def _():
        cnt_scr[...] = jnp.zeros_like(cnt_scr)

    @pl.when(p == 0)
    def _():
        cnt_scr[...] += colsum
        dest_ref[...] = jnp.zeros_like(dest_ref)

    @pl.when((p == 1) & (i == 0))
    def _():
        cnt = cnt_scr[...]
        padded = jnp.ceil(cnt * (1.0 / TM_EXPERT)) * float(TM_EXPERT)
        r = lax.broadcasted_iota(jnp.int32, (LANES, LANES), 0)
        cidx = lax.broadcasted_iota(jnp.int32, (LANES, LANES), 1)
        tri = (r < cidx).astype(F32)
        off = jnp.dot(jnp.broadcast_to(padded, (8, LANES)), tri,
                      preferred_element_type=F32, precision=HI)
        off_scr[...] = off[0:1]
        run_scr[...] = jnp.zeros_like(run_scr)

    @pl.when(p == 1)
    def _():
        r = lax.broadcasted_iota(jnp.int32, (tm, tm), 0)
        cidx = lax.broadcasted_iota(jnp.int32, (tm, tm), 1)
        ltri = (r > cidx).astype(BF16)
        excl = jnp.dot(ltri, oh.astype(BF16), preferred_element_type=F32)
        dfull = off_scr[...] + run_scr[...] + excl
        ei = ei_ref[...]
        lane = lax.broadcasted_iota(jnp.int32, oh.shape, 1)
        lane_f = lane.astype(F32)
        dest = jnp.zeros(oh.shape, F32)
        for kk in range(TOP_K):
            ek = jnp.sum(jnp.where(lane == kk, ei, 0.0), axis=-1, keepdims=True)
            dk = jnp.sum(jnp.where(lane_f == ek, dfull, 0.0), axis=-1, keepdims=True)
            dest = jnp.where(lane == kk, dk, dest)
        dest_ref[...] = dest.astype(jnp.int32)
        run_scr[...] += colsum

    meta_ref[...] = jnp.concatenate([off_scr[...], cnt_scr[...],
                                     jnp.zeros((6, LANES), F32)], axis=0)


def _positions(onehot, ei):
    t = onehot.shape[0]
    tm = TM_DENSE
    tile = pl.BlockSpec((tm, LANES), lambda p, i: (i, 0))
    return pl.pallas_call(
        _pos_kernel,
        grid=(2, t // tm),
        in_specs=[tile, tile],
        out_specs=[pl.BlockSpec((tm, LANES), lambda p, i: (i * p, 0)),
                   pl.BlockSpec((8, LANES), lambda p, i: (0, 0))],
        out_shape=[jax.ShapeDtypeStruct((t, LANES), jnp.int32),
                   jax.ShapeDtypeStruct((8, LANES), F32)],
        scratch_shapes=[pltpu.VMEM((1, LANES), F32)] * 3,
        compiler_params=_cparams(2, 16),
    )(onehot, ei)


def _src_kernel(dest_ref, src_ref):
    i = pl.program_id(0)
    tb = dest_ref.shape[0] // TOP_K

    @pl.when(i == 0)
    def _():
        def zero(j, carry):
            src_ref[j] = 0
            return carry
        lax.fori_loop(0, src_ref.shape[0], zero, 0, unroll=16)

    def body(tt, carry):
        for kk in range(TOP_K):
            src_ref[dest_ref[tt * TOP_K + kk]] = i * tb + tt
        return carry

    lax.fori_loop(0, tb, body, 0, unroll=4)


def _sources(dest_flat, n_rows):
    n = dest_flat.shape[0]
    tb = TM_SRC
    return pl.pallas_call(
        _src_kernel,
        grid=(n // (tb * TOP_K),),
        in_specs=[pl.BlockSpec((tb * TOP_K,), lambda i: (i,), memory_space=pltpu.SMEM)],
        out_specs=pl.BlockSpec(memory_space=pltpu.SMEM),
        out_shape=jax.ShapeDtypeStruct((n_rows,), jnp.int32),
        compiler_params=_cparams(1, 16),
    )(dest_flat)


def _expert_kernel(te_ref, tv_ref, nx_ref, src_cur, src_nxt, h_hbm, wgu_hbm, bgu_ref, wd_hbm, bd_ref,
                   o_ref, xbuf, sem, wgu_st, wd_st, wsem, wgu_bf, wd_bf):
    i = pl.program_id(0)
    nt = pl.num_programs(0)
    tm = o_ref.shape[0]
    slot = lax.rem(i, 2)
    e = te_ref[i]
    prev = te_ref[jnp.maximum(i - 1, 0)]
    dff = wd_hbm.shape[1]

    def weight_copies(ex):
        return (pltpu.make_async_copy(wgu_hbm.at[ex], wgu_st, wsem.at[0]),
                pltpu.make_async_copy(wd_hbm.at[ex], wd_st, wsem.at[1]))

    def gather(src_ref, s):
        def body(r8, carry):
            for uu in range(8):
                pltpu.make_async_copy(h_hbm.at[pl.ds(src_ref[r8 * 8 + uu], 1)],
                                      xbuf.at[s, r8, pl.ds(uu, 1)], sem.at[s]).start()
            return carry
        lax.fori_loop(0, tm // 8, body, 0)

    @pl.when(i == 0)
    def _():
        for cp in weight_copies(e):
            cp.start(priority=1)

    @pl.when((i == 0) & (tv_ref[0] > 0))
    def _():
        gather(src_cur, 0)

    nxt = jnp.minimum(i + 1, nt - 1)

    @pl.when((i + 1 < nt) & (tv_ref[nxt] > 0))
    def _():
        gather(src_nxt, 1 - slot)

    @pl.when((i == 0) | (e != prev))
    def _():
        for cp in weight_copies(e):
            cp.wait()
        wgu_bf[...] = wgu_st[...].astype(BF16)
        wd_bf[...] = wd_st[...].astype(BF16)

        @pl.when(nx_ref[i] >= 0)
        def _():
            for cp in weight_copies(nx_ref[i]):
                cp.start(priority=1)

    @pl.when(tv_ref[i] > 0)
    def _():
        for r8 in range(tm // 8):
            pltpu.make_async_copy(h_hbm.at[pl.ds(0, 8)], xbuf.at[slot, r8], sem.at[slot]).wait()
        x = _unpack_rows(xbuf[slot].reshape(tm, xbuf.shape[3])).astype(BF16)
        hu = jnp.dot(x, wgu_bf[...], preferred_element_type=F32) + bgu_ref[0]
        g = jnp.minimum(hu[:, :dff], SWIGLU_LIMIT)
        up = jnp.clip(hu[:, dff:], -SWIGLU_LIMIT, SWIGLU_LIMIT)
        act = (up + 1.0) * g * jax.nn.sigmoid(SWIGLU_ALPHA * g)
        y = jnp.dot(act.astype(BF16), wd_bf[...], preferred_element_type=F32) + bd_ref[0]
        o_ref[...] = _pack_rows(y)

    @pl.when(tv_ref[i] == 0)
    def _():
        o_ref[...] = jnp.zeros_like(o_ref)


def _experts(tile_e, tile_v, next_e, src, h2p, w_gu, b_gu, w_d, b_d):
    n_rows = src.shape[0]
    dp = h2p.shape[1]
    ne, d, f2 = w_gu.shape
    dff = w_d.shape[1]
    tm = TM_EXPERT
    nt = n_rows // tm
    grid_spec = pltpu.PrefetchScalarGridSpec(
        num_scalar_prefetch=3,
        grid=(nt,),
        in_specs=[pl.BlockSpec((tm,), lambda i, te, tv, nx: (i,), memory_space=pltpu.SMEM),
                  pl.BlockSpec((tm,), lambda i, te, tv, nx: (jnp.minimum(i + 1, nt - 1),),
                               memory_space=pltpu.SMEM),
                  pl.BlockSpec(memory_space=pl.ANY),
                  pl.BlockSpec(memory_space=pl.ANY),
                  pl.BlockSpec((1, 1, f2), lambda i, te, tv, nx: (te[i], 0, 0)),
                  pl.BlockSpec(memory_space=pl.ANY),
                  pl.BlockSpec((1, 1, d), lambda i, te, tv, nx: (te[i], 0, 0))],
        out_specs=pl.BlockSpec((tm, dp), lambda i, te, tv, nx: (i, 0)),
        scratch_shapes=[pltpu.VMEM((2, tm // 8, 8, dp), jnp.uint32), pltpu.SemaphoreType.DMA((2,)),
                        pltpu.VMEM((d, f2), F32), pltpu.VMEM((dff, d), F32),
                        pltpu.SemaphoreType.DMA((2,)),
                        pltpu.VMEM((d, f2), BF16), pltpu.VMEM((dff, d), BF16)],
    )
    return pl.pallas_call(
        _expert_kernel,
        grid_spec=grid_spec,
        out_shape=jax.ShapeDtypeStruct((n_rows, dp), jnp.uint32),
        compiler_params=_cparams(1, 48),
    )(tile_e, tile_v, next_e, src, src, h2p, w_gu, b_gu.reshape(ne, 1, f2), w_d, b_d.reshape(ne, 1, d))


def _combine_kernel(dcur, dnxt, gt_ref, x_ref, g_ref, gf_ref, ys_hbm, o_ref, buf, sem, *, final):
    i = pl.program_id(0)
    nt = pl.num_programs(0)
    td = x_ref.shape[0]
    slot = lax.rem(i, 2)

    def gather(dref, s):
        def body(t8, carry):
            for uu in range(8):
                for kk in range(TOP_K):
                    pltpu.make_async_copy(ys_hbm.at[pl.ds(dref[(t8 * 8 + uu) * TOP_K + kk], 1)],
                                          buf.at[s, kk, t8, pl.ds(uu, 1)], sem.at[s]).start()
            return carry
        lax.fori_loop(0, td // 8, body, 0)

    @pl.when(i == 0)
    def _():
        gather(dcur, 0)

    @pl.when(i + 1 < nt)
    def _():
        gather(dnxt, 1 - slot)

    for kk in range(TOP_K):
        for t8 in range(td // 8):
            pltpu.make_async_copy(ys_hbm.at[pl.ds(0, 8)], buf.at[slot, kk, t8], sem.at[slot]).wait()
    gt = gt_ref[...]
    rows = lambda kk: _unpack_rows(buf[slot, kk].reshape(td, buf.shape[4]))
    y = gt[:, 0:1] * rows(0)
    for kk in range(1, TOP_K):
        y = y + gt[:, kk:kk + 1] * rows(kk)
    xn = x_ref[...] + g_ref[0] * y
    if final:
        ms = jnp.mean(xn * xn, axis=-1, keepdims=True)
        xn = (xn * lax.rsqrt(ms + EPS)) * gf_ref[...]
    o_ref[...] = xn


def _combine(dest_flat, gates, x2, gate_vec, gf, ys, seq, final):
    t, d = x2.shape
    td = TD_ROWS
    tpb = seq // td
    nt = t // td
    return pl.pallas_call(
        functools.partial(_combine_kernel, final=final),
        grid=(nt,),
        in_specs=[pl.BlockSpec((td * TOP_K,), lambda i: (i,), memory_space=pltpu.SMEM),
                  pl.BlockSpec((td * TOP_K,), lambda i: (jnp.minimum(i + 1, nt - 1),),
                               memory_space=pltpu.SMEM),
                  pl.BlockSpec((td, LANES), lambda i: (i, 0)),
                  pl.BlockSpec((td, d), lambda i: (i, 0)),
                  pl.BlockSpec((1, 1, d), lambda i: (i // tpb, 0, 0)),
                  pl.BlockSpec((1, d), lambda i: (0, 0)),
                  pl.BlockSpec(memory_space=pl.ANY)],
        out_specs=pl.BlockSpec((td, d), lambda i: (i, 0)),
        out_shape=jax.ShapeDtypeStruct((t, d), F32),
        scratch_shapes=[pltpu.VMEM((2, TOP_K, td // 8, 8, d // 2), jnp.uint32),
                        pltpu.SemaphoreType.DMA((2,))],
        compiler_params=_cparams(1, 32),
    )(dest_flat, dest_flat, gates, x2, gate_vec, gf.reshape(1, d), ys)


def _moe(x2, routed, gate_vec, w_gu, b_gu, w_d, b_d, gf, seq, final):
    t, d = x2.shape
    ne = w_gu.shape[0]
    h2, onehot, ei, gates = routed
    dest, meta = _positions(onehot, ei)
    dest_flat = dest[:, :TOP_K].reshape(t * TOP_K)

    tm = TM_EXPERT
    n_rows = t * TOP_K + ne * tm
    n_tiles = n_rows // tm
    off = meta[0, :ne].astype(jnp.int32)
    cnt = meta[1, :ne].astype(jnp.int32)
    ends = off + ((cnt + tm - 1) // tm) * tm
    starts = jnp.arange(n_tiles, dtype=jnp.int32) * tm
    valid = starts < ends[-1]
    te = jnp.sum((starts[:, None] >= ends[None, :]).astype(jnp.int32), axis=1)
    te_last = jnp.sum((ends[-1] - 1 >= ends).astype(jnp.int32))
    tile_e = jnp.minimum(jnp.where(valid, te, te_last), ne - 1)
    tile_v = valid.astype(jnp.int32)
    eidx = jnp.arange(ne, dtype=jnp.int32)
    later_used = (eidx[None, :] > eidx[:, None]) & (cnt[None, :] > 0)
    next_of = jnp.min(jnp.where(later_used, eidx[None, :], ne), axis=1)
    next_of = jnp.where(next_of == ne, -1, next_of)
    next_e = jnp.sum(jnp.where(tile_e[:, None] == eidx[None, :], next_of[None, :], 0), axis=1)

    src = _sources(dest_flat, n_rows)
    ys = _experts(tile_e, tile_v, next_e.astype(jnp.int32), src, h2, w_gu, b_gu, w_d, b_d)
    return _combine(dest_flat, gates, x2, gate_vec, gf, ys, seq, final)


def kernel(x, c, positions, l0_norm1_g, l0_ada_w, l0_ada_b, l0_s5_w_in, l0_s5_b_re, l0_s5_b_im, l0_s5_c_re, l0_s5_c_im, l0_s5_lam_re, l0_s5_lam_im, l0_s5_log_dt, l0_s5_d, l0_s5_w_glu, l0_s5_w_out, l0_norm2_g, l0_moe_w_router, l0_moe_b_router, l0_moe_w_gate_up, l0_moe_b_gate_up, l0_moe_w_down, l0_moe_b_down, l1_norm1_g, l1_ada_w, l1_ada_b, l1_moba_w_qkv, l1_moba_w_o, l1_norm2_g, l1_moe_w_router, l1_moe_b_router, l1_moe_w_gate_up, l1_moe_b_gate_up, l1_moe_w_down, l1_moe_b_down, final_norm_g):
    bsz, seq, d = x.shape
    t = bsz * seq
    assert seq % TM_DENSE == 0 and seq % MOBA_BLOCK == 0 and seq % S5_CHUNK == 0
    assert seq // MOBA_BLOCK <= LANES and d == N_HEADS * HEAD_DIM
    x2 = x.reshape(t, d)
    c_pad = jnp.zeros((8, d), F32).at[:bsz].set(c)

    def ada_parts(w, b):
        a = _ada(c_pad, w, b)[:bsz]
        return [a[:, i * d:(i + 1) * d].reshape(bsz, 1, d) for i in range(6)]

    sh1, sc1, g1, sh2, sc2, g2 = ada_parts(l0_ada_w, l0_ada_b)
    u = _nml(x2, l0_norm1_g, sc1, sh1, l0_s5_w_in.astype(BF16), seq, F32)
    rate, theta, bblk, cblk = _s5_params(l0_s5_b_re, l0_s5_b_im, l0_s5_c_re, l0_s5_c_im,
                                         l0_s5_lam_re, l0_s5_lam_im, l0_s5_log_dt)
    z = _s5_scan(u, rate, theta, bblk, cblk, l0_s5_d, bsz, seq)
    x2 = _row_call(_glu_out_kernel, z, x2, g1,
                   [l0_s5_w_glu.astype(BF16), l0_s5_w_out.astype(BF16)], seq)
    routed = _router(x2, l0_norm2_g, sc2, sh2, l0_moe_w_router, l0_moe_b_router, seq)
    x2 = _moe(x2, routed, g2, l0_moe_w_gate_up, l0_moe_b_gate_up, l0_moe_w_down, l0_moe_b_down,
              final_norm_g, seq, final=False)

    sh1, sc1, g1, sh2, sc2, g2 = ada_parts(l1_ada_w, l1_ada_b)
    qkv = _nml(x2, l1_norm1_g, sc1, sh1, l1_moba_w_qkv.astype(BF16), seq, BF16)
    cos, sin = _rope_tables(positions.reshape(t, 1))
    q, k, v, km = _rope(qkv, cos, sin, bsz, seq)
    nb = seq // MOBA_BLOCK
    nbp = -(-nb // 8) * 8
    km_pad = jnp.zeros((bsz, N_HEADS, nbp, HEAD_DIM), F32).at[:, :, :nb].set(km[:, :, :, 0])
    o = _moba(q, k, v, km_pad, bsz, seq)
    x2 = _row_call(_lin_res_kernel, o, x2, g1, [l1_moba_w_o.astype(BF16)], seq)
    routed = _router(x2, l1_norm2_g, sc2, sh2, l1_moe_w_router, l1_moe_b_router, seq)
    x2 = _moe(x2, routed, g2, l1_moe_w_gate_up, l1_moe_b_gate_up, l1_moe_w_down, l1_moe_b_down,
              final_norm_g, seq, final=True)
    return x2.reshape(bsz, seq, d)
```

```python
import functools
import math

import jax
import jax.numpy as jnp
from jax import lax
from jax.experimental import pallas as pl
from jax.experimental.pallas import tpu as pltpu

F32 = jnp.float32
BF16 = jnp.bfloat16
HI = lax.Precision.HIGHEST

EPS = 1e-6
NEG = -1e30
LOG2E = math.log2(math.e)
LANES = 128
MIB = 1024 * 1024

S5_GROUP = 16
S5_STATE = 64
N_HEADS = 8
HEAD_DIM = 128
MOBA_BLOCK = 256
MOBA_TOPK = 3
ROPE_THETA = 10000.0
N_EXPERTS = 32
TOP_K = 4
SWIGLU_LIMIT = 7.0
SWIGLU_ALPHA = 1.702

TM_DENSE = 512
S5_CHUNK = 256
S5_SUPER = 8
MOBA_HEADS_PER_STEP = 2
TM_EXPERT = 256
TD_ROWS = 256
TM_SRC = 2048


def _cparams(n_axes, vmem_mib):
    return pltpu.CompilerParams(dimension_semantics=("arbitrary",) * n_axes,
                                vmem_limit_bytes=vmem_mib * MIB)


def _pack_rows(x):
    n = x.shape[1] // 2
    xb = x.astype(BF16).astype(F32)
    lo = lax.bitcast_convert_type(xb[:, :n], jnp.uint32) >> 16
    hi = lax.bitcast_convert_type(xb[:, n:], jnp.uint32) & jnp.uint32(0xFFFF0000)
    return lo | hi


def _unpack_rows(p):
    lo = lax.bitcast_convert_type(p << 16, F32)
    hi = lax.bitcast_convert_type(p & jnp.uint32(0xFFFF0000), F32)
    return jnp.concatenate([lo, hi], axis=1)


def _normmod(x, g, sc, sh):
    ms = jnp.mean(x * x, axis=-1, keepdims=True)
    return (x * lax.rsqrt(ms + EPS)) * g * (1.0 + sc) + sh


def _ada_kernel(c_ref, w_ref, b_ref, o_ref):
    c = c_ref[...]
    ca = c * jax.nn.sigmoid(c)
    o_ref[...] = jnp.dot(ca, w_ref[...], preferred_element_type=F32, precision=HI) + b_ref[...]


def _ada(c_pad, w, b):
    d, n = w.shape
    tn = 1536
    return pl.pallas_call(
        _ada_kernel,
        grid=(n // tn,),
        in_specs=[pl.BlockSpec((8, d), lambda j: (0, 0)),
                  pl.BlockSpec((d, tn), lambda j: (0, j)),
                  pl.BlockSpec((1, tn), lambda j: (0, j))],
        out_specs=pl.BlockSpec((8, tn), lambda j: (0, j)),
        out_shape=jax.ShapeDtypeStruct((8, n), F32),
        compiler_params=_cparams(1, 32),
    )(c_pad, w, b.reshape(1, n))


def _nml_kernel(x_ref, g_ref, sc_ref, sh_ref, w_ref, o_ref):
    h = _normmod(x_ref[...], g_ref[...], sc_ref[0], sh_ref[0])
    o_ref[...] = jnp.dot(h.astype(BF16), w_ref[...], preferred_element_type=F32).astype(o_ref.dtype)


def _nml(x2, g, sc, sh, w_bf, seq, out_dtype):
    t, d = x2.shape
    n = w_bf.shape[1]
    tm = TM_DENSE
    tpb = seq // tm
    return pl.pallas_call(
        _nml_kernel,
        grid=(t // tm,),
        in_specs=[pl.BlockSpec((tm, d), lambda i: (i, 0)),
                  pl.BlockSpec((1, d), lambda i: (0, 0)),
                  pl.BlockSpec((1, 1, d), lambda i: (i // tpb, 0, 0)),
                  pl.BlockSpec((1, 1, d), lambda i: (i // tpb, 0, 0)),
                  pl.BlockSpec((d, n), lambda i: (0, 0))],
        out_specs=pl.BlockSpec((tm, n), lambda i: (i, 0)),
        out_shape=jax.ShapeDtypeStruct((t, n), out_dtype),
        compiler_params=_cparams(1, 48),
    )(x2, g.reshape(1, d), sc, sh, w_bf)


def _gelu_tanh(y):
    c = math.sqrt(2.0 / math.pi)
    return y * (0.5 * (1.0 + jnp.tanh(c * (y + 0.044715 * (y * y * y)))))


def _s5_kernel(u_ref, rate_ref, theta_ref, bblk_ref, cblk_ref, d_ref, z_ref,
               apr, api, air, aii, xr, xi):
    L = u_ref.shape[0]
    n_super = bblk_ref.shape[0]
    half = apr.shape[1] // n_super
    nch = u_ref.shape[1] // n_super
    b = pl.program_id(0)
    c = pl.program_id(1)

    @pl.when((b == 0) & (c == 0))
    def _():
        tt = lax.broadcasted_iota(jnp.int32, (L, 1), 0).astype(F32)
        ph = theta_ref[...] * tt
        rt = rate_ref[...] * tt
        cs = jnp.cos(ph)
        sn = jnp.sin(ph)
        mag = jnp.exp(rt)
        inv = jnp.exp(-rt)
        apr[...] = mag * cs
        api[...] = mag * sn
        air[...] = inv * cs
        aii[...] = -(inv * sn)

    @pl.when(c == 0)
    def _():
        xr[...] = jnp.zeros_like(xr)
        xi[...] = jnp.zeros_like(xi)

    row = lax.broadcasted_iota(jnp.int32, (L, L), 0)
    col = lax.broadcasted_iota(jnp.int32, (L, L), 1)
    tril = (row >= col).astype(BF16)

    for sg in range(n_super):
        cols = slice(sg * half, (sg + 1) * half)
        ch = slice(sg * nch, (sg + 1) * nch)
        u = u_ref[:, ch]
        bu = jnp.dot(u.astype(BF16), bblk_ref[sg], preferred_element_type=F32)
        bur = bu[:, :half]
        bui = bu[:, half:]
        ir = air[:, cols]
        ii = aii[:, cols]
        zc = jnp.concatenate([bur * ir - bui * ii, bur * ii + bui * ir], axis=1)
        cum = jnp.dot(tril, zc.astype(BF16), preferred_element_type=F32)
        a_r = apr[1:2, cols]
        a_i = api[1:2, cols]
        p_r = xr[:, cols]
        p_i = xi[:, cols]
        cr = cum[:, :half] + (a_r * p_r - a_i * p_i)
        ci = cum[:, half:] + (a_r * p_i + a_i * p_r)
        pw_r = apr[:, cols]
        pw_i = api[:, cols]
        x_re = cr * pw_r - ci * pw_i
        x_im = cr * pw_i + ci * pw_r
        xr[:, cols] = x_re[L - 1:L, :]
        xi[:, cols] = x_im[L - 1:L, :]
        xc = jnp.concatenate([x_re, x_im], axis=1).astype(BF16)
        y = jnp.dot(xc, cblk_ref[sg], preferred_element_type=F32) + d_ref[:, ch] * u
        z_ref[:, ch] = _gelu_tanh(y)


def _s5_scan(u, rate, theta, bblk, cblk, d_skip, bsz, seq):
    t, w = u.shape
    L = S5_CHUNK
    nc = seq // L
    ns = rate.shape[1]
    return pl.pallas_call(
        _s5_kernel,
        grid=(bsz, nc),
        in_specs=[pl.BlockSpec((L, w), lambda b, c: (b * nc + c, 0)),
                  pl.BlockSpec((1, ns), lambda b, c: (0, 0)),
                  pl.BlockSpec((1, ns), lambda b, c: (0, 0)),
                  pl.BlockSpec(bblk.shape, lambda b, c: (0, 0, 0)),
                  pl.BlockSpec(cblk.shape, lambda b, c: (0, 0, 0)),
                  pl.BlockSpec((1, w), lambda b, c: (0, 0))],
        out_specs=pl.BlockSpec((L, w), lambda b, c: (b * nc + c, 0)),
        out_shape=jax.ShapeDtypeStruct((t, w), F32),
        scratch_shapes=[pltpu.VMEM((L, ns), F32)] * 4 + [pltpu.VMEM((1, ns), F32)] * 2,
        compiler_params=_cparams(2, 58),
    )(u, rate, theta, bblk, cblk, d_skip.reshape(1, w))


def _s5_params(b_re, b_im, c_re, c_im, lam_re, lam_im, log_dt):
    g, p, ch = b_re.shape
    dt = jnp.exp(log_dt)[:, None]
    rate = lam_re * dt
    theta = lam_im * dt
    mag = jnp.exp(rate)
    a_re = mag * jnp.cos(theta)
    a_im = mag * jnp.sin(theta)
    den = lam_re * lam_re + lam_im * lam_im
    f_re = ((a_re - 1.0) * lam_re + a_im * lam_im) / den
    f_im = (a_im * lam_re - (a_re - 1.0) * lam_im) / den
    bb_re = f_re[..., None] * b_re - f_im[..., None] * b_im
    bb_im = f_re[..., None] * b_im + f_im[..., None] * b_re
    ns = S5_SUPER
    nsg = g // ns
    eye = jnp.eye(ns, dtype=F32)

    def blk_b(bb):
        return jnp.einsum('sgpc,gh->sgchp', bb.reshape(nsg, ns, p, ch), eye).reshape(nsg, ns * ch, ns * p)

    def blk_c(cc):
        return jnp.einsum('sgcp,gh->sgphc', cc.reshape(nsg, ns, ch, p), eye).reshape(nsg, ns * p, ns * ch)

    bblk = jnp.concatenate([blk_b(bb_re), blk_b(bb_im)], axis=2).astype(BF16)
    cblk = jnp.concatenate([blk_c(c_re), -blk_c(c_im)], axis=1).astype(BF16)
    return rate.reshape(1, g * p), theta.reshape(1, g * p), bblk, cblk


def _route(x, g, sc, sh, wr, br):
    h = _normmod(x, g, sc, sh)
    logits = jnp.dot(h, wr, preferred_element_type=F32, precision=HI) + br
    lane = lax.broadcasted_iota(jnp.int32, logits.shape, 1)
    lane_f = lane.astype(F32)
    work = logits
    onehot = jnp.zeros(logits.shape, F32)
    ei = jnp.zeros(logits.shape, F32)
    vals = []
    for kk in range(TOP_K):
        m = jnp.max(work, axis=-1, keepdims=True)
        idx = jnp.min(jnp.where(work == m, lane_f, float(LANES)), axis=-1, keepdims=True)
        hit = lane_f == idx
        onehot = jnp.where(hit, 1.0, onehot)
        ei = jnp.where(lane == kk, idx, ei)
        vals.append(m)
        work = jnp.where(hit, 2.0 * NEG, work)
    ex = [jnp.exp(vv - vals[0]) for vv in vals]
    tot = ex[0] + ex[1] + ex[2] + ex[3]
    gt = jnp.zeros(logits.shape, F32)
    for kk in range(TOP_K):
        gt = jnp.where(lane == kk, ex[kk] / tot, gt)
    return _pack_rows(h), onehot, ei, gt


def _glu_out_kernel(z_ref, x_ref, g_ref, wg_ref, wo_ref, o_ref):
    z = z_ref[...]
    gate = jax.nn.sigmoid(jnp.dot(z.astype(BF16), wg_ref[...], preferred_element_type=F32))
    m = jnp.dot((z * gate).astype(BF16), wo_ref[...], preferred_element_type=F32)
    o_ref[...] = x_ref[...] + g_ref[0] * m


def _lin_res_kernel(a_ref, x_ref, g_ref, w_ref, o_ref):
    m = jnp.dot(a_ref[...].astype(BF16), w_ref[...], preferred_element_type=F32)
    o_ref[...] = x_ref[...] + g_ref[0] * m


def _row_call(kernel, acts, x2, gate, weights, seq):
    t, d = x2.shape
    tm = TM_DENSE
    tpb = seq // tm
    row = pl.BlockSpec((tm, d), lambda i: (i, 0))
    return pl.pallas_call(
        kernel,
        grid=(t // tm,),
        in_specs=[row, row, pl.BlockSpec((1, 1, d), lambda i: (i // tpb, 0, 0))]
                 + [pl.BlockSpec(w.shape, lambda i: (0, 0)) for w in weights],
        out_specs=row,
        out_shape=jax.ShapeDtypeStruct((t, d), F32),
        compiler_params=_cparams(1, 48),
    )(acts, x2, gate, *weights)


def _router_kernel(x_ref, g_ref, sc_ref, sh_ref, wr_ref, br_ref, h_ref, oh_ref, ei_ref, gt_ref):
    h_ref[...], oh_ref[...], ei_ref[...], gt_ref[...] = _route(
        x_ref[...], g_ref[...], sc_ref[0], sh_ref[0], wr_ref[...], br_ref[...])


def _router(x2, g, sc, sh, w_router, b_router, seq):
    t, d = x2.shape
    ne = w_router.shape[1]
    wr_pad = jnp.zeros((d, LANES), F32).at[:, :ne].set(w_router)
    br_pad = jnp.full((1, LANES), NEG, F32).at[0, :ne].set(b_router)
    tm = TM_DENSE
    tpb = seq // tm
    lane_tile = pl.BlockSpec((tm, LANES), lambda i: (i, 0))
    per_batch = pl.BlockSpec((1, 1, d), lambda i: (i // tpb, 0, 0))
    return pl.pallas_call(
        _router_kernel,
        grid=(t // tm,),
        in_specs=[pl.BlockSpec((tm, d), lambda i: (i, 0)),
                  pl.BlockSpec((1, d), lambda i: (0, 0)), per_batch, per_batch,
                  pl.BlockSpec((d, LANES), lambda i: (0, 0)),
                  pl.BlockSpec((1, LANES), lambda i: (0, 0))],
        out_specs=[pl.BlockSpec((tm, d // 2), lambda i: (i, 0)), lane_tile, lane_tile, lane_tile],
        out_shape=[jax.ShapeDtypeStruct((t, d // 2), jnp.uint32)]
                  + [jax.ShapeDtypeStruct((t, LANES), F32)] * 3,
        compiler_params=_cparams(1, 32),
    )(x2, g.reshape(1, d), sc, sh, wr_pad, br_pad)


def _rope_tab_kernel(pos_ref, inv_ref, sgn_ref, cos_ref, sin_ref):
    ang = pos_ref[...].astype(F32) * inv_ref[...]
    cos_ref[...] = jnp.cos(ang)
    sin_ref[...] = jnp.sin(ang) * sgn_ref[...]


def _rope_tables(pos_col):
    t = pos_col.shape[0]
    ts = 512
    half = HEAD_DIM // 2
    inv = ROPE_THETA ** (-jnp.arange(0, HEAD_DIM, 2, dtype=F32) / HEAD_DIM)
    inv = jnp.concatenate([inv, inv]).reshape(1, HEAD_DIM)
    sgn = jnp.concatenate([-jnp.ones((half,), F32), jnp.ones((half,), F32)]).reshape(1, HEAD_DIM)
    const = pl.BlockSpec((1, HEAD_DIM), lambda i: (0, 0))
    tile = pl.BlockSpec((ts, HEAD_DIM), lambda i: (i, 0))
    return pl.pallas_call(
        _rope_tab_kernel,
        grid=(t // ts,),
        in_specs=[pl.BlockSpec((ts, 1), lambda i: (i, 0)), const, const],
        out_specs=[tile, tile],
        out_shape=[jax.ShapeDtypeStruct((t, HEAD_DIM), F32)] * 2,
        compiler_params=_cparams(1, 16),
    )(pos_col, inv, sgn)


def _rope_kernel(q_ref, k_ref, v_ref, cos_ref, sin_ref, qo_ref, ko_ref, vo_ref, km_ref):
    cs = cos_ref[...]
    sn = sin_ref[...]
    dh = cs.shape[1]
    half = dh // 2
    for hh in range(qo_ref.shape[1]):
        cols = slice(hh * dh, (hh + 1) * dh)
        q = q_ref[:, cols].astype(F32)
        k = k_ref[:, cols].astype(F32)
        qr = q * cs + pltpu.roll(q, half, 1) * sn
        kr = k * cs + pltpu.roll(k, half, 1) * sn
        qo_ref[0, hh] = qr
        ko_ref[0, hh] = kr.astype(BF16)
        vo_ref[0, hh, 0] = v_ref[:, cols].astype(F32).T.astype(BF16)
        km_ref[0, hh, 0] = jnp.mean(kr, axis=0, keepdims=True)


def _rope(qkv, cos, sin, bsz, seq):
    t = qkv.shape[0]
    h, dh, blk = N_HEADS, HEAD_DIM, MOBA_BLOCK
    nb = seq // blk
    part = lambda p: pl.BlockSpec((blk, h * dh), lambda i: (i, p))
    tab = pl.BlockSpec((blk, dh), lambda i: (i, 0))
    head = pl.BlockSpec((1, h, blk, dh), lambda i: (i // nb, 0, i % nb, 0))
    return pl.pallas_call(
        _rope_kernel,
        grid=(t // blk,),
        in_specs=[part(0), part(1), part(2), tab, tab],
        out_specs=[head, head,
                   pl.BlockSpec((1, h, 1, dh, blk), lambda i: (i // nb, 0, i % nb, 0, 0)),
                   pl.BlockSpec((1, h, 1, 1, dh), lambda i: (i // nb, 0, i % nb, 0, 0))],
        out_shape=[jax.ShapeDtypeStruct((bsz, h, seq, dh), F32),
                   jax.ShapeDtypeStruct((bsz, h, seq, dh), BF16),
                   jax.ShapeDtypeStruct((bsz, h, nb, dh, blk), BF16),
                   jax.ShapeDtypeStruct((bsz, h, nb, 1, dh), F32)],
        compiler_params=_cparams(1, 32),
    )(qkv, qkv, qkv, cos, sin)


def _moba_kernel(q_ref, k_ref, vt_ref, km_ref, o_ref, sel_scr, s_a, s_b, s_own):
    nh = q_ref.shape[1]
    blk = q_ref.shape[2]
    dh = q_ref.shape[3]
    j = pl.program_id(2)
    nt = (((1,), (1,)), ((), ()))
    heads = range(nh)

    qs = []
    for hh in heads:
        q = q_ref[0, hh]
        gate = lax.dot_general(km_ref[0, hh], q, nt, preferred_element_type=F32, precision=HI)
        bidx = lax.broadcasted_iota(jnp.int32, gate.shape, 0)
        bidx_f = bidx.astype(F32)
        work = jnp.where(bidx < j, gate, NEG)
        sel = jnp.zeros(gate.shape, F32)
        for _ in range(MOBA_TOPK):
            m = jnp.max(work, axis=0, keepdims=True)
            idx = jnp.min(jnp.where(work == m, bidx_f, float(LANES)), axis=0, keepdims=True)
            hit = bidx_f == idx
            sel = jnp.where(hit & (m > 0.5 * NEG), 1.0, sel)
            work = jnp.where(hit, 2.0 * NEG, work)
        sel_scr[hh] = sel
        qs.append((q * (dh ** -0.5 * LOG2E)).astype(BF16))

    def scores(hh, n):
        start = pl.multiple_of(n * blk, blk)
        kb = k_ref[0, hh, pl.ds(start, blk), :]
        return lax.dot_general(kb, qs[hh], nt, preferred_element_type=F32)

    nb_last = vt_ref.shape[2] - 1

    def score_trip(scr, nn):
        for hh in heads:
            for bb in range(2):
                scr[hh, bb] = scores(hh, jnp.minimum(2 * nn + bb, nb_last))

    def past_pair(hh, nn, scr, m_i, l_i, acc):
        n0 = jnp.minimum(2 * nn, nb_last)
        n1 = jnp.minimum(2 * nn + 1, nb_last)
        s0 = scr[hh, 0]
        s1 = scr[hh, 1]
        r0 = (sel_scr[hh, pl.ds(n0, 1), :] > 0.5) & (2 * nn < j)
        r1 = (sel_scr[hh, pl.ds(n1, 1), :] > 0.5) & (2 * nn + 1 < j)
        rm0 = jnp.where(r0, jnp.max(s0, axis=0, keepdims=True), NEG)
        rm1 = jnp.where(r1, jnp.max(s1, axis=0, keepdims=True), NEG)
        m_new = jnp.maximum(m_i, jnp.maximum(rm0, rm1))
        alpha = jnp.exp2(m_i - m_new)
        p0 = jnp.exp2(s0 - jnp.where(r0, m_new, -NEG))
        p1 = jnp.exp2(s1 - jnp.where(r1, m_new, -NEG))
        l_new = alpha * l_i + (jnp.sum(p0, axis=0, keepdims=True) + jnp.sum(p1, axis=0, keepdims=True))
        pv = (jnp.dot(vt_ref[0, hh, n0], p0.astype(BF16), preferred_element_type=F32)
              + jnp.dot(vt_ref[0, hh, n1], p1.astype(BF16), preferred_element_type=F32))
        return m_new, l_new, alpha * acc + pv

    def body(mm, state):
        score_trip(s_b, 2 * mm + 1)
        state = tuple(past_pair(hh, 2 * mm, s_a, *state[hh]) for hh in heads)
        score_trip(s_a, 2 * mm + 2)
        return tuple(past_pair(hh, 2 * mm + 1, s_b, *state[hh]) for hh in heads)

    init = tuple((jnp.full((1, blk), NEG, F32), jnp.zeros((1, blk), F32), jnp.zeros((dh, blk), F32))
                 for _ in heads)
    score_trip(s_a, 0)
    for hh in heads:
        s_own[hh] = scores(hh, j)
    n_trips = (j + 1) // 2
    carry = lax.fori_loop(0, (n_trips + 1) // 2, body, init)

    for hh in heads:
        m_i, l_i, acc = carry[hh]
        s = s_own[hh]
        kk = lax.broadcasted_iota(jnp.int32, s.shape, 0)
        qq = lax.broadcasted_iota(jnp.int32, s.shape, 1)
        s = jnp.where(kk <= qq, s, NEG)
        m_new = jnp.maximum(m_i, jnp.max(s, axis=0, keepdims=True))
        alpha = jnp.exp2(m_i - m_new)
        p = jnp.exp2(s - m_new)
        l_i = alpha * l_i + jnp.sum(p, axis=0, keepdims=True)
        acc = alpha * acc + jnp.dot(vt_ref[0, hh, j], p.astype(BF16), preferred_element_type=F32)
        o_ref[:, hh * dh:(hh + 1) * dh] = (acc / l_i).T.astype(o_ref.dtype)


def _moba(q, k, vt, km_pad, bsz, seq):
    h, dh, blk = N_HEADS, HEAD_DIM, MOBA_BLOCK
    nb = seq // blk
    nbp = km_pad.shape[2]
    nh = MOBA_HEADS_PER_STEP
    return pl.pallas_call(
        _moba_kernel,
        grid=(bsz, h // nh, nb),
        in_specs=[pl.BlockSpec((1, nh, blk, dh), lambda b, j, i: (b, j, i, 0)),
                  pl.BlockSpec((1, nh, seq, dh), lambda b, j, i: (b, j, 0, 0)),
                  pl.BlockSpec((1, nh, nb, dh, blk), lambda b, j, i: (b, j, 0, 0, 0)),
                  pl.BlockSpec((1, nh, nbp, dh), lambda b, j, i: (b, j, 0, 0))],
        out_specs=pl.BlockSpec((blk, nh * dh), lambda b, j, i: (b * nb + i, j)),
        out_shape=jax.ShapeDtypeStruct((bsz * seq, h * dh), BF16),
        scratch_shapes=[pltpu.VMEM((nh, nbp, blk), F32)] + [pltpu.VMEM((nh, 2, blk, blk), F32)] * 2
                       + [pltpu.VMEM((nh, blk, blk), F32)],
        compiler_params=_cparams(3, 48),
    )(q, k, vt, km_pad)


def _pos_kernel(oh_ref, ei_ref, dest_ref, meta_ref, cnt_scr, off_scr, run_scr):
    p = pl.program_id(0)
    i = pl.program_id(1)
    tm = oh_ref.shape[0]
    oh = oh_ref[...]
    colsum = jnp.sum(oh, axis=0, keepdims=True)

    @pl.when((p == 0) & (i == 0))
    def _():
        cnt_scr[...] = jnp.zeros_like(cnt_scr)

    @pl.when(p == 0)
    def _():
        cnt_scr[...] += colsum
        dest_ref[...] = jnp.zeros_like(dest_ref)

    @pl.when((p == 1) & (i == 0))
    def _():
        cnt = cnt_scr[...]
        padded = jnp.ceil(cnt * (1.0 / TM_EXPERT)) * float(TM_EXPERT)
        r = lax.broadcasted_iota(jnp.int32, (LANES, LANES), 0)
        cidx = lax.broadcasted_iota(jnp.int32, (LANES, LANES), 1)
        tri = (r < cidx).astype(F32)
        off = jnp.dot(jnp.broadcast_to(padded, (8, LANES)), tri,
                      preferred_element_type=F32, precision=HI)
        off_scr[...] = off[0:1]
        run_scr[...] = jnp.zeros_like(run_scr)

    @pl.when(p == 1)
    def _():
        r = lax.broadcasted_iota(jnp.int32, (tm, tm), 0)
        cidx = lax.broadcasted_iota(jnp.int32, (tm, tm), 1)
        ltri = (r > cidx).astype(BF16)
        excl = jnp.dot(ltri, oh.astype(BF16), preferred_element_type=F32)
        dfull = off_scr[...] + run_scr[...] + excl
        ei = ei_ref[...]
        lane = lax.broadcasted_iota(jnp.int32, oh.shape, 1)
        lane_f = lane.astype(F32)
        dest = jnp.zeros(oh.shape, F32)
        for kk in range(TOP_K):
            ek = jnp.sum(jnp.where(lane == kk, ei, 0.0), axis=-1, keepdims=True)
            dk = jnp.sum(jnp.where(lane_f == ek, dfull, 0.0), axis=-1, keepdims=True)
            dest = jnp.where(lane == kk, dk, dest)
        dest_ref[...] = dest.astype(jnp.int32)
        run_scr[...] += colsum

    meta_ref[...] = jnp.concatenate([off_scr[...], cnt_scr[...],
                                     jnp.zeros((6, LANES), F32)], axis=0)


def _positions(onehot, ei):
    t = onehot.shape[0]
    tm = TM_DENSE
    tile = pl.BlockSpec((tm, LANES), lambda p, i: (i, 0))
    return pl.pallas_call(
        _pos_kernel,
        grid=(2, t // tm),
        in_specs=[tile, tile],
        out_specs=[pl.BlockSpec((tm, LANES), lambda p, i: (i * p, 0)),
                   pl.BlockSpec((8, LANES), lambda p, i: (0, 0))],
        out_shape=[jax.ShapeDtypeStruct((t, LANES), jnp.int32),
                   jax.ShapeDtypeStruct((8, LANES), F32)],
        scratch_shapes=[pltpu.VMEM((1, LANES), F32)] * 3,
        compiler_params=_cparams(2, 16),
    )(onehot, ei)


def _src_kernel(dest_ref, pad_lo_ref, pad_hi_ref, src_ref):
    i = pl.program_id(0)
    tb = dest_ref.shape[0] // TOP_K

    @pl.when(i == 0)
    def _():
        def zero(j, carry):
            src_ref[j] = 0
            return carry

        def per_range(rr, carry):
            lax.fori_loop(pad_lo_ref[rr], pad_hi_ref[rr], zero, 0)
            return carry
        lax.fori_loop(0, pad_lo_ref.shape[0], per_range, 0)

    def body(tt, carry):
        for kk in range(TOP_K):
            src_ref[dest_ref[tt * TOP_K + kk]] = i * tb + tt
        return carry

    lax.fori_loop(0, tb, body, 0, unroll=8)


def _sources(dest_flat, pad_lo, pad_hi, n_rows):
    n = dest_flat.shape[0]
    tb = TM_SRC
    whole = pl.BlockSpec(memory_space=pltpu.SMEM)
    return pl.pallas_call(
        _src_kernel,
        grid=(n // (tb * TOP_K),),
        in_specs=[pl.BlockSpec((tb * TOP_K,), lambda i: (i,), memory_space=pltpu.SMEM), whole, whole],
        out_specs=pl.BlockSpec(memory_space=pltpu.SMEM),
        out_shape=jax.ShapeDtypeStruct((n_rows,), jnp.int32),
        compiler_params=_cparams(1, 16),
    )(dest_flat, pad_lo, pad_hi)


def _expert_kernel(te_ref, tv_ref, nx_ref, src_cur, src_nxt, h_hbm, wgu_hbm, bgu_ref, wd_hbm, bd_ref,
                   o_ref, xbuf, sem, wgu_st, wd_st, wsem, wgu_bf, wd_bf):
    i = pl.program_id(0)
    nt = pl.num_programs(0)
    tm = o_ref.shape[0]
    slot = lax.rem(i, 2)
    e = te_ref[i]
    prev = te_ref[jnp.maximum(i - 1, 0)]
    dff = wd_hbm.shape[1]

    def weight_copies(ex):
        return (pltpu.make_async_copy(wgu_hbm.at[ex], wgu_st, wsem.at[0]),
                pltpu.make_async_copy(wd_hbm.at[ex], wd_st, wsem.at[1]))

    def gather(src_ref, s):
        def body(r8, carry):
            for uu in range(8):
                pltpu.make_async_copy(h_hbm.at[pl.ds(src_ref[r8 * 8 + uu], 1)],
                                      xbuf.at[s, r8, pl.ds(uu, 1)], sem.at[s]).start()
            return carry
        lax.fori_loop(0, tm // 8, body, 0)

    @pl.when(i == 0)
    def _():
        for cp in weight_copies(e):
            cp.start(priority=1)

    @pl.when((i == 0) & (tv_ref[0] > 0))
    def _():
        gather(src_cur, 0)

    nxt = jnp.minimum(i + 1, nt - 1)

    @pl.when((i + 1 < nt) & (tv_ref[nxt] > 0))
    def _():
        gather(src_nxt, 1 - slot)

    @pl.when((i == 0) | (e != prev))
    def _():
        for cp in weight_copies(e):
            cp.wait()
        wgu_bf[...] = wgu_st[...].astype(BF16)
        wd_bf[...] = wd_st[...].astype(BF16)

        @pl.when(nx_ref[i] >= 0)
        def _():
            for cp in weight_copies(nx_ref[i]):
                cp.start(priority=1)

    @pl.when(tv_ref[i] > 0)
    def _():
        for r8 in range(tm // 8):
            pltpu.make_async_copy(h_hbm.at[pl.ds(0, 8)], xbuf.at[slot, r8], sem.at[slot]).wait()
        x = _unpack_rows(xbuf[slot].reshape(tm, xbuf.shape[3])).astype(BF16)
        hu = jnp.dot(x, wgu_bf[...], preferred_element_type=F32) + bgu_ref[0]
        g = jnp.minimum(hu[:, :dff], SWIGLU_LIMIT)
        up = jnp.clip(hu[:, dff:], -SWIGLU_LIMIT, SWIGLU_LIMIT)
        act = (up + 1.0) * g * jax.nn.sigmoid(SWIGLU_ALPHA * g)
        y = jnp.dot(act.astype(BF16), wd_bf[...], preferred_element_type=F32) + bd_ref[0]
        o_ref[...] = _pack_rows(y)

    @pl.when(tv_ref[i] == 0)
    def _():
        o_ref[...] = jnp.zeros_like(o_ref)


def _experts(tile_e, tile_v, next_e, src, h2p, w_gu, b_gu, w_d, b_d):
    n_rows = src.shape[0]
    dp = h2p.shape[1]
    ne, d, f2 = w_gu.shape
    dff = w_d.shape[1]
    tm = TM_EXPERT
    nt = n_rows // tm
    grid_spec = pltpu.PrefetchScalarGridSpec(
        num_scalar_prefetch=3,
        grid=(nt,),
        in_specs=[pl.BlockSpec((tm,), lambda i, te, tv, nx: (i,), memory_space=pltpu.SMEM),
                  pl.BlockSpec((tm,), lambda i, te, tv, nx: (jnp.minimum(i + 1, nt - 1),),
                               memory_space=pltpu.SMEM),
                  pl.BlockSpec(memory_space=pl.ANY),
                  pl.BlockSpec(memory_space=pl.ANY),
                  pl.BlockSpec((1, 1, f2), lambda i, te, tv, nx: (te[i], 0, 0)),
                  pl.BlockSpec(memory_space=pl.ANY),
                  pl.BlockSpec((1, 1, d), lambda i, te, tv, nx: (te[i], 0, 0))],
        out_specs=pl.BlockSpec((tm, dp), lambda i, te, tv, nx: (i, 0)),
        scratch_shapes=[pltpu.VMEM((2, tm // 8, 8, dp), jnp.uint32), pltpu.SemaphoreType.DMA((2,)),
                        pltpu.VMEM((d, f2), F32), pltpu.VMEM((dff, d), F32),
                        pltpu.SemaphoreType.DMA((2,)),
                        pltpu.VMEM((d, f2), BF16), pltpu.VMEM((dff, d), BF16)],
    )
    return pl.pallas_call(
        _expert_kernel,
        grid_spec=grid_spec,
        out_shape=jax.ShapeDtypeStruct((n_rows, dp), jnp.uint32),
        compiler_params=_cparams(1, 48),
    )(tile_e, tile_v, next_e, src, src, h2p, w_gu, b_gu.reshape(ne, 1, f2), w_d, b_d.reshape(ne, 1, d))


def _combine_kernel(dcur, dnxt, gt_ref, x_ref, g_ref, gf_ref, ys_hbm, o_ref, buf, sem, *, final):
    i = pl.program_id(0)
    nt = pl.num_programs(0)
    td = x_ref.shape[0]
    slot = lax.rem(i, 2)

    def gather(dref, s):
        def body(t8, carry):
            for uu in range(8):
                for kk in range(TOP_K):
                    pltpu.make_async_copy(ys_hbm.at[pl.ds(dref[(t8 * 8 + uu) * TOP_K + kk], 1)],
                                          buf.at[s, kk, t8, pl.ds(uu, 1)], sem.at[s]).start()
            return carry
        lax.fori_loop(0, td // 8, body, 0)

    @pl.when(i == 0)
    def _():
        gather(dcur, 0)

    @pl.when(i + 1 < nt)
    def _():
        gather(dnxt, 1 - slot)

    for kk in range(TOP_K):
        for t8 in range(td // 8):
            pltpu.make_async_copy(ys_hbm.at[pl.ds(0, 8)], buf.at[slot, kk, t8], sem.at[slot]).wait()
    gt = gt_ref[...]
    rows = lambda kk: _unpack_rows(buf[slot, kk].reshape(td, buf.shape[4]))
    y = gt[:, 0:1] * rows(0)
    for kk in range(1, TOP_K):
        y = y + gt[:, kk:kk + 1] * rows(kk)
    xn = x_ref[...] + g_ref[0] * y
    if final:
        ms = jnp.mean(xn * xn, axis=-1, keepdims=True)
        xn = (xn * lax.rsqrt(ms + EPS)) * gf_ref[...]
    o_ref[...] = xn


def _combine(dest_flat, gates, x2, gate_vec, gf, ys, seq, final):
    t, d = x2.shape
    td = TD_ROWS
    tpb = seq // td
    nt = t // td
    return pl.pallas_call(
        functools.partial(_combine_kernel, final=final),
        grid=(nt,),
        in_specs=[pl.BlockSpec((td * TOP_K,), lambda i: (i,), memory_space=pltpu.SMEM),
                  pl.BlockSpec((td * TOP_K,), lambda i: (jnp.minimum(i + 1, nt - 1),),
                               memory_space=pltpu.SMEM),
                  pl.BlockSpec((td, LANES), lambda i: (i, 0)),
                  pl.BlockSpec((td, d), lambda i: (i, 0)),
                  pl.BlockSpec((1, 1, d), lambda i: (i // tpb, 0, 0)),
                  pl.BlockSpec((1, d), lambda i: (0, 0)),
                  pl.BlockSpec(memory_space=pl.ANY)],
        out_specs=pl.BlockSpec((td, d), lambda i: (i, 0)),
        out_shape=jax.ShapeDtypeStruct((t, d), F32),
        scratch_shapes=[pltpu.VMEM((2, TOP_K, td // 8, 8, d // 2), jnp.uint32),
                        pltpu.SemaphoreType.DMA((2,))],
        compiler_params=_cparams(1, 32),
    )(dest_flat, dest_flat, gates, x2, gate_vec, gf.reshape(1, d), ys)


def _moe(x2, routed, gate_vec, w_gu, b_gu, w_d, b_d, gf, seq, final):
    t, d = x2.shape
    ne = w_gu.shape[0]
    h2, onehot, ei, gates = routed
    dest, meta = _positions(onehot, ei)
    dest_flat = dest[:, :TOP_K].reshape(t * TOP_K)

    tm = TM_EXPERT
    n_rows = t * TOP_K + ne * tm
    n_tiles = n_rows // tm
    off = meta[0, :ne].astype(jnp.int32)
    cnt = meta[1, :ne].astype(jnp.int32)
    ends = off + ((cnt + tm - 1) // tm) * tm
    starts = jnp.arange(n_tiles, dtype=jnp.int32) * tm
    valid = starts < ends[-1]
    te = jnp.sum((starts[:, None] >= ends[None, :]).astype(jnp.int32), axis=1)
    te_last = jnp.sum((ends[-1] - 1 >= ends).astype(jnp.int32))
    tile_e = jnp.minimum(jnp.where(valid, te, te_last), ne - 1)
    tile_v = valid.astype(jnp.int32)
    eidx = jnp.arange(ne, dtype=jnp.int32)
    later_used = (eidx[None, :] > eidx[:, None]) & (cnt[None, :] > 0)
    next_of = jnp.min(jnp.where(later_used, eidx[None, :], ne), axis=1)
    next_of = jnp.where(next_of == ne, -1, next_of)
    next_e = jnp.sum(jnp.where(tile_e[:, None] == eidx[None, :], next_of[None, :], 0), axis=1)

    pad_lo = jnp.concatenate([off + cnt, ends[-1:]]).astype(jnp.int32)
    pad_hi = jnp.concatenate([ends, jnp.full((1,), n_rows, jnp.int32)]).astype(jnp.int32)
    src = _sources(dest_flat, pad_lo, pad_hi, n_rows)
    ys = _experts(tile_e, tile_v, next_e.astype(jnp.int32), src, h2, w_gu, b_gu, w_d, b_d)
    return _combine(dest_flat, gates, x2, gate_vec, gf, ys, seq, final)


def kernel(x, c, positions, l0_norm1_g, l0_ada_w, l0_ada_b, l0_s5_w_in, l0_s5_b_re, l0_s5_b_im, l0_s5_c_re, l0_s5_c_im, l0_s5_lam_re, l0_s5_lam_im, l0_s5_log_dt, l0_s5_d, l0_s5_w_glu, l0_s5_w_out, l0_norm2_g, l0_moe_w_router, l0_moe_b_router, l0_moe_w_gate_up, l0_moe_b_gate_up, l0_moe_w_down, l0_moe_b_down, l1_norm1_g, l1_ada_w, l1_ada_b, l1_moba_w_qkv, l1_moba_w_o, l1_norm2_g, l1_moe_w_router, l1_moe_b_router, l1_moe_w_gate_up, l1_moe_b_gate_up, l1_moe_w_down, l1_moe_b_down, final_norm_g):
    bsz, seq, d = x.shape
    t = bsz * seq
    assert seq % TM_DENSE == 0 and seq % MOBA_BLOCK == 0 and seq % S5_CHUNK == 0
    assert seq // MOBA_BLOCK <= LANES and d == N_HEADS * HEAD_DIM
    x2 = x.reshape(t, d)
    c_pad = jnp.zeros((8, d), F32).at[:bsz].set(c)

    def ada_parts(w, b):
        a = _ada(c_pad, w, b)[:bsz]
        return [a[:, i * d:(i + 1) * d].reshape(bsz, 1, d) for i in range(6)]

    sh1, sc1, g1, sh2, sc2, g2 = ada_parts(l0_ada_w, l0_ada_b)
    u = _nml(x2, l0_norm1_g, sc1, sh1, l0_s5_w_in.astype(BF16), seq, F32)
    rate, theta, bblk, cblk = _s5_params(l0_s5_b_re, l0_s5_b_im, l0_s5_c_re, l0_s5_c_im,
                                         l0_s5_lam_re, l0_s5_lam_im, l0_s5_log_dt)
    z = _s5_scan(u, rate, theta, bblk, cblk, l0_s5_d, bsz, seq)
    x2 = _row_call(_glu_out_kernel, z, x2, g1,
                   [l0_s5_w_glu.astype(BF16), l0_s5_w_out.astype(BF16)], seq)
    routed = _router(x2, l0_norm2_g, sc2, sh2, l0_moe_w_router, l0_moe_b_router, seq)
    x2 = _moe(x2, routed, g2, l0_moe_w_gate_up, l0_moe_b_gate_up, l0_moe_w_down, l0_moe_b_down,
              final_norm_g, seq, final=False)

    sh1, sc1, g1, sh2, sc2, g2 = ada_parts(l1_ada_w, l1_ada_b)
    qkv = _nml(x2, l1_norm1_g, sc1, sh1, l1_moba_w_qkv.astype(BF16), seq, BF16)
    cos, sin = _rope_tables(positions.reshape(t, 1))
    q, k, v, km = _rope(qkv, cos, sin, bsz, seq)
    nb = seq // MOBA_BLOCK
    nbp = -(-nb // 8) * 8
    km_pad = jnp.zeros((bsz, N_HEADS, nbp, HEAD_DIM), F32).at[:, :, :nb].set(km[:, :, :, 0])
    o = _moba(q, k, v, km_pad, bsz, seq)
    x2 = _row_call(_lin_res_kernel, o, x2, g1, [l1_moba_w_o.astype(BF16)], seq)
    routed = _router(x2, l1_norm2_g, sc2, sh2, l1_moe_w_router, l1_moe_b_router, seq)
    x2 = _moe(x2, routed, g2, l1_moe_w_gate_up, l1_moe_b_gate_up, l1_moe_w_down, l1_moe_b_down,
              final_norm_g, seq, final=True)
    return x2.reshape(bsz, seq, d)
```

```python
import functools
import math

import jax
import jax.numpy as jnp
from jax import lax
from jax.experimental import pallas as pl
from jax.experimental.pallas import tpu as pltpu

F32 = jnp.float32
BF16 = jnp.bfloat16
HI = lax.Precision.HIGHEST

EPS = 1e-6
NEG = -1e30
LOG2E = math.log2(math.e)
LANES = 128
MIB = 1024 * 1024

S5_GROUP = 16
S5_STATE = 64
N_HEADS = 8
HEAD_DIM = 128
MOBA_BLOCK = 256
MOBA_TOPK = 3
ROPE_THETA = 10000.0
N_EXPERTS = 32
TOP_K = 4
SWIGLU_LIMIT = 7.0
SWIGLU_ALPHA = 1.702

TM_DENSE = 512
S5_CHUNK = 256
S5_SEG = 32
S5_SUPER = 8
MOBA_HEADS_PER_STEP = 2
TM_EXPERT = 256
TD_ROWS = 256
TM_SRC = 2048


def _cparams(n_axes, vmem_mib):
    return pltpu.CompilerParams(dimension_semantics=("arbitrary",) * n_axes,
                                vmem_limit_bytes=vmem_mib * MIB)


def _pack_rows(x):
    n = x.shape[1] // 2
    xb = x.astype(BF16).astype(F32)
    lo = lax.bitcast_convert_type(xb[:, :n], jnp.uint32) >> 16
    hi = lax.bitcast_convert_type(xb[:, n:], jnp.uint32) & jnp.uint32(0xFFFF0000)
    return lo | hi


def _unpack_rows(p):
    lo = lax.bitcast_convert_type(p << 16, F32)
    hi = lax.bitcast_convert_type(p & jnp.uint32(0xFFFF0000), F32)
    return jnp.concatenate([lo, hi], axis=1)


def _normmod(x, g, sc, sh):
    ms = jnp.mean(x * x, axis=-1, keepdims=True)
    return (x * lax.rsqrt(ms + EPS)) * g * (1.0 + sc) + sh


def _ada_kernel(c_ref, w_ref, b_ref, o_ref):
    c = c_ref[...]
    ca = c * jax.nn.sigmoid(c)
    o_ref[...] = jnp.dot(ca, w_ref[...], preferred_element_type=F32, precision=HI) + b_ref[...]


def _ada(c_pad, w, b):
    d, n = w.shape
    tn = 1536
    return pl.pallas_call(
        _ada_kernel,
        grid=(n // tn,),
        in_specs=[pl.BlockSpec((8, d), lambda j: (0, 0)),
                  pl.BlockSpec((d, tn), lambda j: (0, j)),
                  pl.BlockSpec((1, tn), lambda j: (0, j))],
        out_specs=pl.BlockSpec((8, tn), lambda j: (0, j)),
        out_shape=jax.ShapeDtypeStruct((8, n), F32),
        compiler_params=_cparams(1, 32),
    )(c_pad, w, b.reshape(1, n))


def _nml_kernel(x_ref, g_ref, sc_ref, sh_ref, w_ref, o_ref):
    h = _normmod(x_ref[...], g_ref[...], sc_ref[0], sh_ref[0])
    o_ref[...] = jnp.dot(h.astype(BF16), w_ref[...], preferred_element_type=F32).astype(o_ref.dtype)


def _nml(x2, g, sc, sh, w_bf, seq, out_dtype):
    t, d = x2.shape
    n = w_bf.shape[1]
    tm = TM_DENSE
    tpb = seq // tm
    return pl.pallas_call(
        _nml_kernel,
        grid=(t // tm,),
        in_specs=[pl.BlockSpec((tm, d), lambda i: (i, 0)),
                  pl.BlockSpec((1, d), lambda i: (0, 0)),
                  pl.BlockSpec((1, 1, d), lambda i: (i // tpb, 0, 0)),
                  pl.BlockSpec((1, 1, d), lambda i: (i // tpb, 0, 0)),
                  pl.BlockSpec((d, n), lambda i: (0, 0))],
        out_specs=pl.BlockSpec((tm, n), lambda i: (i, 0)),
        out_shape=jax.ShapeDtypeStruct((t, n), out_dtype),
        compiler_params=_cparams(1, 48),
    )(x2, g.reshape(1, d), sc, sh, w_bf)


def _gelu_tanh(y):
    c = math.sqrt(2.0 / math.pi)
    return y * (0.5 * (1.0 + jnp.tanh(c * (y + 0.044715 * (y * y * y)))))


def _s5_kernel(u_ref, rate_ref, theta_ref, bblk_ref, cblk_ref, d_ref, z_ref,
               apr, api, a1r, a1i, air, aii, xr, xi):
    L = u_ref.shape[0]
    seg = apr.shape[0]
    n_super = bblk_ref.shape[0]
    half = apr.shape[1] // n_super
    nch = u_ref.shape[1] // n_super
    b = pl.program_id(0)
    c = pl.program_id(1)

    @pl.when((b == 0) & (c == 0))
    def _():
        tt = lax.broadcasted_iota(jnp.int32, (seg, 1), 0).astype(F32)
        ph = theta_ref[...] * tt
        rt = rate_ref[...] * tt
        cs = jnp.cos(ph)
        sn = jnp.sin(ph)
        mag = jnp.exp(rt)
        inv = jnp.exp(-rt)
        apr[...] = mag * cs
        api[...] = mag * sn
        air[...] = inv * cs
        aii[...] = -(inv * sn)
        ph1 = theta_ref[...] * (tt + 1.0)
        mag1 = jnp.exp(rate_ref[...] * (tt + 1.0))
        a1r[...] = mag1 * jnp.cos(ph1)
        a1i[...] = mag1 * jnp.sin(ph1)

    @pl.when(c == 0)
    def _():
        xr[...] = jnp.zeros_like(xr)
        xi[...] = jnp.zeros_like(xi)

    row = lax.broadcasted_iota(jnp.int32, (L, L), 0)
    col = lax.broadcasted_iota(jnp.int32, (L, L), 1)
    tril = ((row >= col) & (row // seg == col // seg)).astype(BF16)

    for sg in range(n_super):
        cols = slice(sg * half, (sg + 1) * half)
        ch = slice(sg * nch, (sg + 1) * nch)
        u = u_ref[:, ch]
        bu = jnp.dot(u.astype(BF16), bblk_ref[sg], preferred_element_type=F32)
        ir = air[:, cols]
        ii = aii[:, cols]
        zs = []
        for k in range(L // seg):
            bur = bu[k * seg:(k + 1) * seg, :half]
            bui = bu[k * seg:(k + 1) * seg, half:]
            zs.append(jnp.concatenate([bur * ir - bui * ii, bur * ii + bui * ir], axis=1))
        zc = jnp.concatenate(zs, axis=0).astype(BF16)
        cum = jnp.dot(tril, zc, preferred_element_type=F32)
        pw_r = apr[:, cols]
        pw_i = api[:, cols]
        p1_r = a1r[:, cols]
        p1_i = a1i[:, cols]
        last_r = xr[:, cols]
        last_i = xi[:, cols]
        xs = []
        for k in range(L // seg):
            cr = cum[k * seg:(k + 1) * seg, :half]
            ci = cum[k * seg:(k + 1) * seg, half:]
            x_re = cr * pw_r - ci * pw_i + (p1_r * last_r - p1_i * last_i)
            x_im = cr * pw_i + ci * pw_r + (p1_r * last_i + p1_i * last_r)
            last_r = x_re[seg - 1:seg, :]
            last_i = x_im[seg - 1:seg, :]
            xs.append(jnp.concatenate([x_re, x_im], axis=1))
        xr[:, cols] = last_r
        xi[:, cols] = last_i
        xc = jnp.concatenate(xs, axis=0).astype(BF16)
        y = jnp.dot(xc, cblk_ref[sg], preferred_element_type=F32) + d_ref[:, ch] * u
        z_ref[:, ch] = _gelu_tanh(y)


def _s5_scan(u, rate, theta, bblk, cblk, d_skip, bsz, seq):
    t, w = u.shape
    L = S5_CHUNK
    nc = seq // L
    ns = rate.shape[1]
    return pl.pallas_call(
        _s5_kernel,
        grid=(bsz, nc),
        in_specs=[pl.BlockSpec((L, w), lambda b, c: (b * nc + c, 0)),
                  pl.BlockSpec((1, ns), lambda b, c: (0, 0)),
                  pl.BlockSpec((1, ns), lambda b, c: (0, 0)),
                  pl.BlockSpec(bblk.shape, lambda b, c: (0, 0, 0)),
                  pl.BlockSpec(cblk.shape, lambda b, c: (0, 0, 0)),
                  pl.BlockSpec((1, w), lambda b, c: (0, 0))],
        out_specs=pl.BlockSpec((L, w), lambda b, c: (b * nc + c, 0)),
        out_shape=jax.ShapeDtypeStruct((t, w), F32),
        scratch_shapes=[pltpu.VMEM((S5_SEG, ns), F32)] * 6 + [pltpu.VMEM((1, ns), F32)] * 2,
        compiler_params=_cparams(2, 48),
    )(u, rate, theta, bblk, cblk, d_skip.reshape(1, w))


def _s5_params(b_re, b_im, c_re, c_im, lam_re, lam_im, log_dt):
    g, p, ch = b_re.shape
    dt = jnp.exp(log_dt)[:, None]
    rate = lam_re * dt
    theta = lam_im * dt
    mag = jnp.exp(rate)
    a_re = mag * jnp.cos(theta)
    a_im = mag * jnp.sin(theta)
    den = lam_re * lam_re + lam_im * lam_im
    f_re = ((a_re - 1.0) * lam_re + a_im * lam_im) / den
    f_im = (a_im * lam_re - (a_re - 1.0) * lam_im) / den
    bb_re = f_re[..., None] * b_re - f_im[..., None] * b_im
    bb_im = f_re[..., None] * b_im + f_im[..., None] * b_re
    ns = S5_SUPER
    nsg = g // ns
    eye = jnp.eye(ns, dtype=F32)

    def blk_b(bb):
        return jnp.einsum('sgpc,gh->sgchp', bb.reshape(nsg, ns, p, ch), eye).reshape(nsg, ns * ch, ns * p)

    def blk_c(cc):
        return jnp.einsum('sgcp,gh->sgphc', cc.reshape(nsg, ns, ch, p), eye).reshape(nsg, ns * p, ns * ch)

    bblk = jnp.concatenate([blk_b(bb_re), blk_b(bb_im)], axis=2).astype(BF16)
    cblk = jnp.concatenate([blk_c(c_re), -blk_c(c_im)], axis=1).astype(BF16)
    return rate.reshape(1, g * p), theta.reshape(1, g * p), bblk, cblk


def _route(x, g, sc, sh, wr, br):
    h = _normmod(x, g, sc, sh)
    logits = jnp.dot(h, wr, preferred_element_type=F32, precision=HI) + br
    lane = lax.broadcasted_iota(jnp.int32, logits.shape, 1)
    lane_f = lane.astype(F32)
    work = logits
    onehot = jnp.zeros(logits.shape, F32)
    ei = jnp.zeros(logits.shape, F32)
    vals = []
    for kk in range(TOP_K):
        m = jnp.max(work, axis=-1, keepdims=True)
        idx = jnp.min(jnp.where(work == m, lane_f, float(LANES)), axis=-1, keepdims=True)
        hit = lane_f == idx
        onehot = jnp.where(hit, 1.0, onehot)
        ei = jnp.where(lane == kk, idx, ei)
        vals.append(m)
        work = jnp.where(hit, 2.0 * NEG, work)
    ex = [jnp.exp(vv - vals[0]) for vv in vals]
    tot = ex[0] + ex[1] + ex[2] + ex[3]
    gt = jnp.zeros(logits.shape, F32)
    for kk in range(TOP_K):
        gt = jnp.where(lane == kk, ex[kk] / tot, gt)
    return _pack_rows(h), onehot, ei, gt


def _glu_out_kernel(z_ref, x_ref, g_ref, wg_ref, wo_ref, o_ref):
    z = z_ref[...]
    gate = jax.nn.sigmoid(jnp.dot(z.astype(BF16), wg_ref[...], preferred_element_type=F32))
    m = jnp.dot((z * gate).astype(BF16), wo_ref[...], preferred_element_type=F32)
    o_ref[...] = x_ref[...] + g_ref[0] * m


def _lin_res_kernel(a_ref, x_ref, g_ref, w_ref, o_ref):
    m = jnp.dot(a_ref[...].astype(BF16), w_ref[...], preferred_element_type=F32)
    o_ref[...] = x_ref[...] + g_ref[0] * m


def _row_call(kernel, acts, x2, gate, weights, seq):
    t, d = x2.shape
    tm = TM_DENSE
    tpb = seq // tm
    row = pl.BlockSpec((tm, d), lambda i: (i, 0))
    return pl.pallas_call(
        kernel,
        grid=(t // tm,),
        in_specs=[row, row, pl.BlockSpec((1, 1, d), lambda i: (i // tpb, 0, 0))]
                 + [pl.BlockSpec(w.shape, lambda i: (0, 0)) for w in weights],
        out_specs=row,
        out_shape=jax.ShapeDtypeStruct((t, d), F32),
        compiler_params=_cparams(1, 48),
    )(acts, x2, gate, *weights)


def _router_kernel(x_ref, g_ref, sc_ref, sh_ref, wr_ref, br_ref, h_ref, oh_ref, ei_ref, gt_ref):
    h_ref[...], oh_ref[...], ei_ref[...], gt_ref[...] = _route(
        x_ref[...], g_ref[...], sc_ref[0], sh_ref[0], wr_ref[...], br_ref[...])


def _router(x2, g, sc, sh, w_router, b_router, seq):
    t, d = x2.shape
    ne = w_router.shape[1]
    wr_pad = jnp.zeros((d, LANES), F32).at[:, :ne].set(w_router)
    br_pad = jnp.full((1, LANES), NEG, F32).at[0, :ne].set(b_router)
    tm = TM_DENSE
    tpb = seq // tm
    lane_tile = pl.BlockSpec((tm, LANES), lambda i: (i, 0))
    per_batch = pl.BlockSpec((1, 1, d), lambda i: (i // tpb, 0, 0))
    return pl.pallas_call(
        _router_kernel,
        grid=(t // tm,),
        in_specs=[pl.BlockSpec((tm, d), lambda i: (i, 0)),
                  pl.BlockSpec((1, d), lambda i: (0, 0)), per_batch, per_batch,
                  pl.BlockSpec((d, LANES), lambda i: (0, 0)),
                  pl.BlockSpec((1, LANES), lambda i: (0, 0))],
        out_specs=[pl.BlockSpec((tm, d // 2), lambda i: (i, 0)), lane_tile, lane_tile, lane_tile],
        out_shape=[jax.ShapeDtypeStruct((t, d // 2), jnp.uint32)]
                  + [jax.ShapeDtypeStruct((t, LANES), F32)] * 3,
        compiler_params=_cparams(1, 32),
    )(x2, g.reshape(1, d), sc, sh, wr_pad, br_pad)


def _rope_tab_kernel(pos_ref, inv_ref, sgn_ref, cos_ref, sin_ref):
    ang = pos_ref[...].astype(F32) * inv_ref[...]
    cos_ref[...] = jnp.cos(ang)
    sin_ref[...] = jnp.sin(ang) * sgn_ref[...]


def _rope_tables(pos_col):
    t = pos_col.shape[0]
    ts = 512
    half = HEAD_DIM // 2
    inv = ROPE_THETA ** (-jnp.arange(0, HEAD_DIM, 2, dtype=F32) / HEAD_DIM)
    inv = jnp.concatenate([inv, inv]).reshape(1, HEAD_DIM)
    sgn = jnp.concatenate([-jnp.ones((half,), F32), jnp.ones((half,), F32)]).reshape(1, HEAD_DIM)
    const = pl.BlockSpec((1, HEAD_DIM), lambda i: (0, 0))
    tile = pl.BlockSpec((ts, HEAD_DIM), lambda i: (i, 0))
    return pl.pallas_call(
        _rope_tab_kernel,
        grid=(t // ts,),
        in_specs=[pl.BlockSpec((ts, 1), lambda i: (i, 0)), const, const],
        out_specs=[tile, tile],
        out_shape=[jax.ShapeDtypeStruct((t, HEAD_DIM), F32)] * 2,
        compiler_params=_cparams(1, 16),
    )(pos_col, inv, sgn)


def _rope_kernel(q_ref, k_ref, v_ref, cos_ref, sin_ref, qo_ref, ko_ref, vo_ref, km_ref):
    cs = cos_ref[...]
    sn = sin_ref[...]
    dh = cs.shape[1]
    half = dh // 2
    for hh in range(qo_ref.shape[1]):
        cols = slice(hh * dh, (hh + 1) * dh)
        q = q_ref[:, cols].astype(F32)
        k = k_ref[:, cols].astype(F32)
        qr = q * cs + pltpu.roll(q, half, 1) * sn
        kr = k * cs + pltpu.roll(k, half, 1) * sn
        qo_ref[0, hh] = qr
        ko_ref[0, hh] = kr.astype(BF16)
        vo_ref[0, hh, 0] = v_ref[:, cols].astype(F32).T.astype(BF16)
        km_ref[0, hh, 0] = jnp.mean(kr, axis=0, keepdims=True)


def _rope(qkv, cos, sin, bsz, seq):
    t = qkv.shape[0]
    h, dh, blk = N_HEADS, HEAD_DIM, MOBA_BLOCK
    nb = seq // blk
    part = lambda p: pl.BlockSpec((blk, h * dh), lambda i: (i, p))
    tab = pl.BlockSpec((blk, dh), lambda i: (i, 0))
    head = pl.BlockSpec((1, h, blk, dh), lambda i: (i // nb, 0, i % nb, 0))
    return pl.pallas_call(
        _rope_kernel,
        grid=(t // blk,),
        in_specs=[part(0), part(1), part(2), tab, tab],
        out_specs=[head, head,
                   pl.BlockSpec((1, h, 1, dh, blk), lambda i: (i // nb, 0, i % nb, 0, 0)),
                   pl.BlockSpec((1, h, 1, 1, dh), lambda i: (i // nb, 0, i % nb, 0, 0))],
        out_shape=[jax.ShapeDtypeStruct((bsz, h, seq, dh), F32),
                   jax.ShapeDtypeStruct((bsz, h, seq, dh), BF16),
                   jax.ShapeDtypeStruct((bsz, h, nb, dh, blk), BF16),
                   jax.ShapeDtypeStruct((bsz, h, nb, 1, dh), F32)],
        compiler_params=_cparams(1, 32),
    )(qkv, qkv, qkv, cos, sin)


def _moba_kernel(q_ref, k_ref, vt_ref, km_ref, o_ref, sel_scr, s_a, s_b, s_own):
    nh = q_ref.shape[1]
    blk = q_ref.shape[2]
    dh = q_ref.shape[3]
    j = pl.program_id(2)
    nt = (((1,), (1,)), ((), ()))
    heads = range(nh)

    qs = []
    for hh in heads:
        q = q_ref[0, hh]
        gate = lax.dot_general(km_ref[0, hh], q, nt, preferred_element_type=F32, precision=HI)
        bidx = lax.broadcasted_iota(jnp.int32, gate.shape, 0)
        bidx_f = bidx.astype(F32)
        work = jnp.where(bidx < j, gate, NEG)
        sel = jnp.zeros(gate.shape, F32)
        for _ in range(MOBA_TOPK):
            m = jnp.max(work, axis=0, keepdims=True)
            idx = jnp.min(jnp.where(work == m, bidx_f, float(LANES)), axis=0, keepdims=True)
            hit = bidx_f == idx
            sel = jnp.where(hit & (m > 0.5 * NEG), 1.0, sel)
            work = jnp.where(hit, 2.0 * NEG, work)
        sel_scr[hh] = sel
        qs.append((q * (dh ** -0.5 * LOG2E)).astype(BF16))

    def scores(hh, n):
        start = pl.multiple_of(n * blk, blk)
        kb = k_ref[0, hh, pl.ds(start, blk), :]
        return lax.dot_general(kb, qs[hh], nt, preferred_element_type=F32)

    nb_last = vt_ref.shape[2] - 1

    def score_trip(scr, nn):
        for hh in heads:
            for bb in range(2):
                scr[hh, bb] = scores(hh, jnp.minimum(2 * nn + bb, nb_last))

    def past_pair(hh, nn, scr, m_i, l_i, acc):
        n0 = jnp.minimum(2 * nn, nb_last)
        n1 = jnp.minimum(2 * nn + 1, nb_last)
        s0 = scr[hh, 0]
        s1 = scr[hh, 1]
        r0 = (sel_scr[hh, pl.ds(n0, 1), :] > 0.5) & (2 * nn < j)
        r1 = (sel_scr[hh, pl.ds(n1, 1), :] > 0.5) & (2 * nn + 1 < j)
        rm0 = jnp.where(r0, jnp.max(s0, axis=0, keepdims=True), NEG)
        rm1 = jnp.where(r1, jnp.max(s1, axis=0, keepdims=True), NEG)
        m_new = jnp.maximum(m_i, jnp.maximum(rm0, rm1))
        alpha = jnp.exp2(m_i - m_new)
        p0 = jnp.exp2(s0 - jnp.where(r0, m_new, -NEG))
        p1 = jnp.exp2(s1 - jnp.where(r1, m_new, -NEG))
        l_new = alpha * l_i + (jnp.sum(p0, axis=0, keepdims=True) + jnp.sum(p1, axis=0, keepdims=True))
        pv = (jnp.dot(vt_ref[0, hh, n0], p0.astype(BF16), preferred_element_type=F32)
              + jnp.dot(vt_ref[0, hh, n1], p1.astype(BF16), preferred_element_type=F32))
        return m_new, l_new, alpha * acc + pv

    def body(mm, state):
        score_trip(s_b, 2 * mm + 1)
        state = tuple(past_pair(hh, 2 * mm, s_a, *state[hh]) for hh in heads)
        score_trip(s_a, 2 * mm + 2)
        return tuple(past_pair(hh, 2 * mm + 1, s_b, *state[hh]) for hh in heads)

    init = tuple((jnp.full((1, blk), NEG, F32), jnp.zeros((1, blk), F32), jnp.zeros((dh, blk), F32))
                 for _ in heads)
    score_trip(s_a, 0)
    for hh in heads:
        s_own[hh] = scores(hh, j)
    n_trips = (j + 1) // 2
    carry = lax.fori_loop(0, (n_trips + 1) // 2, body, init)

    for hh in heads:
        m_i, l_i, acc = carry[hh]
        s = s_own[hh]
        kk = lax.broadcasted_iota(jnp.int32, s.shape, 0)
        qq = lax.broadcasted_iota(jnp.int32, s.shape, 1)
        s = jnp.where(kk <= qq, s, NEG)
        m_new = jnp.maximum(m_i, jnp.max(s, axis=0, keepdims=True))
        alpha = jnp.exp2(m_i - m_new)
        p = jnp.exp2(s - m_new)
        l_i = alpha * l_i + jnp.sum(p, axis=0, keepdims=True)
        acc = alpha * acc + jnp.dot(vt_ref[0, hh, j], p.astype(BF16), preferred_element_type=F32)
        o_ref[:, hh * dh:(hh + 1) * dh] = (acc / l_i).T.astype(o_ref.dtype)


def _moba(q, k, vt, km_pad, bsz, seq):
    h, dh, blk = N_HEADS, HEAD_DIM, MOBA_BLOCK
    nb = seq // blk
    nbp = km_pad.shape[2]
    nh = MOBA_HEADS_PER_STEP
    return pl.pallas_call(
        _moba_kernel,
        grid=(bsz, h // nh, nb),
        in_specs=[pl.BlockSpec((1, nh, blk, dh), lambda b, j, i: (b, j, i, 0)),
                  pl.BlockSpec((1, nh, seq, dh), lambda b, j, i: (b, j, 0, 0)),
                  pl.BlockSpec((1, nh, nb, dh, blk), lambda b, j, i: (b, j, 0, 0, 0)),
                  pl.BlockSpec((1, nh, nbp, dh), lambda b, j, i: (b, j, 0, 0))],
        out_specs=pl.BlockSpec((blk, nh * dh), lambda b, j, i: (b * nb + i, j)),
        out_shape=jax.ShapeDtypeStruct((bsz * seq, h * dh), BF16),
        scratch_shapes=[pltpu.VMEM((nh, nbp, blk), F32)] + [pltpu.VMEM((nh, 2, blk, blk), F32)] * 2
                       + [pltpu.VMEM((nh, blk, blk), F32)],
        compiler_params=_cparams(3, 48),
    )(q, k, vt, km_pad)


def _pos_kernel(oh_ref, ei_ref, dest_ref, meta_ref, cnt_scr, off_scr, run_scr):
    p = pl.program_id(0)
    i = pl.program_id(1)
    tm = oh_ref.shape[0]
    oh = oh_ref[...]
    colsum = jnp.sum(oh, axis=0, keepdims=True)

    @pl.when((p == 0) & (i == 0))
    def _():
        cnt_scr[...] = jnp.zeros_like(cnt_scr)

    @pl.when(p == 0)
    def _():
        cnt_scr[...] += colsum
        dest_ref[...] = jnp.zeros_like(dest_ref)

    @pl.when((p == 1) & (i == 0))
    def _():
        cnt = cnt_scr[...]
        padded = jnp.ceil(cnt * (1.0 / TM_EXPERT)) * float(TM_EXPERT)
        r = lax.broadcasted_iota(jnp.int32, (LANES, LANES), 0)
        cidx = lax.broadcasted_iota(jnp.int32, (LANES, LANES), 1)
        tri = (r < cidx).astype(F32)
        off = jnp.dot(jnp.broadcast_to(padded, (8, LANES)), tri,
                      preferred_element_type=F32, precision=HI)
        off_scr[...] = off[0:1]
        run_scr[...] = jnp.zeros_like(run_scr)

    @pl.when(p == 1)
    def _():
        r = lax.broadcasted_iota(jnp.int32, (tm, tm), 0)
        cidx = lax.broadcasted_iota(jnp.int32, (tm, tm), 1)
        ltri = (r > cidx).astype(BF16)
        excl = jnp.dot(ltri, oh.astype(BF16), preferred_element_type=F32)
        dfull = off_scr[...] + run_scr[...] + excl
        ei = ei_ref[...]
        lane = lax.broadcasted_iota(jnp.int32, oh.shape, 1)
        lane_f = lane.astype(F32)
        dest = jnp.zeros(oh.shape, F32)
        for kk in range(TOP_K):
            ek = jnp.sum(jnp.where(lane == kk, ei, 0.0), axis=-1, keepdims=True)
            dk = jnp.sum(jnp.where(lane_f == ek, dfull, 0.0), axis=-1, keepdims=True)
            dest = jnp.where(lane == kk, dk, dest)
        dest_ref[...] = dest.astype(jnp.int32)
        run_scr[...] += colsum

    meta_ref[...] = jnp.concatenate([off_scr[...], cnt_scr[...],
                                     jnp.zeros((6, LANES), F32)], axis=0)


def _positions(onehot, ei):
    t = onehot.shape[0]
    tm = TM_DENSE
    tile = pl.BlockSpec((tm, LANES), lambda p, i: (i, 0))
    return pl.pallas_call(
        _pos_kernel,
        grid=(2, t // tm),
        in_specs=[tile, tile],
        out_specs=[pl.BlockSpec((tm, LANES), lambda p, i: (i * p, 0)),
                   pl.BlockSpec((8, LANES), lambda p, i: (0, 0))],
        out_shape=[jax.ShapeDtypeStruct((t, LANES), jnp.int32),
                   jax.ShapeDtypeStruct((8, LANES), F32)],
        scratch_shapes=[pltpu.VMEM((1, LANES), F32)] * 3,
        compiler_params=_cparams(2, 16),
    )(onehot, ei)


def _src_kernel(dest_ref, pad_lo_ref, pad_hi_ref, src_ref):
    i = pl.program_id(0)
    tb = dest_ref.shape[0] // TOP_K

    @pl.when(i == 0)
    def _():
        def zero(j, carry):
            src_ref[j] = 0
            return carry

        def per_range(rr, carry):
            lax.fori_loop(pad_lo_ref[rr], pad_hi_ref[rr], zero, 0)
            return carry
        lax.fori_loop(0, pad_lo_ref.shape[0], per_range, 0)

    def body(tt, carry):
        for kk in range(TOP_K):
            src_ref[dest_ref[tt * TOP_K + kk]] = i * tb + tt
        return carry

    lax.fori_loop(0, tb, body, 0, unroll=8)


def _sources(dest_flat, pad_lo, pad_hi, n_rows):
    n = dest_flat.shape[0]
    tb = TM_SRC
    whole = pl.BlockSpec(memory_space=pltpu.SMEM)
    return pl.pallas_call(
        _src_kernel,
        grid=(n // (tb * TOP_K),),
        in_specs=[pl.BlockSpec((tb * TOP_K,), lambda i: (i,), memory_space=pltpu.SMEM), whole, whole],
        out_specs=pl.BlockSpec(memory_space=pltpu.SMEM),
        out_shape=jax.ShapeDtypeStruct((n_rows,), jnp.int32),
        compiler_params=_cparams(1, 16),
    )(dest_flat, pad_lo, pad_hi)


def _expert_kernel(te_ref, tv_ref, nx_ref, src_cur, src_nxt, h_hbm, wgu_hbm, bgu_ref, wd_hbm, bd_ref,
                   o_ref, xbuf, sem, wgu_st, wd_st, wsem, wgu_bf, wd_bf):
    i = pl.program_id(0)
    nt = pl.num_programs(0)
    tm = o_ref.shape[0]
    slot = lax.rem(i, 2)
    e = te_ref[i]
    prev = te_ref[jnp.maximum(i - 1, 0)]
    dff = wd_hbm.shape[1]

    def weight_copies(ex):
        return (pltpu.make_async_copy(wgu_hbm.at[ex], wgu_st, wsem.at[0]),
                pltpu.make_async_copy(wd_hbm.at[ex], wd_st, wsem.at[1]))

    def gather(src_ref, s):
        def body(r8, carry):
            for uu in range(8):
                pltpu.make_async_copy(h_hbm.at[pl.ds(src_ref[r8 * 8 + uu], 1)],
                                      xbuf.at[s, r8, pl.ds(uu, 1)], sem.at[s]).start()
            return carry
        lax.fori_loop(0, tm // 8, body, 0)

    @pl.when(i == 0)
    def _():
        for cp in weight_copies(e):
            cp.start(priority=1)

    @pl.when((i == 0) & (tv_ref[0] > 0))
    def _():
        gather(src_cur, 0)

    nxt = jnp.minimum(i + 1, nt - 1)

    @pl.when((i + 1 < nt) & (tv_ref[nxt] > 0))
    def _():
        gather(src_nxt, 1 - slot)

    @pl.when((i == 0) | (e != prev))
    def _():
        for cp in weight_copies(e):
            cp.wait()
        wgu_bf[...] = wgu_st[...].astype(BF16)
        wd_bf[...] = wd_st[...].astype(BF16)

        @pl.when(nx_ref[i] >= 0)
        def _():
            for cp in weight_copies(nx_ref[i]):
                cp.start(priority=1)

    @pl.when(tv_ref[i] > 0)
    def _():
        for r8 in range(tm // 8):
            pltpu.make_async_copy(h_hbm.at[pl.ds(0, 8)], xbuf.at[slot, r8], sem.at[slot]).wait()
        x = _unpack_rows(xbuf[slot].reshape(tm, xbuf.shape[3])).astype(BF16)
        hu = jnp.dot(x, wgu_bf[...], preferred_element_type=F32) + bgu_ref[0]
        g = jnp.minimum(hu[:, :dff], SWIGLU_LIMIT)
        up = jnp.clip(hu[:, dff:], -SWIGLU_LIMIT, SWIGLU_LIMIT)
        act = (up + 1.0) * g * jax.nn.sigmoid(SWIGLU_ALPHA * g)
        y = jnp.dot(act.astype(BF16), wd_bf[...], preferred_element_type=F32) + bd_ref[0]
        o_ref[...] = _pack_rows(y)

    @pl.when(tv_ref[i] == 0)
    def _():
        o_ref[...] = jnp.zeros_like(o_ref)


def _experts(tile_e, tile_v, next_e, src, h2p, w_gu, b_gu, w_d, b_d):
    n_rows = src.shape[0]
    dp = h2p.shape[1]
    ne, d, f2 = w_gu.shape
    dff = w_d.shape[1]
    tm = TM_EXPERT
    nt = n_rows // tm
    grid_spec = pltpu.PrefetchScalarGridSpec(
        num_scalar_prefetch=3,
        grid=(nt,),
        in_specs=[pl.BlockSpec((tm,), lambda i, te, tv, nx: (i,), memory_space=pltpu.SMEM),
                  pl.BlockSpec((tm,), lambda i, te, tv, nx: (jnp.minimum(i + 1, nt - 1),),
                               memory_space=pltpu.SMEM),
                  pl.BlockSpec(memory_space=pl.ANY),
                  pl.BlockSpec(memory_space=pl.ANY),
                  pl.BlockSpec((1, 1, f2), lambda i, te, tv, nx: (te[i], 0, 0)),
                  pl.BlockSpec(memory_space=pl.ANY),
                  pl.BlockSpec((1, 1, d), lambda i, te, tv, nx: (te[i], 0, 0))],
        out_specs=pl.BlockSpec((tm, dp), lambda i, te, tv, nx: (i, 0)),
        scratch_shapes=[pltpu.VMEM((2, tm // 8, 8, dp), jnp.uint32), pltpu.SemaphoreType.DMA((2,)),
                        pltpu.VMEM((d, f2), F32), pltpu.VMEM((dff, d), F32),
                        pltpu.SemaphoreType.DMA((2,)),
                        pltpu.VMEM((d, f2), BF16), pltpu.VMEM((dff, d), BF16)],
    )
    return pl.pallas_call(
        _expert_kernel,
        grid_spec=grid_spec,
        out_shape=jax.ShapeDtypeStruct((n_rows, dp), jnp.uint32),
        compiler_params=_cparams(1, 48),
    )(tile_e, tile_v, next_e, src, src, h2p, w_gu, b_gu.reshape(ne, 1, f2), w_d, b_d.reshape(ne, 1, d))


def _combine_kernel(dcur, dnxt, gt_ref, x_ref, g_ref, gf_ref, ys_hbm, o_ref, buf, sem, *, final):
    i = pl.program_id(0)
    nt = pl.num_programs(0)
    td = x_ref.shape[0]
    slot = lax.rem(i, 2)

    def gather(dref, s):
        def body(t8, carry):
            for uu in range(8):
                for kk in range(TOP_K):
                    pltpu.make_async_copy(ys_hbm.at[pl.ds(dref[(t8 * 8 + uu) * TOP_K + kk], 1)],
                                          buf.at[s, kk, t8, pl.ds(uu, 1)], sem.at[s]).start()
            return carry
        lax.fori_loop(0, td // 8, body, 0)

    @pl.when(i == 0)
    def _():
        gather(dcur, 0)

    @pl.when(i + 1 < nt)
    def _():
        gather(dnxt, 1 - slot)

    for kk in range(TOP_K):
        for t8 in range(td // 8):
            pltpu.make_async_copy(ys_hbm.at[pl.ds(0, 8)], buf.at[slot, kk, t8], sem.at[slot]).wait()
    gt = gt_ref[...]
    rows = lambda kk: _unpack_rows(buf[slot, kk].reshape(td, buf.shape[4]))
    y = gt[:, 0:1] * rows(0)
    for kk in range(1, TOP_K):
        y = y + gt[:, kk:kk + 1] * rows(kk)
    xn = x_ref[...] + g_ref[0] * y
    if final:
        ms = jnp.mean(xn * xn, axis=-1, keepdims=True)
        xn = (xn * lax.rsqrt(ms + EPS)) * gf_ref[...]
    o_ref[...] = xn


def _combine(dest_flat, gates, x2, gate_vec, gf, ys, seq, final):
    t, d = x2.shape
    td = TD_ROWS
    tpb = seq // td
    nt = t // td
    return pl.pallas_call(
        functools.partial(_combine_kernel, final=final),
        grid=(nt,),
        in_specs=[pl.BlockSpec((td * TOP_K,), lambda i: (i,), memory_space=pltpu.SMEM),
                  pl.BlockSpec((td * TOP_K,), lambda i: (jnp.minimum(i + 1, nt - 1),),
                               memory_space=pltpu.SMEM),
                  pl.BlockSpec((td, LANES), lambda i: (i, 0)),
                  pl.BlockSpec((td, d), lambda i: (i, 0)),
                  pl.BlockSpec((1, 1, d), lambda i: (i // tpb, 0, 0)),
                  pl.BlockSpec((1, d), lambda i: (0, 0)),
                  pl.BlockSpec(memory_space=pl.ANY)],
        out_specs=pl.BlockSpec((td, d), lambda i: (i, 0)),
        out_shape=jax.ShapeDtypeStruct((t, d), F32),
        scratch_shapes=[pltpu.VMEM((2, TOP_K, td // 8, 8, d // 2), jnp.uint32),
                        pltpu.SemaphoreType.DMA((2,))],
        compiler_params=_cparams(1, 32),
    )(dest_flat, dest_flat, gates, x2, gate_vec, gf.reshape(1, d), ys)


def _moe(x2, routed, gate_vec, w_gu, b_gu, w_d, b_d, gf, seq, final):
    t, d = x2.shape
    ne = w_gu.shape[0]
    h2, onehot, ei, gates = routed
    dest, meta = _positions(onehot, ei)
    dest_flat = dest[:, :TOP_K].reshape(t * TOP_K)

    tm = TM_EXPERT
    n_rows = t * TOP_K + ne * tm
    n_tiles = n_rows // tm
    off = meta[0, :ne].astype(jnp.int32)
    cnt = meta[1, :ne].astype(jnp.int32)
    ends = off + ((cnt + tm - 1) // tm) * tm
    starts = jnp.arange(n_tiles, dtype=jnp.int32) * tm
    valid = starts < ends[-1]
    te = jnp.sum((starts[:, None] >= ends[None, :]).astype(jnp.int32), axis=1)
    te_last = jnp.sum((ends[-1] - 1 >= ends).astype(jnp.int32))
    tile_e = jnp.minimum(jnp.where(valid, te, te_last), ne - 1)
    tile_v = valid.astype(jnp.int32)
    eidx = jnp.arange(ne, dtype=jnp.int32)
    later_used = (eidx[None, :] > eidx[:, None]) & (cnt[None, :] > 0)
    next_of = jnp.min(jnp.where(later_used, eidx[None, :], ne), axis=1)
    next_of = jnp.where(next_of == ne, -1, next_of)
    next_e = jnp.sum(jnp.where(tile_e[:, None] == eidx[None, :], next_of[None, :], 0), axis=1)

    pad_lo = jnp.concatenate([off + cnt, ends[-1:]]).astype(jnp.int32)
    pad_hi = jnp.concatenate([ends, jnp.full((1,), n_rows, jnp.int32)]).astype(jnp.int32)
    src = _sources(dest_flat, pad_lo, pad_hi, n_rows)
    ys = _experts(tile_e, tile_v, next_e.astype(jnp.int32), src, h2, w_gu, b_gu, w_d, b_d)
    return _combine(dest_flat, gates, x2, gate_vec, gf, ys, seq, final)


def kernel(x, c, positions, l0_norm1_g, l0_ada_w, l0_ada_b, l0_s5_w_in, l0_s5_b_re, l0_s5_b_im, l0_s5_c_re, l0_s5_c_im, l0_s5_lam_re, l0_s5_lam_im, l0_s5_log_dt, l0_s5_d, l0_s5_w_glu, l0_s5_w_out, l0_norm2_g, l0_moe_w_router, l0_moe_b_router, l0_moe_w_gate_up, l0_moe_b_gate_up, l0_moe_w_down, l0_moe_b_down, l1_norm1_g, l1_ada_w, l1_ada_b, l1_moba_w_qkv, l1_moba_w_o, l1_norm2_g, l1_moe_w_router, l1_moe_b_router, l1_moe_w_gate_up, l1_moe_b_gate_up, l1_moe_w_down, l1_moe_b_down, final_norm_g):
    bsz, seq, d = x.shape
    t = bsz * seq
    assert seq % TM_DENSE == 0 and seq % MOBA_BLOCK == 0 and seq % S5_CHUNK == 0
    assert S5_CHUNK % S5_SEG == 0 and seq % TD_ROWS == 0
    assert seq // MOBA_BLOCK <= LANES and d == N_HEADS * HEAD_DIM
    x2 = x.reshape(t, d)
    c_pad = jnp.zeros((8, d), F32).at[:bsz].set(c)

    def ada_parts(w, b):
        a = _ada(c_pad, w, b)[:bsz]
        return [a[:, i * d:(i + 1) * d].reshape(bsz, 1, d) for i in range(6)]

    sh1, sc1, g1, sh2, sc2, g2 = ada_parts(l0_ada_w, l0_ada_b)
    u = _nml(x2, l0_norm1_g, sc1, sh1, l0_s5_w_in.astype(BF16), seq, F32)
    rate, theta, bblk, cblk = _s5_params(l0_s5_b_re, l0_s5_b_im, l0_s5_c_re, l0_s5_c_im,
                                         l0_s5_lam_re, l0_s5_lam_im, l0_s5_log_dt)
    z = _s5_scan(u, rate, theta, bblk, cblk, l0_s5_d, bsz, seq)
    x2 = _row_call(_glu_out_kernel, z, x2, g1,
                   [l0_s5_w_glu.astype(BF16), l0_s5_w_out.astype(BF16)], seq)
    routed = _router(x2, l0_norm2_g, sc2, sh2, l0_moe_w_router, l0_moe_b_router, seq)
    x2 = _moe(x2, routed, g2, l0_moe_w_gate_up, l0_moe_b_gate_up, l0_moe_w_down, l0_moe_b_down,
              final_norm_g, seq, final=False)

    sh1, sc1, g1, sh2, sc2, g2 = ada_parts(l1_ada_w, l1_ada_b)
    qkv = _nml(x2, l1_norm1_g, sc1, sh1, l1_moba_w_qkv.astype(BF16), seq, BF16)
    cos, sin = _rope_tables(positions.reshape(t, 1))
    q, k, v, km = _rope(qkv, cos, sin, bsz, seq)
    nb = seq // MOBA_BLOCK
    nbp = -(-nb // 8) * 8
    km_pad = jnp.zeros((bsz, N_HEADS, nbp, HEAD_DIM), F32).at[:, :, :nb].set(km[:, :, :, 0])
    o = _moba(q, k, v, km_pad, bsz, seq)
    x2 = _row_call(_lin_res_kernel, o, x2, g1, [l1_moba_w_o.astype(BF16)], seq)
    routed = _router(x2, l1_norm2_g, sc2, sh2, l1_moe_w_router, l1_moe_b_router, seq)
    x2 = _moe(x2, routed, g2, l1_moe_w_gate_up, l1_moe_b_gate_up, l1_moe_w_down, l1_moe_b_down,
              final_norm_g, seq, final=True)
    return x2.reshape(bsz, seq, d)
```

```python
import functools
import math

import jax
import jax.numpy as jnp
from jax import lax
from jax.experimental import pallas as pl
from jax.experimental.pallas import tpu as pltpu

F32 = jnp.float32
BF16 = jnp.bfloat16
HI = lax.Precision.HIGHEST

EPS = 1e-6
NEG = -1e30
LOG2E = math.log2(math.e)
LANES = 128
MIB = 1024 * 1024

S5_GROUP = 16
S5_STATE = 64
N_HEADS = 8
HEAD_DIM = 128
MOBA_BLOCK = 256
MOBA_TOPK = 3
ROPE_THETA = 10000.0
N_EXPERTS = 32
TOP_K = 4
SWIGLU_LIMIT = 7.0
SWIGLU_ALPHA = 1.702

TM_DENSE = 512
S5_CHUNK = 256
S5_SEG = 32
S5_SUPER = 8
MOBA_HEADS_PER_STEP = 2
TM_EXPERT = 256
TD_ROWS = 256
TM_SRC = 2048


def _cparams(n_axes, vmem_mib):
    return pltpu.CompilerParams(dimension_semantics=("arbitrary",) * n_axes,
                                vmem_limit_bytes=vmem_mib * MIB)


def _pack_rows(x):
    n = x.shape[1] // 2
    xb = x.astype(BF16).astype(F32)
    lo = lax.bitcast_convert_type(xb[:, :n], jnp.uint32) >> 16
    hi = lax.bitcast_convert_type(xb[:, n:], jnp.uint32) & jnp.uint32(0xFFFF0000)
    return lo | hi


def _unpack_rows(p):
    lo = lax.bitcast_convert_type(p << 16, F32)
    hi = lax.bitcast_convert_type(p & jnp.uint32(0xFFFF0000), F32)
    return jnp.concatenate([lo, hi], axis=1)


def _normmod(x, g, sc, sh):
    ms = jnp.mean(x * x, axis=-1, keepdims=True)
    return (x * lax.rsqrt(ms + EPS)) * g * (1.0 + sc) + sh


def _ada_kernel(c_ref, w_ref, b_ref, o_ref):
    c = c_ref[...]
    ca = c * jax.nn.sigmoid(c)
    o_ref[...] = jnp.dot(ca, w_ref[...], preferred_element_type=F32, precision=HI) + b_ref[...]


def _ada(c_pad, w, b):
    d, n = w.shape
    tn = 1536
    return pl.pallas_call(
        _ada_kernel,
        grid=(n // tn,),
        in_specs=[pl.BlockSpec((8, d), lambda j: (0, 0)),
                  pl.BlockSpec((d, tn), lambda j: (0, j)),
                  pl.BlockSpec((1, tn), lambda j: (0, j))],
        out_specs=pl.BlockSpec((8, tn), lambda j: (0, j)),
        out_shape=jax.ShapeDtypeStruct((8, n), F32),
        compiler_params=_cparams(1, 32),
    )(c_pad, w, b.reshape(1, n))


def _nml_kernel(x_ref, g_ref, sc_ref, sh_ref, w_ref, o_ref):
    h = _normmod(x_ref[...], g_ref[...], sc_ref[0], sh_ref[0])
    o_ref[...] = jnp.dot(h.astype(BF16), w_ref[...], preferred_element_type=F32).astype(o_ref.dtype)


def _nml(x2, g, sc, sh, w_bf, seq, out_dtype):
    t, d = x2.shape
    n = w_bf.shape[1]
    tm = TM_DENSE
    tpb = seq // tm
    return pl.pallas_call(
        _nml_kernel,
        grid=(t // tm,),
        in_specs=[pl.BlockSpec((tm, d), lambda i: (i, 0)),
                  pl.BlockSpec((1, d), lambda i: (0, 0)),
                  pl.BlockSpec((1, 1, d), lambda i: (i // tpb, 0, 0)),
                  pl.BlockSpec((1, 1, d), lambda i: (i // tpb, 0, 0)),
                  pl.BlockSpec((d, n), lambda i: (0, 0))],
        out_specs=pl.BlockSpec((tm, n), lambda i: (i, 0)),
        out_shape=jax.ShapeDtypeStruct((t, n), out_dtype),
        compiler_params=_cparams(1, 48),
    )(x2, g.reshape(1, d), sc, sh, w_bf)


def _gelu_tanh(y):
    c = math.sqrt(2.0 / math.pi)
    return y * (0.5 * (1.0 + jnp.tanh(c * (y + 0.044715 * (y * y * y)))))


def _s5_kernel(u_ref, rate_ref, theta_ref, bblk_ref, cblk_ref, d_ref, z_ref,
               apr, api, a1r, a1i, air, aii, xr, xi):
    L = u_ref.shape[0]
    seg = apr.shape[0]
    n_super = bblk_ref.shape[0]
    half = apr.shape[1] // n_super
    nch = u_ref.shape[1] // n_super
    b = pl.program_id(0)
    c = pl.program_id(1)

    @pl.when((b == 0) & (c == 0))
    def _():
        tt = lax.broadcasted_iota(jnp.int32, (seg, 1), 0).astype(F32)
        ph = theta_ref[...] * tt
        rt = rate_ref[...] * tt
        cs = jnp.cos(ph)
        sn = jnp.sin(ph)
        mag = jnp.exp(rt)
        inv = jnp.exp(-rt)
        apr[...] = mag * cs
        api[...] = mag * sn
        air[...] = inv * cs
        aii[...] = -(inv * sn)
        ph1 = theta_ref[...] * (tt + 1.0)
        mag1 = jnp.exp(rate_ref[...] * (tt + 1.0))
        a1r[...] = mag1 * jnp.cos(ph1)
        a1i[...] = mag1 * jnp.sin(ph1)

    @pl.when(c == 0)
    def _():
        xr[...] = jnp.zeros_like(xr)
        xi[...] = jnp.zeros_like(xi)

    row = lax.broadcasted_iota(jnp.int32, (L, L), 0)
    col = lax.broadcasted_iota(jnp.int32, (L, L), 1)
    tril = ((row >= col) & (row // seg == col // seg)).astype(BF16)

    for sg in range(n_super):
        cols = slice(sg * half, (sg + 1) * half)
        ch = slice(sg * nch, (sg + 1) * nch)
        u = u_ref[:, ch]
        bu = jnp.dot(u.astype(BF16), bblk_ref[sg], preferred_element_type=F32)
        ir = air[:, cols]
        ii = aii[:, cols]
        zs = []
        for k in range(L // seg):
            bur = bu[k * seg:(k + 1) * seg, :half]
            bui = bu[k * seg:(k + 1) * seg, half:]
            zs.append(jnp.concatenate([bur * ir - bui * ii, bur * ii + bui * ir], axis=1))
        zc = jnp.concatenate(zs, axis=0).astype(BF16)
        cum = jnp.dot(tril, zc, preferred_element_type=F32)
        pw_r = apr[:, cols]
        pw_i = api[:, cols]
        p1_r = a1r[:, cols]
        p1_i = a1i[:, cols]
        last_r = xr[:, cols]
        last_i = xi[:, cols]
        xs = []
        for k in range(L // seg):
            cr = cum[k * seg:(k + 1) * seg, :half]
            ci = cum[k * seg:(k + 1) * seg, half:]
            x_re = cr * pw_r - ci * pw_i + (p1_r * last_r - p1_i * last_i)
            x_im = cr * pw_i + ci * pw_r + (p1_r * last_i + p1_i * last_r)
            last_r = x_re[seg - 1:seg, :]
            last_i = x_im[seg - 1:seg, :]
            xs.append(jnp.concatenate([x_re, x_im], axis=1))
        xr[:, cols] = last_r
        xi[:, cols] = last_i
        xc = jnp.concatenate(xs, axis=0).astype(BF16)
        y = jnp.dot(xc, cblk_ref[sg], preferred_element_type=F32) + d_ref[:, ch] * u
        z_ref[:, ch] = _gelu_tanh(y)


def _s5_scan(u, rate, theta, bblk, cblk, d_skip, bsz, seq):
    t, w = u.shape
    L = S5_CHUNK
    nc = seq // L
    ns = rate.shape[1]
    return pl.pallas_call(
        _s5_kernel,
        grid=(bsz, nc),
        in_specs=[pl.BlockSpec((L, w), lambda b, c: (b * nc + c, 0)),
                  pl.BlockSpec((1, ns), lambda b, c: (0, 0)),
                  pl.BlockSpec((1, ns), lambda b, c: (0, 0)),
                  pl.BlockSpec(bblk.shape, lambda b, c: (0, 0, 0)),
                  pl.BlockSpec(cblk.shape, lambda b, c: (0, 0, 0)),
                  pl.BlockSpec((1, w), lambda b, c: (0, 0))],
        out_specs=pl.BlockSpec((L, w), lambda b, c: (b * nc + c, 0)),
        out_shape=jax.ShapeDtypeStruct((t, w), F32),
        scratch_shapes=[pltpu.VMEM((S5_SEG, ns), F32)] * 6 + [pltpu.VMEM((1, ns), F32)] * 2,
        compiler_params=_cparams(2, 48),
    )(u, rate, theta, bblk, cblk, d_skip.reshape(1, w))


def _s5_params(b_re, b_im, c_re, c_im, lam_re, lam_im, log_dt):
    g, p, ch = b_re.shape
    dt = jnp.exp(log_dt)[:, None]
    rate = lam_re * dt
    theta = lam_im * dt
    mag = jnp.exp(rate)
    a_re = mag * jnp.cos(theta)
    a_im = mag * jnp.sin(theta)
    den = lam_re * lam_re + lam_im * lam_im
    f_re = ((a_re - 1.0) * lam_re + a_im * lam_im) / den
    f_im = (a_im * lam_re - (a_re - 1.0) * lam_im) / den
    bb_re = f_re[..., None] * b_re - f_im[..., None] * b_im
    bb_im = f_re[..., None] * b_im + f_im[..., None] * b_re
    ns = S5_SUPER
    nsg = g // ns
    eye = jnp.eye(ns, dtype=F32)

    def blk_b(bb):
        return jnp.einsum('sgpc,gh->sgchp', bb.reshape(nsg, ns, p, ch), eye).reshape(nsg, ns * ch, ns * p)

    def blk_c(cc):
        return jnp.einsum('sgcp,gh->sgphc', cc.reshape(nsg, ns, ch, p), eye).reshape(nsg, ns * p, ns * ch)

    bblk = jnp.concatenate([blk_b(bb_re), blk_b(bb_im)], axis=2).astype(BF16)
    cblk = jnp.concatenate([blk_c(c_re), -blk_c(c_im)], axis=1).astype(BF16)
    return rate.reshape(1, g * p), theta.reshape(1, g * p), bblk, cblk


def _route(x, g, sc, sh, wr, br):
    h = _normmod(x, g, sc, sh).astype(BF16)
    logits = jnp.dot(h, wr, preferred_element_type=F32) + br
    lane = lax.broadcasted_iota(jnp.int32, logits.shape, 1)
    lane_f = lane.astype(F32)
    work = logits
    onehot = jnp.zeros(logits.shape, F32)
    ei = jnp.zeros(logits.shape, F32)
    vals = []
    for kk in range(TOP_K):
        m = jnp.max(work, axis=-1, keepdims=True)
        idx = jnp.min(jnp.where(work == m, lane_f, float(LANES)), axis=-1, keepdims=True)
        hit = lane_f == idx
        onehot = jnp.where(hit, 1.0, onehot)
        ei = jnp.where(lane == kk, idx, ei)
        vals.append(m)
        work = jnp.where(hit, 2.0 * NEG, work)
    ex = [jnp.exp(vv - vals[0]) for vv in vals]
    tot = ex[0] + ex[1] + ex[2] + ex[3]
    gt = jnp.zeros(logits.shape, F32)
    for kk in range(TOP_K):
        gt = jnp.where(lane == kk, ex[kk] / tot, gt)
    return _pack_rows(h), onehot, ei, gt


def _glu_out_kernel(z_ref, x_ref, g_ref, wg_ref, wo_ref, o_ref):
    z = z_ref[...]
    gate = jax.nn.sigmoid(jnp.dot(z.astype(BF16), wg_ref[...], preferred_element_type=F32))
    m = jnp.dot((z * gate).astype(BF16), wo_ref[...], preferred_element_type=F32)
    o_ref[...] = x_ref[...] + g_ref[0] * m


def _lin_res_kernel(a_ref, x_ref, g_ref, w_ref, o_ref):
    m = jnp.dot(a_ref[...].astype(BF16), w_ref[...], preferred_element_type=F32)
    o_ref[...] = x_ref[...] + g_ref[0] * m


def _row_call(kernel, acts, x2, gate, weights, seq):
    t, d = x2.shape
    tm = TM_DENSE
    tpb = seq // tm
    row = pl.BlockSpec((tm, d), lambda i: (i, 0))
    return pl.pallas_call(
        kernel,
        grid=(t // tm,),
        in_specs=[row, row, pl.BlockSpec((1, 1, d), lambda i: (i // tpb, 0, 0))]
                 + [pl.BlockSpec(w.shape, lambda i: (0, 0)) for w in weights],
        out_specs=row,
        out_shape=jax.ShapeDtypeStruct((t, d), F32),
        compiler_params=_cparams(1, 48),
    )(acts, x2, gate, *weights)


def _router_kernel(x_ref, g_ref, sc_ref, sh_ref, wr_ref, br_ref, h_ref, oh_ref, ei_ref, gt_ref):
    h_ref[...], oh_ref[...], ei_ref[...], gt_ref[...] = _route(
        x_ref[...], g_ref[...], sc_ref[0], sh_ref[0], wr_ref[...], br_ref[...])


def _router(x2, g, sc, sh, w_router, b_router, seq):
    t, d = x2.shape
    ne = w_router.shape[1]
    wr_pad = jnp.zeros((d, LANES), F32).at[:, :ne].set(w_router).astype(BF16)
    br_pad = jnp.full((1, LANES), NEG, F32).at[0, :ne].set(b_router)
    tm = TM_DENSE
    tpb = seq // tm
    lane_tile = pl.BlockSpec((tm, LANES), lambda i: (i, 0))
    per_batch = pl.BlockSpec((1, 1, d), lambda i: (i // tpb, 0, 0))
    return pl.pallas_call(
        _router_kernel,
        grid=(t // tm,),
        in_specs=[pl.BlockSpec((tm, d), lambda i: (i, 0)),
                  pl.BlockSpec((1, d), lambda i: (0, 0)), per_batch, per_batch,
                  pl.BlockSpec((d, LANES), lambda i: (0, 0)),
                  pl.BlockSpec((1, LANES), lambda i: (0, 0))],
        out_specs=[pl.BlockSpec((tm, d // 2), lambda i: (i, 0)), lane_tile, lane_tile, lane_tile],
        out_shape=[jax.ShapeDtypeStruct((t, d // 2), jnp.uint32)]
                  + [jax.ShapeDtypeStruct((t, LANES), F32)] * 3,
        compiler_params=_cparams(1, 32),
    )(x2, g.reshape(1, d), sc, sh, wr_pad, br_pad)


def _rope_tab_kernel(pos_ref, inv_ref, sgn_ref, cos_ref, sin_ref):
    ang = pos_ref[...].astype(F32) * inv_ref[...]
    cos_ref[...] = jnp.cos(ang)
    sin_ref[...] = jnp.sin(ang) * sgn_ref[...]


def _rope_tables(pos_col):
    t = pos_col.shape[0]
    ts = 512
    half = HEAD_DIM // 2
    inv = ROPE_THETA ** (-jnp.arange(0, HEAD_DIM, 2, dtype=F32) / HEAD_DIM)
    inv = jnp.concatenate([inv, inv]).reshape(1, HEAD_DIM)
    sgn = jnp.concatenate([-jnp.ones((half,), F32), jnp.ones((half,), F32)]).reshape(1, HEAD_DIM)
    const = pl.BlockSpec((1, HEAD_DIM), lambda i: (0, 0))
    tile = pl.BlockSpec((ts, HEAD_DIM), lambda i: (i, 0))
    return pl.pallas_call(
        _rope_tab_kernel,
        grid=(t // ts,),
        in_specs=[pl.BlockSpec((ts, 1), lambda i: (i, 0)), const, const],
        out_specs=[tile, tile],
        out_shape=[jax.ShapeDtypeStruct((t, HEAD_DIM), F32)] * 2,
        compiler_params=_cparams(1, 16),
    )(pos_col, inv, sgn)


def _rope_kernel(q_ref, k_ref, v_ref, cos_ref, sin_ref, qo_ref, ko_ref, vo_ref, km_ref):
    cs = cos_ref[...]
    sn = sin_ref[...]
    dh = cs.shape[1]
    half = dh // 2
    for hh in range(qo_ref.shape[1]):
        cols = slice(hh * dh, (hh + 1) * dh)
        q = q_ref[:, cols].astype(F32)
        k = k_ref[:, cols].astype(F32)
        qr = q * cs + pltpu.roll(q, half, 1) * sn
        kr = k * cs + pltpu.roll(k, half, 1) * sn
        qo_ref[0, hh] = qr
        ko_ref[0, hh] = kr.astype(BF16)
        vo_ref[0, hh, 0] = v_ref[:, cols].astype(F32).T.astype(BF16)
        km_ref[0, hh, 0] = jnp.mean(kr, axis=0, keepdims=True)


def _rope(qkv, cos, sin, bsz, seq):
    t = qkv.shape[0]
    h, dh, blk = N_HEADS, HEAD_DIM, MOBA_BLOCK
    nb = seq // blk
    part = lambda p: pl.BlockSpec((blk, h * dh), lambda i: (i, p))
    tab = pl.BlockSpec((blk, dh), lambda i: (i, 0))
    head = pl.BlockSpec((1, h, blk, dh), lambda i: (i // nb, 0, i % nb, 0))
    return pl.pallas_call(
        _rope_kernel,
        grid=(t // blk,),
        in_specs=[part(0), part(1), part(2), tab, tab],
        out_specs=[head, head,
                   pl.BlockSpec((1, h, 1, dh, blk), lambda i: (i // nb, 0, i % nb, 0, 0)),
                   pl.BlockSpec((1, h, 1, 1, dh), lambda i: (i // nb, 0, i % nb, 0, 0))],
        out_shape=[jax.ShapeDtypeStruct((bsz, h, seq, dh), F32),
                   jax.ShapeDtypeStruct((bsz, h, seq, dh), BF16),
                   jax.ShapeDtypeStruct((bsz, h, nb, dh, blk), BF16),
                   jax.ShapeDtypeStruct((bsz, h, nb, 1, dh), F32)],
        compiler_params=_cparams(1, 32),
    )(qkv, qkv, qkv, cos, sin)


def _moba_kernel(q_ref, k_ref, vt_ref, km_ref, o_ref, sel_scr, s_a, s_b, s_own):
    nh = q_ref.shape[1]
    blk = q_ref.shape[2]
    dh = q_ref.shape[3]
    j = pl.program_id(2)
    nt = (((1,), (1,)), ((), ()))
    heads = range(nh)

    qs = []
    for hh in heads:
        q = q_ref[0, hh]
        gate = lax.dot_general(km_ref[0, hh], q, nt, preferred_element_type=F32, precision=HI)
        bidx = lax.broadcasted_iota(jnp.int32, gate.shape, 0)
        bidx_f = bidx.astype(F32)
        work = jnp.where(bidx < j, gate, NEG)
        sel = jnp.zeros(gate.shape, F32)
        for _ in range(MOBA_TOPK):
            m = jnp.max(work, axis=0, keepdims=True)
            idx = jnp.min(jnp.where(work == m, bidx_f, float(LANES)), axis=0, keepdims=True)
            hit = bidx_f == idx
            sel = jnp.where(hit & (m > 0.5 * NEG), 1.0, sel)
            work = jnp.where(hit, 2.0 * NEG, work)
        sel_scr[hh] = sel
        qs.append((q * (dh ** -0.5 * LOG2E)).astype(BF16))

    def scores(hh, n):
        start = pl.multiple_of(n * blk, blk)
        kb = k_ref[0, hh, pl.ds(start, blk), :]
        return lax.dot_general(kb, qs[hh], nt, preferred_element_type=F32)

    nb_last = vt_ref.shape[2] - 1

    def score_trip(scr, nn):
        for hh in heads:
            for bb in range(2):
                scr[hh, bb] = scores(hh, jnp.minimum(2 * nn + bb, nb_last))

    def past_pair(hh, nn, scr, m_i, l_i, acc):
        n0 = jnp.minimum(2 * nn, nb_last)
        n1 = jnp.minimum(2 * nn + 1, nb_last)
        s0 = scr[hh, 0]
        s1 = scr[hh, 1]
        r0 = (sel_scr[hh, pl.ds(n0, 1), :] > 0.5) & (2 * nn < j)
        r1 = (sel_scr[hh, pl.ds(n1, 1), :] > 0.5) & (2 * nn + 1 < j)
        rm0 = jnp.where(r0, jnp.max(s0, axis=0, keepdims=True), NEG)
        rm1 = jnp.where(r1, jnp.max(s1, axis=0, keepdims=True), NEG)
        m_new = jnp.maximum(m_i, jnp.maximum(rm0, rm1))
        alpha = jnp.exp2(m_i - m_new)
        p0 = jnp.exp2(s0 - jnp.where(r0, m_new, -NEG))
        p1 = jnp.exp2(s1 - jnp.where(r1, m_new, -NEG))
        l_new = alpha * l_i + (jnp.sum(p0, axis=0, keepdims=True) + jnp.sum(p1, axis=0, keepdims=True))
        pv = (jnp.dot(vt_ref[0, hh, n0], p0.astype(BF16), preferred_element_type=F32)
              + jnp.dot(vt_ref[0, hh, n1], p1.astype(BF16), preferred_element_type=F32))
        return m_new, l_new, alpha * acc + pv

    def body(mm, state):
        score_trip(s_b, 2 * mm + 1)
        state = tuple(past_pair(hh, 2 * mm, s_a, *state[hh]) for hh in heads)
        score_trip(s_a, 2 * mm + 2)
        return tuple(past_pair(hh, 2 * mm + 1, s_b, *state[hh]) for hh in heads)

    init = tuple((jnp.full((1, blk), NEG, F32), jnp.zeros((1, blk), F32), jnp.zeros((dh, blk), F32))
                 for _ in heads)
    score_trip(s_a, 0)
    for hh in heads:
        s_own[hh] = scores(hh, j)
    n_trips = (j + 1) // 2
    carry = lax.fori_loop(0, (n_trips + 1) // 2, body, init)

    for hh in heads:
        m_i, l_i, acc = carry[hh]
        s = s_own[hh]
        kk = lax.broadcasted_iota(jnp.int32, s.shape, 0)
        qq = lax.broadcasted_iota(jnp.int32, s.shape, 1)
        s = jnp.where(kk <= qq, s, NEG)
        m_new = jnp.maximum(m_i, jnp.max(s, axis=0, keepdims=True))
        alpha = jnp.exp2(m_i - m_new)
        p = jnp.exp2(s - m_new)
        l_i = alpha * l_i + jnp.sum(p, axis=0, keepdims=True)
        acc = alpha * acc + jnp.dot(vt_ref[0, hh, j], p.astype(BF16), preferred_element_type=F32)
        o_ref[:, hh * dh:(hh + 1) * dh] = (acc / l_i).T.astype(o_ref.dtype)


def _moba(q, k, vt, km_pad, bsz, seq):
    h, dh, blk = N_HEADS, HEAD_DIM, MOBA_BLOCK
    nb = seq // blk
    nbp = km_pad.shape[2]
    nh = MOBA_HEADS_PER_STEP
    return pl.pallas_call(
        _moba_kernel,
        grid=(bsz, h // nh, nb),
        in_specs=[pl.BlockSpec((1, nh, blk, dh), lambda b, j, i: (b, j, i, 0)),
                  pl.BlockSpec((1, nh, seq, dh), lambda b, j, i: (b, j, 0, 0)),
                  pl.BlockSpec((1, nh, nb, dh, blk), lambda b, j, i: (b, j, 0, 0, 0)),
                  pl.BlockSpec((1, nh, nbp, dh), lambda b, j, i: (b, j, 0, 0))],
        out_specs=pl.BlockSpec((blk, nh * dh), lambda b, j, i: (b * nb + i, j)),
        out_shape=jax.ShapeDtypeStruct((bsz * seq, h * dh), BF16),
        scratch_shapes=[pltpu.VMEM((nh, nbp, blk), F32)] + [pltpu.VMEM((nh, 2, blk, blk), F32)] * 2
                       + [pltpu.VMEM((nh, blk, blk), F32)],
        compiler_params=_cparams(3, 48),
    )(q, k, vt, km_pad)


def _pos_kernel(oh_ref, ei_ref, dest_ref, meta_ref, cnt_scr, off_scr, run_scr):
    p = pl.program_id(0)
    i = pl.program_id(1)
    tm = oh_ref.shape[0]
    oh = oh_ref[...]
    colsum = jnp.sum(oh, axis=0, keepdims=True)

    @pl.when((p == 0) & (i == 0))
    def _():
        cnt_scr[...] = jnp.zeros_like(cnt_scr)

    @pl.when(p == 0)
    def _():
        cnt_scr[...] += colsum
        dest_ref[...] = jnp.zeros_like(dest_ref)

    @pl.when((p == 1) & (i == 0))
    def _():
        cnt = cnt_scr[...]
        padded = jnp.ceil(cnt * (1.0 / TM_EXPERT)) * float(TM_EXPERT)
        r = lax.broadcasted_iota(jnp.int32, (LANES, LANES), 0)
        cidx = lax.broadcasted_iota(jnp.int32, (LANES, LANES), 1)
        tri = (r < cidx).astype(F32)
        off = jnp.dot(jnp.broadcast_to(padded, (8, LANES)), tri,
                      preferred_element_type=F32, precision=HI)
        off_scr[...] = off[0:1]
        run_scr[...] = jnp.zeros_like(run_scr)

    @pl.when(p == 1)
    def _():
        r = lax.broadcasted_iota(jnp.int32, (tm, tm), 0)
        cidx = lax.broadcasted_iota(jnp.int32, (tm, tm), 1)
        ltri = (r > cidx).astype(BF16)
        excl = jnp.dot(ltri, oh.astype(BF16), preferred_element_type=F32)
        dfull = off_scr[...] + run_scr[...] + excl
        ei = ei_ref[...]
        lane = lax.broadcasted_iota(jnp.int32, oh.shape, 1)
        lane_f = lane.astype(F32)
        dest = jnp.zeros(oh.shape, F32)
        for kk in range(TOP_K):
            ek = jnp.sum(jnp.where(lane == kk, ei, 0.0), axis=-1, keepdims=True)
            dk = jnp.sum(jnp.where(lane_f == ek, dfull, 0.0), axis=-1, keepdims=True)
            dest = jnp.where(lane == kk, dk, dest)
        dest_ref[...] = dest.astype(jnp.int32)
        run_scr[...] += colsum

    meta_ref[...] = jnp.concatenate([off_scr[...], cnt_scr[...],
                                     jnp.zeros((6, LANES), F32)], axis=0)


def _positions(onehot, ei):
    t = onehot.shape[0]
    tm = TM_DENSE
    tile = pl.BlockSpec((tm, LANES), lambda p, i: (i, 0))
    return pl.pallas_call(
        _pos_kernel,
        grid=(2, t // tm),
        in_specs=[tile, tile],
        out_specs=[pl.BlockSpec((tm, LANES), lambda p, i: (i * p, 0)),
                   pl.BlockSpec((8, LANES), lambda p, i: (0, 0))],
        out_shape=[jax.ShapeDtypeStruct((t, LANES), jnp.int32),
                   jax.ShapeDtypeStruct((8, LANES), F32)],
        scratch_shapes=[pltpu.VMEM((1, LANES), F32)] * 3,
        compiler_params=_cparams(2, 16),
    )(onehot, ei)


def _src_kernel(dest_ref, pad_lo_ref, pad_hi_ref, src_ref):
    i = pl.program_id(0)
    tb = dest_ref.shape[0] // TOP_K

    @pl.when(i == 0)
    def _():
        def zero(j, carry):
            src_ref[j] = 0
            return carry

        def per_range(rr, carry):
            lax.fori_loop(pad_lo_ref[rr], pad_hi_ref[rr], zero, 0)
            return carry
        lax.fori_loop(0, pad_lo_ref.shape[0], per_range, 0)

    def body(tt, carry):
        for kk in range(TOP_K):
            src_ref[dest_ref[tt * TOP_K + kk]] = i * tb + tt
        return carry

    lax.fori_loop(0, tb, body, 0, unroll=8)


def _sources(dest_flat, pad_lo, pad_hi, n_rows):
    n = dest_flat.shape[0]
    tb = TM_SRC
    whole = pl.BlockSpec(memory_space=pltpu.SMEM)
    return pl.pallas_call(
        _src_kernel,
        grid=(n // (tb * TOP_K),),
        in_specs=[pl.BlockSpec((tb * TOP_K,), lambda i: (i,), memory_space=pltpu.SMEM), whole, whole],
        out_specs=pl.BlockSpec(memory_space=pltpu.SMEM),
        out_shape=jax.ShapeDtypeStruct((n_rows,), jnp.int32),
        compiler_params=_cparams(1, 16),
    )(dest_flat, pad_lo, pad_hi)


def _expert_kernel(te_ref, tv_ref, nx_ref, src_cur, src_nxt, h_hbm, wgu_hbm, bgu_ref, wd_hbm, bd_ref,
                   o_ref, xbuf, sem, wgu_st, wd_st, wsem, wgu_bf, wd_bf):
    i = pl.program_id(0)
    nt = pl.num_programs(0)
    tm = o_ref.shape[0]
    slot = lax.rem(i, 2)
    e = te_ref[i]
    prev = te_ref[jnp.maximum(i - 1, 0)]
    dff = wd_hbm.shape[1]

    def weight_copies(ex):
        return (pltpu.make_async_copy(wgu_hbm.at[ex], wgu_st, wsem.at[0]),
                pltpu.make_async_copy(wd_hbm.at[ex], wd_st, wsem.at[1]))

    def gather(src_ref, s):
        def body(r8, carry):
            for uu in range(8):
                pltpu.make_async_copy(h_hbm.at[pl.ds(src_ref[r8 * 8 + uu], 1)],
                                      xbuf.at[s, r8, pl.ds(uu, 1)], sem.at[s]).start()
            return carry
        lax.fori_loop(0, tm // 8, body, 0)

    @pl.when(i == 0)
    def _():
        for cp in weight_copies(e):
            cp.start(priority=1)

    @pl.when((i == 0) & (tv_ref[0] > 0))
    def _():
        gather(src_cur, 0)

    nxt = jnp.minimum(i + 1, nt - 1)

    @pl.when((i + 1 < nt) & (tv_ref[nxt] > 0))
    def _():
        gather(src_nxt, 1 - slot)

    @pl.when((i == 0) | (e != prev))
    def _():
        for cp in weight_copies(e):
            cp.wait()
        wgu_bf[...] = wgu_st[...].astype(BF16)
        wd_bf[...] = wd_st[...].astype(BF16)

        @pl.when(nx_ref[i] >= 0)
        def _():
            for cp in weight_copies(nx_ref[i]):
                cp.start(priority=1)

    @pl.when(tv_ref[i] > 0)
    def _():
        for r8 in range(tm // 8):
            pltpu.make_async_copy(h_hbm.at[pl.ds(0, 8)], xbuf.at[slot, r8], sem.at[slot]).wait()
        x = _unpack_rows(xbuf[slot].reshape(tm, xbuf.shape[3])).astype(BF16)
        hu = jnp.dot(x, wgu_bf[...], preferred_element_type=F32) + bgu_ref[0]
        g = jnp.minimum(hu[:, :dff], SWIGLU_LIMIT)
        up = jnp.clip(hu[:, dff:], -SWIGLU_LIMIT, SWIGLU_LIMIT)
        act = (up + 1.0) * g * jax.nn.sigmoid(SWIGLU_ALPHA * g)
        y = jnp.dot(act.astype(BF16), wd_bf[...], preferred_element_type=F32) + bd_ref[0]
        o_ref[...] = _pack_rows(y)

    @pl.when(tv_ref[i] == 0)
    def _():
        o_ref[...] = jnp.zeros_like(o_ref)


def _experts(tile_e, tile_v, next_e, src, h2p, w_gu, b_gu, w_d, b_d):
    n_rows = src.shape[0]
    dp = h2p.shape[1]
    ne, d, f2 = w_gu.shape
    dff = w_d.shape[1]
    tm = TM_EXPERT
    nt = n_rows // tm
    grid_spec = pltpu.PrefetchScalarGridSpec(
        num_scalar_prefetch=3,
        grid=(nt,),
        in_specs=[pl.BlockSpec((tm,), lambda i, te, tv, nx: (i,), memory_space=pltpu.SMEM),
                  pl.BlockSpec((tm,), lambda i, te, tv, nx: (jnp.minimum(i + 1, nt - 1),),
                               memory_space=pltpu.SMEM),
                  pl.BlockSpec(memory_space=pl.ANY),
                  pl.BlockSpec(memory_space=pl.ANY),
                  pl.BlockSpec((1, 1, f2), lambda i, te, tv, nx: (te[i], 0, 0)),
                  pl.BlockSpec(memory_space=pl.ANY),
                  pl.BlockSpec((1, 1, d), lambda i, te, tv, nx: (te[i], 0, 0))],
        out_specs=pl.BlockSpec((tm, dp), lambda i, te, tv, nx: (i, 0)),
        scratch_shapes=[pltpu.VMEM((2, tm // 8, 8, dp), jnp.uint32), pltpu.SemaphoreType.DMA((2,)),
                        pltpu.VMEM((d, f2), F32), pltpu.VMEM((dff, d), F32),
                        pltpu.SemaphoreType.DMA((2,)),
                        pltpu.VMEM((d, f2), BF16), pltpu.VMEM((dff, d), BF16)],
    )
    return pl.pallas_call(
        _expert_kernel,
        grid_spec=grid_spec,
        out_shape=jax.ShapeDtypeStruct((n_rows, dp), jnp.uint32),
        compiler_params=_cparams(1, 48),
    )(tile_e, tile_v, next_e, src, src, h2p, w_gu, b_gu.reshape(ne, 1, f2), w_d, b_d.reshape(ne, 1, d))


def _combine_kernel(dcur, dnxt, gt_ref, x_ref, g_ref, gf_ref, ys_hbm, o_ref, buf, sem, *, final):
    i = pl.program_id(0)
    nt = pl.num_programs(0)
    td = x_ref.shape[0]
    slot = lax.rem(i, 2)

    def gather(dref, s):
        def body(t8, carry):
            for uu in range(8):
                for kk in range(TOP_K):
                    pltpu.make_async_copy(ys_hbm.at[pl.ds(dref[(t8 * 8 + uu) * TOP_K + kk], 1)],
                                          buf.at[s, kk, t8, pl.ds(uu, 1)], sem.at[s]).start()
            return carry
        lax.fori_loop(0, td // 8, body, 0)

    @pl.when(i == 0)
    def _():
        gather(dcur, 0)

    @pl.when(i + 1 < nt)
    def _():
        gather(dnxt, 1 - slot)

    for kk in range(TOP_K):
        for t8 in range(td // 8):
            pltpu.make_async_copy(ys_hbm.at[pl.ds(0, 8)], buf.at[slot, kk, t8], sem.at[slot]).wait()
    gt = gt_ref[...]
    rows = lambda kk: _unpack_rows(buf[slot, kk].reshape(td, buf.shape[4]))
    y = gt[:, 0:1] * rows(0)
    for kk in range(1, TOP_K):
        y = y + gt[:, kk:kk + 1] * rows(kk)
    xn = x_ref[...] + g_ref[0] * y
    if final:
        ms = jnp.mean(xn * xn, axis=-1, keepdims=True)
        xn = (xn * lax.rsqrt(ms + EPS)) * gf_ref[...]
    o_ref[...] = xn


def _combine(dest_flat, gates, x2, gate_vec, gf, ys, seq, final):
    t, d = x2.shape
    td = TD_ROWS
    tpb = seq // td
    nt = t // td
    return pl.pallas_call(
        functools.partial(_combine_kernel, final=final),
        grid=(nt,),
        in_specs=[pl.BlockSpec((td * TOP_K,), lambda i: (i,), memory_space=pltpu.SMEM),
                  pl.BlockSpec((td * TOP_K,), lambda i: (jnp.minimum(i + 1, nt - 1),),
                               memory_space=pltpu.SMEM),
                  pl.BlockSpec((td, LANES), lambda i: (i, 0)),
                  pl.BlockSpec((td, d), lambda i: (i, 0)),
                  pl.BlockSpec((1, 1, d), lambda i: (i // tpb, 0, 0)),
                  pl.BlockSpec((1, d), lambda i: (0, 0)),
                  pl.BlockSpec(memory_space=pl.ANY)],
        out_specs=pl.BlockSpec((td, d), lambda i: (i, 0)),
        out_shape=jax.ShapeDtypeStruct((t, d), F32),
        scratch_shapes=[pltpu.VMEM((2, TOP_K, td // 8, 8, d // 2), jnp.uint32),
                        pltpu.SemaphoreType.DMA((2,))],
        compiler_params=_cparams(1, 32),
    )(dest_flat, dest_flat, gates, x2, gate_vec, gf.reshape(1, d), ys)


def _moe(x2, routed, gate_vec, w_gu, b_gu, w_d, b_d, gf, seq, final):
    t, d = x2.shape
    ne = w_gu.shape[0]
    h2, onehot, ei, gates = routed
    dest, meta = _positions(onehot, ei)
    dest_flat = dest[:, :TOP_K].reshape(t * TOP_K)

    tm = TM_EXPERT
    n_rows = t * TOP_K + ne * tm
    n_tiles = n_rows // tm
    off = meta[0, :ne].astype(jnp.int32)
    cnt = meta[1, :ne].astype(jnp.int32)
    ends = off + ((cnt + tm - 1) // tm) * tm
    starts = jnp.arange(n_tiles, dtype=jnp.int32) * tm
    valid = starts < ends[-1]
    te = jnp.sum((starts[:, None] >= ends[None, :]).astype(jnp.int32), axis=1)
    te_last = jnp.sum((ends[-1] - 1 >= ends).astype(jnp.int32))
    tile_e = jnp.minimum(jnp.where(valid, te, te_last), ne - 1)
    tile_v = valid.astype(jnp.int32)
    eidx = jnp.arange(ne, dtype=jnp.int32)
    later_used = (eidx[None, :] > eidx[:, None]) & (cnt[None, :] > 0)
    next_of = jnp.min(jnp.where(later_used, eidx[None, :], ne), axis=1)
    next_of = jnp.where(next_of == ne, -1, next_of)
    next_e = jnp.sum(jnp.where(tile_e[:, None] == eidx[None, :], next_of[None, :], 0), axis=1)

    pad_lo = jnp.concatenate([off + cnt, ends[-1:]]).astype(jnp.int32)
    pad_hi = jnp.concatenate([ends, jnp.full((1,), n_rows, jnp.int32)]).astype(jnp.int32)
    src = _sources(dest_flat, pad_lo, pad_hi, n_rows)
    ys = _experts(tile_e, tile_v, next_e.astype(jnp.int32), src, h2, w_gu, b_gu, w_d, b_d)
    return _combine(dest_flat, gates, x2, gate_vec, gf, ys, seq, final)


def kernel(x, c, positions, l0_norm1_g, l0_ada_w, l0_ada_b, l0_s5_w_in, l0_s5_b_re, l0_s5_b_im, l0_s5_c_re, l0_s5_c_im, l0_s5_lam_re, l0_s5_lam_im, l0_s5_log_dt, l0_s5_d, l0_s5_w_glu, l0_s5_w_out, l0_norm2_g, l0_moe_w_router, l0_moe_b_router, l0_moe_w_gate_up, l0_moe_b_gate_up, l0_moe_w_down, l0_moe_b_down, l1_norm1_g, l1_ada_w, l1_ada_b, l1_moba_w_qkv, l1_moba_w_o, l1_norm2_g, l1_moe_w_router, l1_moe_b_router, l1_moe_w_gate_up, l1_moe_b_gate_up, l1_moe_w_down, l1_moe_b_down, final_norm_g):
    bsz, seq, d = x.shape
    t = bsz * seq
    assert seq % TM_DENSE == 0 and seq % MOBA_BLOCK == 0 and seq % S5_CHUNK == 0
    assert S5_CHUNK % S5_SEG == 0 and seq % TD_ROWS == 0
    assert seq // MOBA_BLOCK <= LANES and d == N_HEADS * HEAD_DIM
    x2 = x.reshape(t, d)
    c_pad = jnp.zeros((8, d), F32).at[:bsz].set(c)

    def ada_parts(w, b):
        a = _ada(c_pad, w, b)[:bsz]
        return [a[:, i * d:(i + 1) * d].reshape(bsz, 1, d) for i in range(6)]

    sh1, sc1, g1, sh2, sc2, g2 = ada_parts(l0_ada_w, l0_ada_b)
    u = _nml(x2, l0_norm1_g, sc1, sh1, l0_s5_w_in.astype(BF16), seq, F32)
    rate, theta, bblk, cblk = _s5_params(l0_s5_b_re, l0_s5_b_im, l0_s5_c_re, l0_s5_c_im,
                                         l0_s5_lam_re, l0_s5_lam_im, l0_s5_log_dt)
    z = _s5_scan(u, rate, theta, bblk, cblk, l0_s5_d, bsz, seq)
    x2 = _row_call(_glu_out_kernel, z, x2, g1,
                   [l0_s5_w_glu.astype(BF16), l0_s5_w_out.astype(BF16)], seq)
    routed = _router(x2, l0_norm2_g, sc2, sh2, l0_moe_w_router, l0_moe_b_router, seq)
    x2 = _moe(x2, routed, g2, l0_moe_w_gate_up, l0_moe_b_gate_up, l0_moe_w_down, l0_moe_b_down,
              final_norm_g, seq, final=False)

    sh1, sc1, g1, sh2, sc2, g2 = ada_parts(l1_ada_w, l1_ada_b)
    qkv = _nml(x2, l1_norm1_g, sc1, sh1, l1_moba_w_qkv.astype(BF16), seq, BF16)
    cos, sin = _rope_tables(positions.reshape(t, 1))
    q, k, v, km = _rope(qkv, cos, sin, bsz, seq)
    nb = seq // MOBA_BLOCK
    nbp = -(-nb // 8) * 8
    km_pad = jnp.zeros((bsz, N_HEADS, nbp, HEAD_DIM), F32).at[:, :, :nb].set(km[:, :, :, 0])
    o = _moba(q, k, v, km_pad, bsz, seq)
    x2 = _row_call(_lin_res_kernel, o, x2, g1, [l1_moba_w_o.astype(BF16)], seq)
    routed = _router(x2, l1_norm2_g, sc2, sh2, l1_moe_w_router, l1_moe_b_router, seq)
    x2 = _moe(x2, routed, g2, l1_moe_w_gate_up, l1_moe_b_gate_up, l1_moe_w_down, l1_moe_b_down,
              final_norm_g, seq, final=True)
    return x2.reshape(bsz, seq, d)
```

```python
import functools
import math

import jax
import jax.numpy as jnp
from jax import lax
from jax.experimental import pallas as pl
from jax.experimental.pallas import tpu as pltpu

F32 = jnp.float32
BF16 = jnp.bfloat16
HI = lax.Precision.HIGHEST

EPS = 1e-6
NEG = -1e30
LOG2E = math.log2(math.e)
LANES = 128
MIB = 1024 * 1024

S5_GROUP = 16
S5_STATE = 64
N_HEADS = 8
HEAD_DIM = 128
MOBA_BLOCK = 256
MOBA_TOPK = 3
ROPE_THETA = 10000.0
N_EXPERTS = 32
TOP_K = 4
SWIGLU_LIMIT = 7.0
SWIGLU_ALPHA = 1.702

TM_DENSE = 512
S5_CHUNK = 256
S5_SEG = 32
S5_SUPER = 8
MOBA_HEADS_PER_STEP = 2
TM_EXPERT = 256
TD_ROWS = 256
TM_SRC = 2048


def _cparams(n_axes, vmem_mib):
    return pltpu.CompilerParams(dimension_semantics=("arbitrary",) * n_axes,
                                vmem_limit_bytes=vmem_mib * MIB)


def _pack_rows(x):
    n = x.shape[1] // 2
    xb = x.astype(BF16).astype(F32)
    lo = lax.bitcast_convert_type(xb[:, :n], jnp.uint32) >> 16
    hi = lax.bitcast_convert_type(xb[:, n:], jnp.uint32) & jnp.uint32(0xFFFF0000)
    return lo | hi


def _unpack_rows(p):
    lo = lax.bitcast_convert_type(p << 16, F32)
    hi = lax.bitcast_convert_type(p & jnp.uint32(0xFFFF0000), F32)
    return jnp.concatenate([lo, hi], axis=1)


def _normmod(x, g, sc, sh):
    ms = jnp.mean(x * x, axis=-1, keepdims=True)
    return (x * lax.rsqrt(ms + EPS)) * g * (1.0 + sc) + sh


def _ada_kernel(c_ref, w_ref, b_ref, o_ref):
    c = c_ref[...]
    ca = c * jax.nn.sigmoid(c)
    o_ref[...] = jnp.dot(ca, w_ref[...], preferred_element_type=F32, precision=HI) + b_ref[...]


def _ada(c_pad, w, b):
    d, n = w.shape
    tn = 1536
    return pl.pallas_call(
        _ada_kernel,
        grid=(n // tn,),
        in_specs=[pl.BlockSpec((8, d), lambda j: (0, 0)),
                  pl.BlockSpec((d, tn), lambda j: (0, j)),
                  pl.BlockSpec((1, tn), lambda j: (0, j))],
        out_specs=pl.BlockSpec((8, tn), lambda j: (0, j)),
        out_shape=jax.ShapeDtypeStruct((8, n), F32),
        compiler_params=_cparams(1, 32),
    )(c_pad, w, b.reshape(1, n))


def _nml_kernel(x_ref, g_ref, sc_ref, sh_ref, w_ref, o_ref):
    h = _normmod(x_ref[...], g_ref[...], sc_ref[0], sh_ref[0])
    o_ref[...] = jnp.dot(h.astype(BF16), w_ref[...], preferred_element_type=F32).astype(o_ref.dtype)


def _nml(x2, g, sc, sh, w_bf, seq, out_dtype):
    t, d = x2.shape
    n = w_bf.shape[1]
    tm = TM_DENSE
    tpb = seq // tm
    return pl.pallas_call(
        _nml_kernel,
        grid=(t // tm,),
        in_specs=[pl.BlockSpec((tm, d), lambda i: (i, 0)),
                  pl.BlockSpec((1, d), lambda i: (0, 0)),
                  pl.BlockSpec((1, 1, d), lambda i: (i // tpb, 0, 0)),
                  pl.BlockSpec((1, 1, d), lambda i: (i // tpb, 0, 0)),
                  pl.BlockSpec((d, n), lambda i: (0, 0))],
        out_specs=pl.BlockSpec((tm, n), lambda i: (i, 0)),
        out_shape=jax.ShapeDtypeStruct((t, n), out_dtype),
        compiler_params=_cparams(1, 48),
    )(x2, g.reshape(1, d), sc, sh, w_bf)


def _gelu_tanh(y):
    c = math.sqrt(2.0 / math.pi)
    return y * (0.5 * (1.0 + jnp.tanh(c * (y + 0.044715 * (y * y * y)))))


def _s5_kernel(u_ref, rate_ref, theta_ref, bblk_ref, cblk_ref, d_ref, z_ref,
               apr, api, a1r, a1i, air, aii, xr, xi):
    L = u_ref.shape[0]
    seg = apr.shape[0]
    n_super = bblk_ref.shape[0]
    half = apr.shape[1] // n_super
    nch = u_ref.shape[1] // n_super
    b = pl.program_id(0)
    c = pl.program_id(1)

    @pl.when((b == 0) & (c == 0))
    def _():
        tt = lax.broadcasted_iota(jnp.int32, (seg, 1), 0).astype(F32)
        ph = theta_ref[...] * tt
        rt = rate_ref[...] * tt
        cs = jnp.cos(ph)
        sn = jnp.sin(ph)
        mag = jnp.exp(rt)
        inv = jnp.exp(-rt)
        apr[...] = mag * cs
        api[...] = mag * sn
        air[...] = inv * cs
        aii[...] = -(inv * sn)
        ph1 = theta_ref[...] * (tt + 1.0)
        mag1 = jnp.exp(rate_ref[...] * (tt + 1.0))
        a1r[...] = mag1 * jnp.cos(ph1)
        a1i[...] = mag1 * jnp.sin(ph1)

    @pl.when(c == 0)
    def _():
        xr[...] = jnp.zeros_like(xr)
        xi[...] = jnp.zeros_like(xi)

    row = lax.broadcasted_iota(jnp.int32, (L, L), 0)
    col = lax.broadcasted_iota(jnp.int32, (L, L), 1)
    tril = ((row >= col) & (row // seg == col // seg)).astype(BF16)

    for sg in range(n_super):
        cols = slice(sg * half, (sg + 1) * half)
        ch = slice(sg * nch, (sg + 1) * nch)
        u = u_ref[:, ch]
        bu = jnp.dot(u.astype(BF16), bblk_ref[sg], preferred_element_type=F32)
        ir = air[:, cols]
        ii = aii[:, cols]
        zs = []
        for k in range(L // seg):
            bur = bu[k * seg:(k + 1) * seg, :half]
            bui = bu[k * seg:(k + 1) * seg, half:]
            zs.append(jnp.concatenate([bur * ir - bui * ii, bur * ii + bui * ir], axis=1))
        zc = jnp.concatenate(zs, axis=0).astype(BF16)
        cum = jnp.dot(tril, zc, preferred_element_type=F32)
        pw_r = apr[:, cols]
        pw_i = api[:, cols]
        p1_r = a1r[:, cols]
        p1_i = a1i[:, cols]
        last_r = xr[:, cols]
        last_i = xi[:, cols]
        xs = []
        for k in range(L // seg):
            cr = cum[k * seg:(k + 1) * seg, :half]
            ci = cum[k * seg:(k + 1) * seg, half:]
            x_re = cr * pw_r - ci * pw_i + (p1_r * last_r - p1_i * last_i)
            x_im = cr * pw_i + ci * pw_r + (p1_r * last_i + p1_i * last_r)
            last_r = x_re[seg - 1:seg, :]
            last_i = x_im[seg - 1:seg, :]
            xs.append(jnp.concatenate([x_re, x_im], axis=1))
        xr[:, cols] = last_r
        xi[:, cols] = last_i
        xc = jnp.concatenate(xs, axis=0).astype(BF16)
        y = jnp.dot(xc, cblk_ref[sg], preferred_element_type=F32) + d_ref[:, ch] * u
        z_ref[:, ch] = _gelu_tanh(y).astype(z_ref.dtype)


def _s5_scan(u, rate, theta, bblk, cblk, d_skip, bsz, seq):
    t, w = u.shape
    L = S5_CHUNK
    nc = seq // L
    ns = rate.shape[1]
    return pl.pallas_call(
        _s5_kernel,
        grid=(bsz, nc),
        in_specs=[pl.BlockSpec((L, w), lambda b, c: (b * nc + c, 0)),
                  pl.BlockSpec((1, ns), lambda b, c: (0, 0)),
                  pl.BlockSpec((1, ns), lambda b, c: (0, 0)),
                  pl.BlockSpec(bblk.shape, lambda b, c: (0, 0, 0)),
                  pl.BlockSpec(cblk.shape, lambda b, c: (0, 0, 0)),
                  pl.BlockSpec((1, w), lambda b, c: (0, 0))],
        out_specs=pl.BlockSpec((L, w), lambda b, c: (b * nc + c, 0)),
        out_shape=jax.ShapeDtypeStruct((t, w), BF16),
        scratch_shapes=[pltpu.VMEM((S5_SEG, ns), F32)] * 6 + [pltpu.VMEM((1, ns), F32)] * 2,
        compiler_params=_cparams(2, 48),
    )(u, rate, theta, bblk, cblk, d_skip.reshape(1, w))


def _s5_params(b_re, b_im, c_re, c_im, lam_re, lam_im, log_dt):
    g, p, ch = b_re.shape
    dt = jnp.exp(log_dt)[:, None]
    rate = lam_re * dt
    theta = lam_im * dt
    mag = jnp.exp(rate)
    a_re = mag * jnp.cos(theta)
    a_im = mag * jnp.sin(theta)
    den = lam_re * lam_re + lam_im * lam_im
    f_re = ((a_re - 1.0) * lam_re + a_im * lam_im) / den
    f_im = (a_im * lam_re - (a_re - 1.0) * lam_im) / den
    bb_re = f_re[..., None] * b_re - f_im[..., None] * b_im
    bb_im = f_re[..., None] * b_im + f_im[..., None] * b_re
    ns = S5_SUPER
    nsg = g // ns
    eye = jnp.eye(ns, dtype=F32)

    def blk_b(bb):
        return jnp.einsum('sgpc,gh->sgchp', bb.reshape(nsg, ns, p, ch), eye).reshape(nsg, ns * ch, ns * p)

    def blk_c(cc):
        return jnp.einsum('sgcp,gh->sgphc', cc.reshape(nsg, ns, ch, p), eye).reshape(nsg, ns * p, ns * ch)

    bblk = jnp.concatenate([blk_b(bb_re), blk_b(bb_im)], axis=2).astype(BF16)
    cblk = jnp.concatenate([blk_c(c_re), -blk_c(c_im)], axis=1).astype(BF16)
    return rate.reshape(1, g * p), theta.reshape(1, g * p), bblk, cblk


def _route(x, g, sc, sh, wr, br):
    h = _normmod(x, g, sc, sh).astype(BF16)
    logits = jnp.dot(h, wr, preferred_element_type=F32) + br
    lane = lax.broadcasted_iota(jnp.int32, logits.shape, 1)
    lane_f = lane.astype(F32)
    work = logits
    onehot = jnp.zeros(logits.shape, F32)
    ei = jnp.zeros(logits.shape, F32)
    vals = []
    for kk in range(TOP_K):
        m = jnp.max(work, axis=-1, keepdims=True)
        idx = jnp.min(jnp.where(work == m, lane_f, float(LANES)), axis=-1, keepdims=True)
        hit = lane_f == idx
        onehot = jnp.where(hit, 1.0, onehot)
        ei = jnp.where(lane == kk, idx, ei)
        vals.append(m)
        work = jnp.where(hit, 2.0 * NEG, work)
    ex = [jnp.exp(vv - vals[0]) for vv in vals]
    tot = ex[0] + ex[1] + ex[2] + ex[3]
    gt = jnp.zeros(logits.shape, F32)
    for kk in range(TOP_K):
        gt = jnp.where(lane == kk, ex[kk] / tot, gt)
    return _pack_rows(h), onehot, ei, gt


def _glu_out_kernel(z_ref, x_ref, g_ref, wg_ref, wo_ref, o_ref):
    z = z_ref[...]
    gate = jax.nn.sigmoid(jnp.dot(z, wg_ref[...], preferred_element_type=F32))
    m = jnp.dot((z.astype(F32) * gate).astype(BF16), wo_ref[...], preferred_element_type=F32)
    o_ref[...] = x_ref[...] + g_ref[0] * m


def _lin_res_kernel(a_ref, x_ref, g_ref, w_ref, o_ref):
    m = jnp.dot(a_ref[...].astype(BF16), w_ref[...], preferred_element_type=F32)
    o_ref[...] = x_ref[...] + g_ref[0] * m


def _row_call(kernel, acts, x2, gate, weights, seq):
    t, d = x2.shape
    tm = TM_DENSE
    tpb = seq // tm
    row = pl.BlockSpec((tm, d), lambda i: (i, 0))
    return pl.pallas_call(
        kernel,
        grid=(t // tm,),
        in_specs=[row, row, pl.BlockSpec((1, 1, d), lambda i: (i // tpb, 0, 0))]
                 + [pl.BlockSpec(w.shape, lambda i: (0, 0)) for w in weights],
        out_specs=row,
        out_shape=jax.ShapeDtypeStruct((t, d), F32),
        compiler_params=_cparams(1, 48),
    )(acts, x2, gate, *weights)


def _router_kernel(x_ref, g_ref, sc_ref, sh_ref, wr_ref, br_ref, h_ref, oh_ref, ei_ref, gt_ref):
    h_ref[...], oh_ref[...], ei_ref[...], gt_ref[...] = _route(
        x_ref[...], g_ref[...], sc_ref[0], sh_ref[0], wr_ref[...], br_ref[...])


def _router(x2, g, sc, sh, w_router, b_router, seq):
    t, d = x2.shape
    ne = w_router.shape[1]
    wr_pad = jnp.zeros((d, LANES), F32).at[:, :ne].set(w_router).astype(BF16)
    br_pad = jnp.full((1, LANES), NEG, F32).at[0, :ne].set(b_router)
    tm = TM_DENSE
    tpb = seq // tm
    lane_tile = pl.BlockSpec((tm, LANES), lambda i: (i, 0))
    per_batch = pl.BlockSpec((1, 1, d), lambda i: (i // tpb, 0, 0))
    return pl.pallas_call(
        _router_kernel,
        grid=(t // tm,),
        in_specs=[pl.BlockSpec((tm, d), lambda i: (i, 0)),
                  pl.BlockSpec((1, d), lambda i: (0, 0)), per_batch, per_batch,
                  pl.BlockSpec((d, LANES), lambda i: (0, 0)),
                  pl.BlockSpec((1, LANES), lambda i: (0, 0))],
        out_specs=[pl.BlockSpec((tm, d // 2), lambda i: (i, 0)), lane_tile, lane_tile, lane_tile],
        out_shape=[jax.ShapeDtypeStruct((t, d // 2), jnp.uint32)]
                  + [jax.ShapeDtypeStruct((t, LANES), F32)] * 3,
        compiler_params=_cparams(1, 32),
    )(x2, g.reshape(1, d), sc, sh, wr_pad, br_pad)


def _rope_tab_kernel(pos_ref, inv_ref, sgn_ref, cos_ref, sin_ref):
    ang = pos_ref[...].astype(F32) * inv_ref[...]
    cos_ref[...] = jnp.cos(ang)
    sin_ref[...] = jnp.sin(ang) * sgn_ref[...]


def _rope_tables(pos_col):
    t = pos_col.shape[0]
    ts = 512
    half = HEAD_DIM // 2
    inv = ROPE_THETA ** (-jnp.arange(0, HEAD_DIM, 2, dtype=F32) / HEAD_DIM)
    inv = jnp.concatenate([inv, inv]).reshape(1, HEAD_DIM)
    sgn = jnp.concatenate([-jnp.ones((half,), F32), jnp.ones((half,), F32)]).reshape(1, HEAD_DIM)
    const = pl.BlockSpec((1, HEAD_DIM), lambda i: (0, 0))
    tile = pl.BlockSpec((ts, HEAD_DIM), lambda i: (i, 0))
    return pl.pallas_call(
        _rope_tab_kernel,
        grid=(t // ts,),
        in_specs=[pl.BlockSpec((ts, 1), lambda i: (i, 0)), const, const],
        out_specs=[tile, tile],
        out_shape=[jax.ShapeDtypeStruct((t, HEAD_DIM), F32)] * 2,
        compiler_params=_cparams(1, 16),
    )(pos_col, inv, sgn)


def _rope_kernel(q_ref, k_ref, v_ref, cos_ref, sin_ref, qo_ref, ko_ref, vo_ref, km_ref):
    cs = cos_ref[...]
    sn = sin_ref[...]
    dh = cs.shape[1]
    half = dh // 2
    for hh in range(qo_ref.shape[1]):
        cols = slice(hh * dh, (hh + 1) * dh)
        q = q_ref[:, cols].astype(F32)
        k = k_ref[:, cols].astype(F32)
        qr = q * cs + pltpu.roll(q, half, 1) * sn
        kr = k * cs + pltpu.roll(k, half, 1) * sn
        qo_ref[0, hh] = qr
        ko_ref[0, hh] = kr.astype(BF16)
        vo_ref[0, hh, 0] = v_ref[:, cols].astype(F32).T.astype(BF16)
        km_ref[0, hh, 0] = jnp.mean(kr, axis=0, keepdims=True)


def _rope(qkv, cos, sin, bsz, seq):
    t = qkv.shape[0]
    h, dh, blk = N_HEADS, HEAD_DIM, MOBA_BLOCK
    nb = seq // blk
    part = lambda p: pl.BlockSpec((blk, h * dh), lambda i: (i, p))
    tab = pl.BlockSpec((blk, dh), lambda i: (i, 0))
    head = pl.BlockSpec((1, h, blk, dh), lambda i: (i // nb, 0, i % nb, 0))
    return pl.pallas_call(
        _rope_kernel,
        grid=(t // blk,),
        in_specs=[part(0), part(1), part(2), tab, tab],
        out_specs=[head, head,
                   pl.BlockSpec((1, h, 1, dh, blk), lambda i: (i // nb, 0, i % nb, 0, 0)),
                   pl.BlockSpec((1, h, 1, 1, dh), lambda i: (i // nb, 0, i % nb, 0, 0))],
        out_shape=[jax.ShapeDtypeStruct((bsz, h, seq, dh), F32),
                   jax.ShapeDtypeStruct((bsz, h, seq, dh), BF16),
                   jax.ShapeDtypeStruct((bsz, h, nb, dh, blk), BF16),
                   jax.ShapeDtypeStruct((bsz, h, nb, 1, dh), F32)],
        compiler_params=_cparams(1, 32),
    )(qkv, qkv, qkv, cos, sin)


def _moba_kernel(q_ref, k_ref, vt_ref, km_ref, o_ref, sel_scr, s_a, s_b, s_own):
    nh = q_ref.shape[1]
    blk = q_ref.shape[2]
    dh = q_ref.shape[3]
    j = pl.program_id(2)
    nt = (((1,), (1,)), ((), ()))
    heads = range(nh)

    qs = []
    for hh in heads:
        q = q_ref[0, hh]
        gate = lax.dot_general(km_ref[0, hh].astype(BF16), q.astype(BF16), nt,
                               preferred_element_type=F32)
        bidx = lax.broadcasted_iota(jnp.int32, gate.shape, 0)
        bidx_f = bidx.astype(F32)
        work = jnp.where(bidx < j, gate, NEG)
        sel = jnp.zeros(gate.shape, F32)
        for _ in range(MOBA_TOPK):
            m = jnp.max(work, axis=0, keepdims=True)
            idx = jnp.min(jnp.where(work == m, bidx_f, float(LANES)), axis=0, keepdims=True)
            hit = bidx_f == idx
            sel = jnp.where(hit & (m > 0.5 * NEG), 1.0, sel)
            work = jnp.where(hit, 2.0 * NEG, work)
        sel_scr[hh] = sel
        qs.append((q * (dh ** -0.5 * LOG2E)).astype(BF16))

    def scores(hh, n):
        start = pl.multiple_of(n * blk, blk)
        kb = k_ref[0, hh, pl.ds(start, blk), :]
        return lax.dot_general(kb, qs[hh], nt, preferred_element_type=F32)

    nb_last = vt_ref.shape[2] - 1

    def score_trip(scr, nn):
        for hh in heads:
            for bb in range(2):
                scr[hh, bb] = scores(hh, jnp.minimum(2 * nn + bb, nb_last))

    def past_pair(hh, nn, scr, m_i, l_i, acc):
        n0 = jnp.minimum(2 * nn, nb_last)
        n1 = jnp.minimum(2 * nn + 1, nb_last)
        s0 = scr[hh, 0]
        s1 = scr[hh, 1]
        r0 = (sel_scr[hh, pl.ds(n0, 1), :] > 0.5) & (2 * nn < j)
        r1 = (sel_scr[hh, pl.ds(n1, 1), :] > 0.5) & (2 * nn + 1 < j)
        rm0 = jnp.where(r0, jnp.max(s0, axis=0, keepdims=True), NEG)
        rm1 = jnp.where(r1, jnp.max(s1, axis=0, keepdims=True), NEG)
        m_new = jnp.maximum(m_i, jnp.maximum(rm0, rm1))
        alpha = jnp.exp2(m_i - m_new)
        p0 = jnp.exp2(s0 - jnp.where(r0, m_new, -NEG))
        p1 = jnp.exp2(s1 - jnp.where(r1, m_new, -NEG))
        l_new = alpha * l_i + (jnp.sum(p0, axis=0, keepdims=True) + jnp.sum(p1, axis=0, keepdims=True))
        pv = (jnp.dot(vt_ref[0, hh, n0], p0.astype(BF16), preferred_element_type=F32)
              + jnp.dot(vt_ref[0, hh, n1], p1.astype(BF16), preferred_element_type=F32))
        return m_new, l_new, alpha * acc + pv

    def body(mm, state):
        score_trip(s_b, 2 * mm + 1)
        state = tuple(past_pair(hh, 2 * mm, s_a, *state[hh]) for hh in heads)
        score_trip(s_a, 2 * mm + 2)
        return tuple(past_pair(hh, 2 * mm + 1, s_b, *state[hh]) for hh in heads)

    init = tuple((jnp.full((1, blk), NEG, F32), jnp.zeros((1, blk), F32), jnp.zeros((dh, blk), F32))
                 for _ in heads)
    score_trip(s_a, 0)
    for hh in heads:
        s_own[hh] = scores(hh, j)
    n_trips = (j + 1) // 2
    carry = lax.fori_loop(0, (n_trips + 1) // 2, body, init)

    for hh in heads:
        m_i, l_i, acc = carry[hh]
        s = s_own[hh]
        kk = lax.broadcasted_iota(jnp.int32, s.shape, 0)
        qq = lax.broadcasted_iota(jnp.int32, s.shape, 1)
        s = jnp.where(kk <= qq, s, NEG)
        m_new = jnp.maximum(m_i, jnp.max(s, axis=0, keepdims=True))
        alpha = jnp.exp2(m_i - m_new)
        p = jnp.exp2(s - m_new)
        l_i = alpha * l_i + jnp.sum(p, axis=0, keepdims=True)
        acc = alpha * acc + jnp.dot(vt_ref[0, hh, j], p.astype(BF16), preferred_element_type=F32)
        o_ref[:, hh * dh:(hh + 1) * dh] = (acc / l_i).T.astype(o_ref.dtype)


def _moba(q, k, vt, km_pad, bsz, seq):
    h, dh, blk = N_HEADS, HEAD_DIM, MOBA_BLOCK
    nb = seq // blk
    nbp = km_pad.shape[2]
    nh = MOBA_HEADS_PER_STEP
    return pl.pallas_call(
        _moba_kernel,
        grid=(bsz, h // nh, nb),
        in_specs=[pl.BlockSpec((1, nh, blk, dh), lambda b, j, i: (b, j, i, 0)),
                  pl.BlockSpec((1, nh, seq, dh), lambda b, j, i: (b, j, 0, 0)),
                  pl.BlockSpec((1, nh, nb, dh, blk), lambda b, j, i: (b, j, 0, 0, 0)),
                  pl.BlockSpec((1, nh, nbp, dh), lambda b, j, i: (b, j, 0, 0))],
        out_specs=pl.BlockSpec((blk, nh * dh), lambda b, j, i: (b * nb + i, j)),
        out_shape=jax.ShapeDtypeStruct((bsz * seq, h * dh), BF16),
        scratch_shapes=[pltpu.VMEM((nh, nbp, blk), F32)] + [pltpu.VMEM((nh, 2, blk, blk), F32)] * 2
                       + [pltpu.VMEM((nh, blk, blk), F32)],
        compiler_params=_cparams(3, 48),
    )(q, k, vt, km_pad)


def _pos_kernel(oh_ref, ei_ref, dest_ref, meta_ref, cnt_scr, off_scr, run_scr):
    p = pl.program_id(0)
    i = pl.program_id(1)
    tm = oh_ref.shape[0]
    oh = oh_ref[...]
    colsum = jnp.sum(oh, axis=0, keepdims=True)

    @pl.when((p == 0) & (i == 0))
    def _():
        cnt_scr[...] = jnp.zeros_like(cnt_scr)

    @pl.when(p == 0)
    def _():
        cnt_scr[...] += colsum
        dest_ref[...] = jnp.zeros_like(dest_ref)

    @pl.when((p == 1) & (i == 0))
    def _():
        cnt = cnt_scr[...]
        padded = jnp.ceil(cnt * (1.0 / TM_EXPERT)) * float(TM_EXPERT)
        r = lax.broadcasted_iota(jnp.int32, (LANES, LANES), 0)
        cidx = lax.broadcasted_iota(jnp.int32, (LANES, LANES), 1)
        tri = (r < cidx).astype(F32)
        off = jnp.dot(jnp.broadcast_to(padded, (8, LANES)), tri,
                      preferred_element_type=F32, precision=HI)
        off_scr[...] = off[0:1]
        run_scr[...] = jnp.zeros_like(run_scr)

    @pl.when(p == 1)
    def _():
        r = lax.broadcasted_iota(jnp.int32, (tm, tm), 0)
        cidx = lax.broadcasted_iota(jnp.int32, (tm, tm), 1)
        ltri = (r > cidx).astype(BF16)
        excl = jnp.dot(ltri, oh.astype(BF16), preferred_element_type=F32)
        dfull = off_scr[...] + run_scr[...] + excl
        ei = ei_ref[...]
        lane = lax.broadcasted_iota(jnp.int32, oh.shape, 1)
        lane_f = lane.astype(F32)
        dest = jnp.zeros(oh.shape, F32)
        for kk in range(TOP_K):
            ek = jnp.sum(jnp.where(lane == kk, ei, 0.0), axis=-1, keepdims=True)
            dk = jnp.sum(jnp.where(lane_f == ek, dfull, 0.0), axis=-1, keepdims=True)
            dest = jnp.where(lane == kk, dk, dest)
        dest_ref[...] = dest.astype(jnp.int32)
        run_scr[...] += colsum

    meta_ref[...] = jnp.concatenate([off_scr[...], cnt_scr[...],
                                     jnp.zeros((6, LANES), F32)], axis=0)


def _positions(onehot, ei):
    t = onehot.shape[0]
    tm = TM_DENSE
    tile = pl.BlockSpec((tm, LANES), lambda p, i: (i, 0))
    return pl.pallas_call(
        _pos_kernel,
        grid=(2, t // tm),
        in_specs=[tile, tile],
        out_specs=[pl.BlockSpec((tm, LANES), lambda p, i: (i * p, 0)),
                   pl.BlockSpec((8, LANES), lambda p, i: (0, 0))],
        out_shape=[jax.ShapeDtypeStruct((t, LANES), jnp.int32),
                   jax.ShapeDtypeStruct((8, LANES), F32)],
        scratch_shapes=[pltpu.VMEM((1, LANES), F32)] * 3,
        compiler_params=_cparams(2, 16),
    )(onehot, ei)


def _src_kernel(dest_ref, pad_lo_ref, pad_hi_ref, src_ref):
    i = pl.program_id(0)
    tb = dest_ref.shape[0] // TOP_K

    @pl.when(i == 0)
    def _():
        def zero(j, carry):
            src_ref[j] = 0
            return carry

        def per_range(rr, carry):
            lax.fori_loop(pad_lo_ref[rr], pad_hi_ref[rr], zero, 0)
            return carry
        lax.fori_loop(0, pad_lo_ref.shape[0], per_range, 0)

    def body(tt, carry):
        for kk in range(TOP_K):
            src_ref[dest_ref[tt * TOP_K + kk]] = i * tb + tt
        return carry

    lax.fori_loop(0, tb, body, 0, unroll=8)


def _sources(dest_flat, pad_lo, pad_hi, n_rows):
    n = dest_flat.shape[0]
    tb = TM_SRC
    whole = pl.BlockSpec(memory_space=pltpu.SMEM)
    return pl.pallas_call(
        _src_kernel,
        grid=(n // (tb * TOP_K),),
        in_specs=[pl.BlockSpec((tb * TOP_K,), lambda i: (i,), memory_space=pltpu.SMEM), whole, whole],
        out_specs=pl.BlockSpec(memory_space=pltpu.SMEM),
        out_shape=jax.ShapeDtypeStruct((n_rows,), jnp.int32),
        compiler_params=_cparams(1, 16),
    )(dest_flat, pad_lo, pad_hi)


def _expert_kernel(te_ref, tv_ref, nx_ref, src_cur, src_nxt, h_hbm, wgu_hbm, bgu_ref, wd_hbm, bd_ref,
                   o_ref, xbuf, sem, wgu_st, wd_st, wsem, wgu_bf, wd_bf):
    i = pl.program_id(0)
    nt = pl.num_programs(0)
    tm = o_ref.shape[0]
    slot = lax.rem(i, 2)
    e = te_ref[i]
    prev = te_ref[jnp.maximum(i - 1, 0)]
    dff = wd_hbm.shape[1]

    def weight_copies(ex):
        return (pltpu.make_async_copy(wgu_hbm.at[ex], wgu_st, wsem.at[0]),
                pltpu.make_async_copy(wd_hbm.at[ex], wd_st, wsem.at[1]))

    def gather(src_ref, s):
        def body(r8, carry):
            for uu in range(8):
                pltpu.make_async_copy(h_hbm.at[pl.ds(src_ref[r8 * 8 + uu], 1)],
                                      xbuf.at[s, r8, pl.ds(uu, 1)], sem.at[s]).start()
            return carry
        lax.fori_loop(0, tm // 8, body, 0)

    @pl.when(i == 0)
    def _():
        for cp in weight_copies(e):
            cp.start(priority=1)

    @pl.when((i == 0) & (tv_ref[0] > 0))
    def _():
        gather(src_cur, 0)

    nxt = jnp.minimum(i + 1, nt - 1)

    @pl.when((i + 1 < nt) & (tv_ref[nxt] > 0))
    def _():
        gather(src_nxt, 1 - slot)

    @pl.when((i == 0) | (e != prev))
    def _():
        for cp in weight_copies(e):
            cp.wait()
        wgu_bf[...] = wgu_st[...].astype(BF16)
        wd_bf[...] = wd_st[...].astype(BF16)

        @pl.when(nx_ref[i] >= 0)
        def _():
            for cp in weight_copies(nx_ref[i]):
                cp.start(priority=1)

    @pl.when(tv_ref[i] > 0)
    def _():
        for r8 in range(tm // 8):
            pltpu.make_async_copy(h_hbm.at[pl.ds(0, 8)], xbuf.at[slot, r8], sem.at[slot]).wait()
        x = _unpack_rows(xbuf[slot].reshape(tm, xbuf.shape[3])).astype(BF16)
        hu = jnp.dot(x, wgu_bf[...], preferred_element_type=F32) + bgu_ref[0]
        g = jnp.minimum(hu[:, :dff], SWIGLU_LIMIT)
        up = jnp.clip(hu[:, dff:], -SWIGLU_LIMIT, SWIGLU_LIMIT)
        act = (up + 1.0) * g * jax.nn.sigmoid(SWIGLU_ALPHA * g)
        y = jnp.dot(act.astype(BF16), wd_bf[...], preferred_element_type=F32) + bd_ref[0]
        o_ref[...] = _pack_rows(y)

    @pl.when(tv_ref[i] == 0)
    def _():
        o_ref[...] = jnp.zeros_like(o_ref)


def _experts(tile_e, tile_v, next_e, src, h2p, w_gu, b_gu, w_d, b_d):
    n_rows = src.shape[0]
    dp = h2p.shape[1]
    ne, d, f2 = w_gu.shape
    dff = w_d.shape[1]
    tm = TM_EXPERT
    nt = n_rows // tm
    grid_spec = pltpu.PrefetchScalarGridSpec(
        num_scalar_prefetch=3,
        grid=(nt,),
        in_specs=[pl.BlockSpec((tm,), lambda i, te, tv, nx: (i,), memory_space=pltpu.SMEM),
                  pl.BlockSpec((tm,), lambda i, te, tv, nx: (jnp.minimum(i + 1, nt - 1),),
                               memory_space=pltpu.SMEM),
                  pl.BlockSpec(memory_space=pl.ANY),
                  pl.BlockSpec(memory_space=pl.ANY),
                  pl.BlockSpec((1, 1, f2), lambda i, te, tv, nx: (te[i], 0, 0)),
                  pl.BlockSpec(memory_space=pl.ANY),
                  pl.BlockSpec((1, 1, d), lambda i, te, tv, nx: (te[i], 0, 0))],
        out_specs=pl.BlockSpec((tm, dp), lambda i, te, tv, nx: (i, 0)),
        scratch_shapes=[pltpu.VMEM((2, tm // 8, 8, dp), jnp.uint32), pltpu.SemaphoreType.DMA((2,)),
                        pltpu.VMEM((d, f2), F32), pltpu.VMEM((dff, d), F32),
                        pltpu.SemaphoreType.DMA((2,)),
                        pltpu.VMEM((d, f2), BF16), pltpu.VMEM((dff, d), BF16)],
    )
    return pl.pallas_call(
        _expert_kernel,
        grid_spec=grid_spec,
        out_shape=jax.ShapeDtypeStruct((n_rows, dp), jnp.uint32),
        compiler_params=_cparams(1, 48),
    )(tile_e, tile_v, next_e, src, src, h2p, w_gu, b_gu.reshape(ne, 1, f2), w_d, b_d.reshape(ne, 1, d))


def _combine_kernel(dcur, dnxt, gt_ref, x_ref, g_ref, gf_ref, ys_hbm, o_ref, buf, sem, *, final):
    i = pl.program_id(0)
    nt = pl.num_programs(0)
    td = x_ref.shape[0]
    slot = lax.rem(i, 2)

    def gather(dref, s):
        def body(t8, carry):
            for uu in range(8):
                for kk in range(TOP_K):
                    pltpu.make_async_copy(ys_hbm.at[pl.ds(dref[(t8 * 8 + uu) * TOP_K + kk], 1)],
                                          buf.at[s, kk, t8, pl.ds(uu, 1)], sem.at[s]).start()
            return carry
        lax.fori_loop(0, td // 8, body, 0)

    @pl.when(i == 0)
    def _():
        gather(dcur, 0)

    @pl.when(i + 1 < nt)
    def _():
        gather(dnxt, 1 - slot)

    for kk in range(TOP_K):
        for t8 in range(td // 8):
            pltpu.make_async_copy(ys_hbm.at[pl.ds(0, 8)], buf.at[slot, kk, t8], sem.at[slot]).wait()
    gt = gt_ref[...]
    rows = lambda kk: _unpack_rows(buf[slot, kk].reshape(td, buf.shape[4]))
    y = gt[:, 0:1] * rows(0)
    for kk in range(1, TOP_K):
        y = y + gt[:, kk:kk + 1] * rows(kk)
    xn = x_ref[...] + g_ref[0] * y
    if final:
        ms = jnp.mean(xn * xn, axis=-1, keepdims=True)
        xn = (xn * lax.rsqrt(ms + EPS)) * gf_ref[...]
    o_ref[...] = xn


def _combine(dest_flat, gates, x2, gate_vec, gf, ys, seq, final):
    t, d = x2.shape
    td = TD_ROWS
    tpb = seq // td
    nt = t // td
    return pl.pallas_call(
        functools.partial(_combine_kernel, final=final),
        grid=(nt,),
        in_specs=[pl.BlockSpec((td * TOP_K,), lambda i: (i,), memory_space=pltpu.SMEM),
                  pl.BlockSpec((td * TOP_K,), lambda i: (jnp.minimum(i + 1, nt - 1),),
                               memory_space=pltpu.SMEM),
                  pl.BlockSpec((td, LANES), lambda i: (i, 0)),
                  pl.BlockSpec((td, d), lambda i: (i, 0)),
                  pl.BlockSpec((1, 1, d), lambda i: (i // tpb, 0, 0)),
                  pl.BlockSpec((1, d), lambda i: (0, 0)),
                  pl.BlockSpec(memory_space=pl.ANY)],
        out_specs=pl.BlockSpec((td, d), lambda i: (i, 0)),
        out_shape=jax.ShapeDtypeStruct((t, d), F32),
        scratch_shapes=[pltpu.VMEM((2, TOP_K, td // 8, 8, d // 2), jnp.uint32),
                        pltpu.SemaphoreType.DMA((2,))],
        compiler_params=_cparams(1, 32),
    )(dest_flat, dest_flat, gates, x2, gate_vec, gf.reshape(1, d), ys)


def _moe(x2, routed, gate_vec, w_gu, b_gu, w_d, b_d, gf, seq, final):
    t, d = x2.shape
    ne = w_gu.shape[0]
    h2, onehot, ei, gates = routed
    dest, meta = _positions(onehot, ei)
    dest_flat = dest[:, :TOP_K].reshape(t * TOP_K)

    tm = TM_EXPERT
    n_rows = t * TOP_K + ne * tm
    n_tiles = n_rows // tm
    off = meta[0, :ne].astype(jnp.int32)
    cnt = meta[1, :ne].astype(jnp.int32)
    ends = off + ((cnt + tm - 1) // tm) * tm
    starts = jnp.arange(n_tiles, dtype=jnp.int32) * tm
    valid = starts < ends[-1]
    te = jnp.sum((starts[:, None] >= ends[None, :]).astype(jnp.int32), axis=1)
    te_last = jnp.sum((ends[-1] - 1 >= ends).astype(jnp.int32))
    tile_e = jnp.minimum(jnp.where(valid, te, te_last), ne - 1)
    tile_v = valid.astype(jnp.int32)
    eidx = jnp.arange(ne, dtype=jnp.int32)
    later_used = (eidx[None, :] > eidx[:, None]) & (cnt[None, :] > 0)
    next_of = jnp.min(jnp.where(later_used, eidx[None, :], ne), axis=1)
    next_of = jnp.where(next_of == ne, -1, next_of)
    next_e = jnp.sum(jnp.where(tile_e[:, None] == eidx[None, :], next_of[None, :], 0), axis=1)

    pad_lo = jnp.concatenate([off + cnt, ends[-1:]]).astype(jnp.int32)
    pad_hi = jnp.concatenate([ends, jnp.full((1,), n_rows, jnp.int32)]).astype(jnp.int32)
    src = _sources(dest_flat, pad_lo, pad_hi, n_rows)
    ys = _experts(tile_e, tile_v, next_e.astype(jnp.int32), src, h2, w_gu, b_gu, w_d, b_d)
    return _combine(dest_flat, gates, x2, gate_vec, gf, ys, seq, final)


def kernel(x, c, positions, l0_norm1_g, l0_ada_w, l0_ada_b, l0_s5_w_in, l0_s5_b_re, l0_s5_b_im, l0_s5_c_re, l0_s5_c_im, l0_s5_lam_re, l0_s5_lam_im, l0_s5_log_dt, l0_s5_d, l0_s5_w_glu, l0_s5_w_out, l0_norm2_g, l0_moe_w_router, l0_moe_b_router, l0_moe_w_gate_up, l0_moe_b_gate_up, l0_moe_w_down, l0_moe_b_down, l1_norm1_g, l1_ada_w, l1_ada_b, l1_moba_w_qkv, l1_moba_w_o, l1_norm2_g, l1_moe_w_router, l1_moe_b_router, l1_moe_w_gate_up, l1_moe_b_gate_up, l1_moe_w_down, l1_moe_b_down, final_norm_g):
    bsz, seq, d = x.shape
    t = bsz * seq
    assert seq % TM_DENSE == 0 and seq % MOBA_BLOCK == 0 and seq % S5_CHUNK == 0
    assert S5_CHUNK % S5_SEG == 0 and seq % TD_ROWS == 0
    assert seq // MOBA_BLOCK <= LANES and d == N_HEADS * HEAD_DIM
    x2 = x.reshape(t, d)
    c_pad = jnp.zeros((8, d), F32).at[:bsz].set(c)

    def ada_parts(w, b):
        a = _ada(c_pad, w, b)[:bsz]
        return [a[:, i * d:(i + 1) * d].reshape(bsz, 1, d) for i in range(6)]

    sh1, sc1, g1, sh2, sc2, g2 = ada_parts(l0_ada_w, l0_ada_b)
    u = _nml(x2, l0_norm1_g, sc1, sh1, l0_s5_w_in.astype(BF16), seq, F32)
    rate, theta, bblk, cblk = _s5_params(l0_s5_b_re, l0_s5_b_im, l0_s5_c_re, l0_s5_c_im,
                                         l0_s5_lam_re, l0_s5_lam_im, l0_s5_log_dt)
    z = _s5_scan(u, rate, theta, bblk, cblk, l0_s5_d, bsz, seq)
    x2 = _row_call(_glu_out_kernel, z, x2, g1,
                   [l0_s5_w_glu.astype(BF16), l0_s5_w_out.astype(BF16)], seq)
    routed = _router(x2, l0_norm2_g, sc2, sh2, l0_moe_w_router, l0_moe_b_router, seq)
    x2 = _moe(x2, routed, g2, l0_moe_w_gate_up, l0_moe_b_gate_up, l0_moe_w_down, l0_moe_b_down,
              final_norm_g, seq, final=False)

    sh1, sc1, g1, sh2, sc2, g2 = ada_parts(l1_ada_w, l1_ada_b)
    qkv = _nml(x2, l1_norm1_g, sc1, sh1, l1_moba_w_qkv.astype(BF16), seq, BF16)
    cos, sin = _rope_tables(positions.reshape(t, 1))
    q, k, v, km = _rope(qkv, cos, sin, bsz, seq)
    nb = seq // MOBA_BLOCK
    nbp = -(-nb // 8) * 8
    km_pad = jnp.zeros((bsz, N_HEADS, nbp, HEAD_DIM), F32).at[:, :, :nb].set(km[:, :, :, 0])
    o = _moba(q, k, v, km_pad, bsz, seq)
    x2 = _row_call(_lin_res_kernel, o, x2, g1, [l1_moba_w_o.astype(BF16)], seq)
    routed = _router(x2, l1_norm2_g, sc2, sh2, l1_moe_w_router, l1_moe_b_router, seq)
    x2 = _moe(x2, routed, g2, l1_moe_w_gate_up, l1_moe_b_gate_up, l1_moe_w_down, l1_moe_b_down,
              final_norm_g, seq, final=True)
    return x2.reshape(bsz, seq, d)
```

```python
import functools
import math

import jax
import jax.numpy as jnp
from jax import lax
from jax.experimental import pallas as pl
from jax.experimental.pallas import tpu as pltpu

F32 = jnp.float32
BF16 = jnp.bfloat16
HI = lax.Precision.HIGHEST

EPS = 1e-6
NEG = -1e30
LOG2E = math.log2(math.e)
LANES = 128
MIB = 1024 * 1024

S5_GROUP = 16
S5_STATE = 64
N_HEADS = 8
HEAD_DIM = 128
MOBA_BLOCK = 256
MOBA_TOPK = 3
ROPE_THETA = 10000.0
N_EXPERTS = 32
TOP_K = 4
SWIGLU_LIMIT = 7.0
SWIGLU_ALPHA = 1.702

TM_DENSE = 512
S5_CHUNK = 256
S5_SEG = 32
S5_SUPER = 8
MOBA_HEADS_PER_STEP = 2
TM_EXPERT = 256
TD_ROWS = 256
TM_SRC = 2048


def _cparams(n_axes, vmem_mib):
    return pltpu.CompilerParams(dimension_semantics=("arbitrary",) * n_axes,
                                vmem_limit_bytes=vmem_mib * MIB)


def _pack_rows(x):
    n = x.shape[1] // 2
    xb = x.astype(BF16).astype(F32)
    lo = lax.bitcast_convert_type(xb[:, :n], jnp.uint32) >> 16
    hi = lax.bitcast_convert_type(xb[:, n:], jnp.uint32) & jnp.uint32(0xFFFF0000)
    return lo | hi


def _unpack_rows(p):
    lo = lax.bitcast_convert_type(p << 16, F32)
    hi = lax.bitcast_convert_type(p & jnp.uint32(0xFFFF0000), F32)
    return jnp.concatenate([lo, hi], axis=1)


def _normmod(x, g, sc, sh):
    ms = jnp.mean(x * x, axis=-1, keepdims=True)
    return (x * lax.rsqrt(ms + EPS)) * g * (1.0 + sc) + sh


def _ada_kernel(c_ref, w_ref, b_ref, o_ref):
    c = c_ref[...]
    ca = c * jax.nn.sigmoid(c)
    o_ref[...] = jnp.dot(ca, w_ref[...], preferred_element_type=F32, precision=HI) + b_ref[...]


def _ada(c_pad, w, b):
    d, n = w.shape
    tn = 1536
    return pl.pallas_call(
        _ada_kernel,
        grid=(n // tn,),
        in_specs=[pl.BlockSpec((8, d), lambda j: (0, 0)),
                  pl.BlockSpec((d, tn), lambda j: (0, j)),
                  pl.BlockSpec((1, tn), lambda j: (0, j))],
        out_specs=pl.BlockSpec((8, tn), lambda j: (0, j)),
        out_shape=jax.ShapeDtypeStruct((8, n), F32),
        compiler_params=_cparams(1, 32),
    )(c_pad, w, b.reshape(1, n))


def _nml_kernel(x_ref, g_ref, sc_ref, sh_ref, w_ref, o_ref):
    h = _normmod(x_ref[...], g_ref[...], sc_ref[0], sh_ref[0])
    o_ref[...] = jnp.dot(h.astype(BF16), w_ref[...], preferred_element_type=F32).astype(o_ref.dtype)


def _nml(x2, g, sc, sh, w_bf, seq, out_dtype):
    t, d = x2.shape
    n = w_bf.shape[1]
    tm = TM_DENSE
    tpb = seq // tm
    return pl.pallas_call(
        _nml_kernel,
        grid=(t // tm,),
        in_specs=[pl.BlockSpec((tm, d), lambda i: (i, 0)),
                  pl.BlockSpec((1, d), lambda i: (0, 0)),
                  pl.BlockSpec((1, 1, d), lambda i: (i // tpb, 0, 0)),
                  pl.BlockSpec((1, 1, d), lambda i: (i // tpb, 0, 0)),
                  pl.BlockSpec((d, n), lambda i: (0, 0))],
        out_specs=pl.BlockSpec((tm, n), lambda i: (i, 0)),
        out_shape=jax.ShapeDtypeStruct((t, n), out_dtype),
        compiler_params=_cparams(1, 48),
    )(x2, g.reshape(1, d), sc, sh, w_bf)


def _gelu_tanh(y):
    c = math.sqrt(2.0 / math.pi)
    return y * (0.5 * (1.0 + jnp.tanh(c * (y + 0.044715 * (y * y * y)))))


def _s5_kernel(u_ref, rate_ref, theta_ref, bblk_ref, cblk_ref, d_ref, z_ref,
               apr, api, a1r, a1i, air, aii, xr, xi):
    L = u_ref.shape[0]
    seg = apr.shape[0]
    n_super = bblk_ref.shape[0]
    half = apr.shape[1] // n_super
    nch = u_ref.shape[1] // n_super
    b = pl.program_id(0)
    c = pl.program_id(1)

    @pl.when((b == 0) & (c == 0))
    def _():
        tt = lax.broadcasted_iota(jnp.int32, (seg, 1), 0).astype(F32)
        ph = theta_ref[...] * tt
        rt = rate_ref[...] * tt
        cs = jnp.cos(ph)
        sn = jnp.sin(ph)
        mag = jnp.exp(rt)
        inv = jnp.exp(-rt)
        apr[...] = mag * cs
        api[...] = mag * sn
        air[...] = inv * cs
        aii[...] = -(inv * sn)
        ph1 = theta_ref[...] * (tt + 1.0)
        mag1 = jnp.exp(rate_ref[...] * (tt + 1.0))
        a1r[...] = mag1 * jnp.cos(ph1)
        a1i[...] = mag1 * jnp.sin(ph1)

    @pl.when(c == 0)
    def _():
        xr[...] = jnp.zeros_like(xr)
        xi[...] = jnp.zeros_like(xi)

    row = lax.broadcasted_iota(jnp.int32, (L, L), 0)
    col = lax.broadcasted_iota(jnp.int32, (L, L), 1)
    tril = ((row >= col) & (row // seg == col // seg)).astype(BF16)

    for sg in range(n_super):
        cols = slice(sg * half, (sg + 1) * half)
        ch = slice(sg * nch, (sg + 1) * nch)
        u = u_ref[:, ch]
        bu = jnp.dot(u.astype(BF16), bblk_ref[sg], preferred_element_type=F32)
        ir = air[:, cols]
        ii = aii[:, cols]
        zs = []
        for k in range(L // seg):
            bur = bu[k * seg:(k + 1) * seg, :half]
            bui = bu[k * seg:(k + 1) * seg, half:]
            zs.append(jnp.concatenate([bur * ir - bui * ii, bur * ii + bui * ir], axis=1))
        zc = jnp.concatenate(zs, axis=0).astype(BF16)
        cum = jnp.dot(tril, zc, preferred_element_type=F32)
        pw_r = apr[:, cols]
        pw_i = api[:, cols]
        p1_r = a1r[:, cols]
        p1_i = a1i[:, cols]
        last_r = xr[:, cols]
        last_i = xi[:, cols]
        xs = []
        for k in range(L // seg):
            cr = cum[k * seg:(k + 1) * seg, :half]
            ci = cum[k * seg:(k + 1) * seg, half:]
            x_re = cr * pw_r - ci * pw_i + (p1_r * last_r - p1_i * last_i)
            x_im = cr * pw_i + ci * pw_r + (p1_r * last_i + p1_i * last_r)
            last_r = x_re[seg - 1:seg, :]
            last_i = x_im[seg - 1:seg, :]
            xs.append(jnp.concatenate([x_re, x_im], axis=1))
        xr[:, cols] = last_r
        xi[:, cols] = last_i
        xc = jnp.concatenate(xs, axis=0).astype(BF16)
        y = jnp.dot(xc, cblk_ref[sg], preferred_element_type=F32) + d_ref[:, ch] * u
        z_ref[:, ch] = _gelu_tanh(y).astype(z_ref.dtype)


def _s5_scan(u, rate, theta, bblk, cblk, d_skip, bsz, seq):
    t, w = u.shape
    L = S5_CHUNK
    nc = seq // L
    ns = rate.shape[1]
    return pl.pallas_call(
        _s5_kernel,
        grid=(bsz, nc),
        in_specs=[pl.BlockSpec((L, w), lambda b, c: (b * nc + c, 0)),
                  pl.BlockSpec((1, ns), lambda b, c: (0, 0)),
                  pl.BlockSpec((1, ns), lambda b, c: (0, 0)),
                  pl.BlockSpec(bblk.shape, lambda b, c: (0, 0, 0)),
                  pl.BlockSpec(cblk.shape, lambda b, c: (0, 0, 0)),
                  pl.BlockSpec((1, w), lambda b, c: (0, 0))],
        out_specs=pl.BlockSpec((L, w), lambda b, c: (b * nc + c, 0)),
        out_shape=jax.ShapeDtypeStruct((t, w), BF16),
        scratch_shapes=[pltpu.VMEM((S5_SEG, ns), F32)] * 6 + [pltpu.VMEM((1, ns), F32)] * 2,
        compiler_params=_cparams(2, 48),
    )(u, rate, theta, bblk, cblk, d_skip.reshape(1, w))


def _s5_params(b_re, b_im, c_re, c_im, lam_re, lam_im, log_dt):
    g, p, ch = b_re.shape
    dt = jnp.exp(log_dt)[:, None]
    rate = lam_re * dt
    theta = lam_im * dt
    mag = jnp.exp(rate)
    a_re = mag * jnp.cos(theta)
    a_im = mag * jnp.sin(theta)
    den = lam_re * lam_re + lam_im * lam_im
    f_re = ((a_re - 1.0) * lam_re + a_im * lam_im) / den
    f_im = (a_im * lam_re - (a_re - 1.0) * lam_im) / den
    bb_re = f_re[..., None] * b_re - f_im[..., None] * b_im
    bb_im = f_re[..., None] * b_im + f_im[..., None] * b_re
    ns = S5_SUPER
    nsg = g // ns
    eye = jnp.eye(ns, dtype=F32)

    def blk_b(bb):
        return jnp.einsum('sgpc,gh->sgchp', bb.reshape(nsg, ns, p, ch), eye).reshape(nsg, ns * ch, ns * p)

    def blk_c(cc):
        return jnp.einsum('sgcp,gh->sgphc', cc.reshape(nsg, ns, ch, p), eye).reshape(nsg, ns * p, ns * ch)

    bblk = jnp.concatenate([blk_b(bb_re), blk_b(bb_im)], axis=2).astype(BF16)
    cblk = jnp.concatenate([blk_c(c_re), -blk_c(c_im)], axis=1).astype(BF16)
    return rate.reshape(1, g * p), theta.reshape(1, g * p), bblk, cblk


def _route(x, g, sc, sh, wr, br):
    h = _normmod(x, g, sc, sh).astype(BF16)
    logits = jnp.dot(h, wr, preferred_element_type=F32) + br
    lane = lax.broadcasted_iota(jnp.int32, logits.shape, 1)
    lane_f = lane.astype(F32)
    work = logits
    onehot = jnp.zeros(logits.shape, F32)
    ei = jnp.zeros(logits.shape, F32)
    vals = []
    for kk in range(TOP_K):
        m = jnp.max(work, axis=-1, keepdims=True)
        idx = jnp.min(jnp.where(work == m, lane_f, float(LANES)), axis=-1, keepdims=True)
        hit = lane_f == idx
        onehot = jnp.where(hit, 1.0, onehot)
        ei = jnp.where(lane == kk, idx, ei)
        vals.append(m)
        work = jnp.where(hit, 2.0 * NEG, work)
    ex = [jnp.exp(vv - vals[0]) for vv in vals]
    tot = ex[0] + ex[1] + ex[2] + ex[3]
    gt = jnp.zeros(logits.shape, F32)
    for kk in range(TOP_K):
        gt = jnp.where(lane == kk, ex[kk] / tot, gt)
    return _pack_rows(h), onehot, ei, gt


def _glu_out_kernel(z_ref, x_ref, g_ref, wg_ref, wo_ref, o_ref):
    z = z_ref[...]
    gate = jax.nn.sigmoid(jnp.dot(z, wg_ref[...], preferred_element_type=F32))
    m = jnp.dot((z.astype(F32) * gate).astype(BF16), wo_ref[...], preferred_element_type=F32)
    o_ref[...] = x_ref[...] + g_ref[0] * m


def _lin_res_kernel(a_ref, x_ref, g_ref, w_ref, o_ref):
    m = jnp.dot(a_ref[...].astype(BF16), w_ref[...], preferred_element_type=F32)
    o_ref[...] = x_ref[...] + g_ref[0] * m


def _row_call(kernel, acts, x2, gate, weights, seq):
    t, d = x2.shape
    tm = TM_DENSE
    tpb = seq // tm
    row = pl.BlockSpec((tm, d), lambda i: (i, 0))
    return pl.pallas_call(
        kernel,
        grid=(t // tm,),
        in_specs=[row, row, pl.BlockSpec((1, 1, d), lambda i: (i // tpb, 0, 0))]
                 + [pl.BlockSpec(w.shape, lambda i: (0, 0)) for w in weights],
        out_specs=row,
        out_shape=jax.ShapeDtypeStruct((t, d), F32),
        compiler_params=_cparams(1, 48),
    )(acts, x2, gate, *weights)


def _router_kernel(x_ref, g_ref, sc_ref, sh_ref, wr_ref, br_ref, h_ref, oh_ref, ei_ref, gt_ref):
    h_ref[...], oh_ref[...], ei_ref[...], gt_ref[...] = _route(
        x_ref[...], g_ref[...], sc_ref[0], sh_ref[0], wr_ref[...], br_ref[...])


def _router(x2, g, sc, sh, w_router, b_router, seq):
    t, d = x2.shape
    ne = w_router.shape[1]
    wr_pad = jnp.zeros((d, LANES), F32).at[:, :ne].set(w_router).astype(BF16)
    br_pad = jnp.full((1, LANES), NEG, F32).at[0, :ne].set(b_router)
    tm = TM_DENSE
    tpb = seq // tm
    lane_tile = pl.BlockSpec((tm, LANES), lambda i: (i, 0))
    per_batch = pl.BlockSpec((1, 1, d), lambda i: (i // tpb, 0, 0))
    return pl.pallas_call(
        _router_kernel,
        grid=(t // tm,),
        in_specs=[pl.BlockSpec((tm, d), lambda i: (i, 0)),
                  pl.BlockSpec((1, d), lambda i: (0, 0)), per_batch, per_batch,
                  pl.BlockSpec((d, LANES), lambda i: (0, 0)),
                  pl.BlockSpec((1, LANES), lambda i: (0, 0))],
        out_specs=[pl.BlockSpec((tm, d // 2), lambda i: (i, 0)), lane_tile, lane_tile, lane_tile],
        out_shape=[jax.ShapeDtypeStruct((t, d // 2), jnp.uint32)]
                  + [jax.ShapeDtypeStruct((t, LANES), F32)] * 3,
        compiler_params=_cparams(1, 32),
    )(x2, g.reshape(1, d), sc, sh, wr_pad, br_pad)


def _rope_tab_kernel(pos_ref, inv_ref, sgn_ref, cos_ref, sin_ref):
    ang = pos_ref[...].astype(F32) * inv_ref[...]
    cos_ref[...] = jnp.cos(ang)
    sin_ref[...] = jnp.sin(ang) * sgn_ref[...]


def _rope_tables(pos_col):
    t = pos_col.shape[0]
    ts = 512
    half = HEAD_DIM // 2
    inv = ROPE_THETA ** (-jnp.arange(0, HEAD_DIM, 2, dtype=F32) / HEAD_DIM)
    inv = jnp.concatenate([inv, inv]).reshape(1, HEAD_DIM)
    sgn = jnp.concatenate([-jnp.ones((half,), F32), jnp.ones((half,), F32)]).reshape(1, HEAD_DIM)
    const = pl.BlockSpec((1, HEAD_DIM), lambda i: (0, 0))
    tile = pl.BlockSpec((ts, HEAD_DIM), lambda i: (i, 0))
    return pl.pallas_call(
        _rope_tab_kernel,
        grid=(t // ts,),
        in_specs=[pl.BlockSpec((ts, 1), lambda i: (i, 0)), const, const],
        out_specs=[tile, tile],
        out_shape=[jax.ShapeDtypeStruct((t, HEAD_DIM), F32)] * 2,
        compiler_params=_cparams(1, 16),
    )(pos_col, inv, sgn)


def _rope_kernel(q_ref, k_ref, v_ref, cos_ref, sin_ref, qo_ref, ko_ref, vo_ref, km_ref):
    cs = cos_ref[...]
    sn = sin_ref[...]
    dh = cs.shape[1]
    half = dh // 2
    for hh in range(qo_ref.shape[1]):
        cols = slice(hh * dh, (hh + 1) * dh)
        q = q_ref[:, cols].astype(F32)
        k = k_ref[:, cols].astype(F32)
        qr = q * cs + pltpu.roll(q, half, 1) * sn
        kr = k * cs + pltpu.roll(k, half, 1) * sn
        qo_ref[0, hh] = qr
        ko_ref[0, hh] = kr.astype(BF16)
        vo_ref[0, hh, 0] = v_ref[:, cols].astype(F32).T.astype(BF16)
        km_ref[0, hh, 0] = jnp.mean(kr, axis=0, keepdims=True)


def _rope(qkv, cos, sin, bsz, seq):
    t = qkv.shape[0]
    h, dh, blk = N_HEADS, HEAD_DIM, MOBA_BLOCK
    nb = seq // blk
    part = lambda p: pl.BlockSpec((blk, h * dh), lambda i: (i, p))
    tab = pl.BlockSpec((blk, dh), lambda i: (i, 0))
    head = pl.BlockSpec((1, h, blk, dh), lambda i: (i // nb, 0, i % nb, 0))
    return pl.pallas_call(
        _rope_kernel,
        grid=(t // blk,),
        in_specs=[part(0), part(1), part(2), tab, tab],
        out_specs=[head, head,
                   pl.BlockSpec((1, h, 1, dh, blk), lambda i: (i // nb, 0, i % nb, 0, 0)),
                   pl.BlockSpec((1, h, 1, 1, dh), lambda i: (i // nb, 0, i % nb, 0, 0))],
        out_shape=[jax.ShapeDtypeStruct((bsz, h, seq, dh), F32),
                   jax.ShapeDtypeStruct((bsz, h, seq, dh), BF16),
                   jax.ShapeDtypeStruct((bsz, h, nb, dh, blk), BF16),
                   jax.ShapeDtypeStruct((bsz, h, nb, 1, dh), F32)],
        compiler_params=_cparams(1, 32),
    )(qkv, qkv, qkv, cos, sin)


def _moba_kernel(q_ref, k_ref, vt_ref, km_ref, o_ref, sel_scr, s_a, s_b, s_own):
    nh = q_ref.shape[1]
    blk = q_ref.shape[2]
    dh = q_ref.shape[3]
    j = pl.program_id(2)
    nt = (((1,), (1,)), ((), ()))
    heads = range(nh)

    qs = []
    for hh in heads:
        q = q_ref[0, hh]
        gate = lax.dot_general(km_ref[0, hh].astype(BF16), q.astype(BF16), nt,
                               preferred_element_type=F32)
        bidx = lax.broadcasted_iota(jnp.int32, gate.shape, 0)
        bidx_f = bidx.astype(F32)
        work = jnp.where(bidx < j, gate, NEG)
        sel = jnp.zeros(gate.shape, F32)
        for _ in range(MOBA_TOPK):
            m = jnp.max(work, axis=0, keepdims=True)
            idx = jnp.min(jnp.where(work == m, bidx_f, float(LANES)), axis=0, keepdims=True)
            hit = bidx_f == idx
            sel = jnp.where(hit & (m > 0.5 * NEG), 1.0, sel)
            work = jnp.where(hit, 2.0 * NEG, work)
        sel_scr[hh] = sel
        qs.append((q * (dh ** -0.5 * LOG2E)).astype(BF16))

    def scores(hh, n):
        start = pl.multiple_of(n * blk, blk)
        kb = k_ref[0, hh, pl.ds(start, blk), :]
        return lax.dot_general(kb, qs[hh], nt, preferred_element_type=F32)

    nb_last = vt_ref.shape[2] - 1

    def score_trip(scr, nn):
        for hh in heads:
            for bb in range(2):
                scr[hh, bb] = scores(hh, jnp.minimum(2 * nn + bb, nb_last))

    def past_pair(hh, nn, scr, m_i, l_i, acc):
        n0 = jnp.minimum(2 * nn, nb_last)
        n1 = jnp.minimum(2 * nn + 1, nb_last)
        s0 = scr[hh, 0]
        s1 = scr[hh, 1]
        r0 = (sel_scr[hh, pl.ds(n0, 1), :] > 0.5) & (2 * nn < j)
        r1 = (sel_scr[hh, pl.ds(n1, 1), :] > 0.5) & (2 * nn + 1 < j)
        rm0 = jnp.where(r0, jnp.max(s0, axis=0, keepdims=True), NEG)
        rm1 = jnp.where(r1, jnp.max(s1, axis=0, keepdims=True), NEG)
        m_new = jnp.maximum(m_i, jnp.maximum(rm0, rm1))
        alpha = jnp.exp2(m_i - m_new)
        p0 = jnp.exp2(s0 - jnp.where(r0, m_new, -NEG))
        p1 = jnp.exp2(s1 - jnp.where(r1, m_new, -NEG))
        l_new = alpha * l_i + (jnp.sum(p0, axis=0, keepdims=True) + jnp.sum(p1, axis=0, keepdims=True))
        pv = (jnp.dot(vt_ref[0, hh, n0], p0.astype(BF16), preferred_element_type=F32)
              + jnp.dot(vt_ref[0, hh, n1], p1.astype(BF16), preferred_element_type=F32))
        return m_new, l_new, alpha * acc + pv

    def body(mm, state):
        score_trip(s_b, 2 * mm + 1)
        state = tuple(past_pair(hh, 2 * mm, s_a, *state[hh]) for hh in heads)
        score_trip(s_a, 2 * mm + 2)
        return tuple(past_pair(hh, 2 * mm + 1, s_b, *state[hh]) for hh in heads)

    init = tuple((jnp.full((1, blk), NEG, F32), jnp.zeros((1, blk), F32), jnp.zeros((dh, blk), F32))
                 for _ in heads)
    score_trip(s_a, 0)
    for hh in heads:
        s_own[hh] = scores(hh, j)
    n_trips = (j + 1) // 2
    carry = lax.fori_loop(0, (n_trips + 1) // 2, body, init)

    for hh in heads:
        m_i, l_i, acc = carry[hh]
        s = s_own[hh]
        kk = lax.broadcasted_iota(jnp.int32, s.shape, 0)
        qq = lax.broadcasted_iota(jnp.int32, s.shape, 1)
        s = jnp.where(kk <= qq, s, NEG)
        m_new = jnp.maximum(m_i, jnp.max(s, axis=0, keepdims=True))
        alpha = jnp.exp2(m_i - m_new)
        p = jnp.exp2(s - m_new)
        l_i = alpha * l_i + jnp.sum(p, axis=0, keepdims=True)
        acc = alpha * acc + jnp.dot(vt_ref[0, hh, j], p.astype(BF16), preferred_element_type=F32)
        o_ref[:, hh * dh:(hh + 1) * dh] = (acc / l_i).T.astype(o_ref.dtype)


def _moba(q, k, vt, km_pad, bsz, seq):
    h, dh, blk = N_HEADS, HEAD_DIM, MOBA_BLOCK
    nb = seq // blk
    nbp = km_pad.shape[2]
    nh = MOBA_HEADS_PER_STEP
    return pl.pallas_call(
        _moba_kernel,
        grid=(bsz, h // nh, nb),
        in_specs=[pl.BlockSpec((1, nh, blk, dh), lambda b, j, i: (b, j, i, 0)),
                  pl.BlockSpec((1, nh, seq, dh), lambda b, j, i: (b, j, 0, 0)),
                  pl.BlockSpec((1, nh, nb, dh, blk), lambda b, j, i: (b, j, 0, 0, 0)),
                  pl.BlockSpec((1, nh, nbp, dh), lambda b, j, i: (b, j, 0, 0))],
        out_specs=pl.BlockSpec((blk, nh * dh), lambda b, j, i: (b * nb + i, j)),
        out_shape=jax.ShapeDtypeStruct((bsz * seq, h * dh), BF16),
        scratch_shapes=[pltpu.VMEM((nh, nbp, blk), F32)] + [pltpu.VMEM((nh, 2, blk, blk), F32)] * 2
                       + [pltpu.VMEM((nh, blk, blk), F32)],
        compiler_params=_cparams(3, 48),
    )(q, k, vt, km_pad)


def _pos_kernel(oh_ref, ei_ref, dest_ref, meta_ref, cnt_scr, off_scr, run_scr):
    p = pl.program_id(0)
    i = pl.program_id(1)
    tm = oh_ref.shape[0]
    oh = oh_ref[...]
    colsum = jnp.sum(oh, axis=0, keepdims=True)

    @pl.when((p == 0) & (i == 0))
    def _():
        cnt_scr[...] = jnp.zeros_like(cnt_scr)

    @pl.when(p == 0)
    def _():
        cnt_scr[...] += colsum
        dest_ref[...] = jnp.zeros_like(dest_ref)

    @pl.when((p == 1) & (i == 0))
    def _():
        cnt = cnt_scr[...]
        padded = jnp.ceil(cnt * (1.0 / TM_EXPERT)) * float(TM_EXPERT)
        r = lax.broadcasted_iota(jnp.int32, (LANES, LANES), 0)
        cidx = lax.broadcasted_iota(jnp.int32, (LANES, LANES), 1)
        tri = (r < cidx).astype(F32)
        off = jnp.dot(jnp.broadcast_to(padded, (8, LANES)), tri,
                      preferred_element_type=F32, precision=HI)
        off_scr[...] = off[0:1]
        run_scr[...] = jnp.zeros_like(run_scr)

    @pl.when(p == 1)
    def _():
        r = lax.broadcasted_iota(jnp.int32, (tm, tm), 0)
        cidx = lax.broadcasted_iota(jnp.int32, (tm, tm), 1)
        ltri = (r > cidx).astype(BF16)
        excl = jnp.dot(ltri, oh.astype(BF16), preferred_element_type=F32)
        dfull = off_scr[...] + run_scr[...] + excl
        ei = ei_ref[...]
        lane = lax.broadcasted_iota(jnp.int32, oh.shape, 1)
        lane_f = lane.astype(F32)
        dest = jnp.zeros(oh.shape, F32)
        for kk in range(TOP_K):
            ek = jnp.sum(jnp.where(lane == kk, ei, 0.0), axis=-1, keepdims=True)
            dk = jnp.sum(jnp.where(lane_f == ek, dfull, 0.0), axis=-1, keepdims=True)
            dest = jnp.where(lane == kk, dk, dest)
        dest_ref[...] = dest.astype(jnp.int32)
        run_scr[...] += colsum

    meta_ref[...] = jnp.concatenate([off_scr[...], cnt_scr[...],
                                     jnp.zeros((6, LANES), F32)], axis=0)


def _positions(onehot, ei):
    t = onehot.shape[0]
    tm = TM_DENSE
    tile = pl.BlockSpec((tm, LANES), lambda p, i: (i, 0))
    return pl.pallas_call(
        _pos_kernel,
        grid=(2, t // tm),
        in_specs=[tile, tile],
        out_specs=[pl.BlockSpec((tm, LANES), lambda p, i: (i * p, 0)),
                   pl.BlockSpec((8, LANES), lambda p, i: (0, 0))],
        out_shape=[jax.ShapeDtypeStruct((t, LANES), jnp.int32),
                   jax.ShapeDtypeStruct((8, LANES), F32)],
        scratch_shapes=[pltpu.VMEM((1, LANES), F32)] * 3,
        compiler_params=_cparams(2, 16),
    )(onehot, ei)


def _src_kernel(dest_ref, pad_lo_ref, pad_hi_ref, src_ref):
    i = pl.program_id(0)
    tb = dest_ref.shape[0] // TOP_K

    @pl.when(i == 0)
    def _():
        def zero(j, carry):
            src_ref[j] = 0
            return carry

        def per_range(rr, carry):
            lax.fori_loop(pad_lo_ref[rr], pad_hi_ref[rr], zero, 0)
            return carry
        lax.fori_loop(0, pad_lo_ref.shape[0], per_range, 0)

    def body(tt, carry):
        for kk in range(TOP_K):
            src_ref[dest_ref[tt * TOP_K + kk]] = i * tb + tt
        return carry

    lax.fori_loop(0, tb, body, 0, unroll=8)


def _sources(dest_flat, pad_lo, pad_hi, n_rows):
    n = dest_flat.shape[0]
    tb = TM_SRC
    whole = pl.BlockSpec(memory_space=pltpu.SMEM)
    return pl.pallas_call(
        _src_kernel,
        grid=(n // (tb * TOP_K),),
        in_specs=[pl.BlockSpec((tb * TOP_K,), lambda i: (i,), memory_space=pltpu.SMEM), whole, whole],
        out_specs=pl.BlockSpec(memory_space=pltpu.SMEM),
        out_shape=jax.ShapeDtypeStruct((n_rows,), jnp.int32),
        compiler_params=_cparams(1, 16),
    )(dest_flat, pad_lo, pad_hi)


def _expert_kernel(te_ref, tv_ref, nx_ref, src_cur, src_nxt, h_hbm, wgu_hbm, bgu_ref, wd_hbm, bd_ref,
                   o_ref, xbuf, sem, wgu_st, wd_st, wsem, wgu_bf, wd_bf):
    i = pl.program_id(0)
    nt = pl.num_programs(0)
    tm = o_ref.shape[0]
    slot = lax.rem(i, 2)
    e = te_ref[i]
    prev = te_ref[jnp.maximum(i - 1, 0)]
    dff = wd_hbm.shape[1]

    def weight_copies(ex):
        return (pltpu.make_async_copy(wgu_hbm.at[ex], wgu_st, wsem.at[0]),
                pltpu.make_async_copy(wd_hbm.at[ex], wd_st, wsem.at[1]))

    def gather(src_ref, s):
        def body(r8, carry):
            for uu in range(8):
                pltpu.make_async_copy(h_hbm.at[pl.ds(src_ref[r8 * 8 + uu], 1)],
                                      xbuf.at[s, r8, pl.ds(uu, 1)], sem.at[s]).start()
            return carry
        lax.fori_loop(0, tm // 8, body, 0)

    @pl.when(i == 0)
    def _():
        for cp in weight_copies(e):
            cp.start(priority=1)

    @pl.when((i == 0) & (tv_ref[0] > 0))
    def _():
        gather(src_cur, 0)

    nxt = jnp.minimum(i + 1, nt - 1)

    @pl.when((i + 1 < nt) & (tv_ref[nxt] > 0))
    def _():
        gather(src_nxt, 1 - slot)

    @pl.when((i == 0) | (e != prev))
    def _():
        for cp in weight_copies(e):
            cp.wait()
        wgu_bf[...] = wgu_st[...].astype(BF16)
        wd_bf[...] = wd_st[...].astype(BF16)

        @pl.when(nx_ref[i] >= 0)
        def _():
            for cp in weight_copies(nx_ref[i]):
                cp.start(priority=1)

    @pl.when(tv_ref[i] > 0)
    def _():
        for r8 in range(tm // 8):
            pltpu.make_async_copy(h_hbm.at[pl.ds(0, 8)], xbuf.at[slot, r8], sem.at[slot]).wait()
        x = _unpack_rows(xbuf[slot].reshape(tm, xbuf.shape[3])).astype(BF16)
        hu = jnp.dot(x, wgu_bf[...], preferred_element_type=F32) + bgu_ref[0]
        g = jnp.minimum(hu[:, :dff], SWIGLU_LIMIT)
        up = jnp.clip(hu[:, dff:], -SWIGLU_LIMIT, SWIGLU_LIMIT)
        act = (up + 1.0) * g * jax.nn.sigmoid(SWIGLU_ALPHA * g)
        y = jnp.dot(act.astype(BF16), wd_bf[...], preferred_element_type=F32) + bd_ref[0]
        o_ref[...] = _pack_rows(y)

    @pl.when(tv_ref[i] == 0)
    def _():
        o_ref[...] = jnp.zeros_like(o_ref)


def _experts(tile_e, tile_v, next_e, src, h2p, w_gu, b_gu, w_d, b_d):
    n_rows = src.shape[0]
    dp = h2p.shape[1]
    ne, d, f2 = w_gu.shape
    dff = w_d.shape[1]
    tm = TM_EXPERT
    nt = n_rows // tm
    grid_spec = pltpu.PrefetchScalarGridSpec(
        num_scalar_prefetch=3,
        grid=(nt,),
        in_specs=[pl.BlockSpec((tm,), lambda i, te, tv, nx: (i,), memory_space=pltpu.SMEM),
                  pl.BlockSpec((tm,), lambda i, te, tv, nx: (jnp.minimum(i + 1, nt - 1),),
                               memory_space=pltpu.SMEM),
                  pl.BlockSpec(memory_space=pl.ANY),
                  pl.BlockSpec(memory_space=pl.ANY),
                  pl.BlockSpec((1, 1, f2), lambda i, te, tv, nx: (te[i], 0, 0)),
                  pl.BlockSpec(memory_space=pl.ANY),
                  pl.BlockSpec((1, 1, d), lambda i, te, tv, nx: (te[i], 0, 0))],
        out_specs=pl.BlockSpec((tm, dp), lambda i, te, tv, nx: (i, 0)),
        scratch_shapes=[pltpu.VMEM((2, tm // 8, 8, dp), jnp.uint32), pltpu.SemaphoreType.DMA((2,)),
                        pltpu.VMEM((d, f2), F32), pltpu.VMEM((dff, d), F32),
                        pltpu.SemaphoreType.DMA((2,)),
                        pltpu.VMEM((d, f2), BF16), pltpu.VMEM((dff, d), BF16)],
    )
    return pl.pallas_call(
        _expert_kernel,
        grid_spec=grid_spec,
        out_shape=jax.ShapeDtypeStruct((n_rows, dp), jnp.uint32),
        compiler_params=_cparams(1, 48),
    )(tile_e, tile_v, next_e, src, src, h2p, w_gu, b_gu.reshape(ne, 1, f2), w_d, b_d.reshape(ne, 1, d))


def _combine_kernel(dcur, dnxt, gt_ref, x_ref, g_ref, gf_ref, ys_hbm, o_ref, buf, sem, *, final):
    i = pl.program_id(0)
    nt = pl.num_programs(0)
    td = x_ref.shape[0]
    slot = lax.rem(i, 2)

    def gather(dref, s):
        def body(t8, carry):
            for uu in range(8):
                for kk in range(TOP_K):
                    pltpu.make_async_copy(ys_hbm.at[pl.ds(dref[(t8 * 8 + uu) * TOP_K + kk], 1)],
                                          buf.at[s, kk, t8, pl.ds(uu, 1)], sem.at[s]
                                          ).start(priority=kk % 2)
            return carry
        lax.fori_loop(0, td // 8, body, 0)

    @pl.when(i == 0)
    def _():
        gather(dcur, 0)

    @pl.when(i + 1 < nt)
    def _():
        gather(dnxt, 1 - slot)

    for kk in range(TOP_K):
        for t8 in range(td // 8):
            pltpu.make_async_copy(ys_hbm.at[pl.ds(0, 8)], buf.at[slot, kk, t8], sem.at[slot]).wait()
    gt = gt_ref[...]
    rows = lambda kk: _unpack_rows(buf[slot, kk].reshape(td, buf.shape[4]))
    y = gt[:, 0:1] * rows(0)
    for kk in range(1, TOP_K):
        y = y + gt[:, kk:kk + 1] * rows(kk)
    xn = x_ref[...] + g_ref[0] * y
    if final:
        ms = jnp.mean(xn * xn, axis=-1, keepdims=True)
        xn = (xn * lax.rsqrt(ms + EPS)) * gf_ref[...]
    o_ref[...] = xn


def _combine(dest_flat, gates, x2, gate_vec, gf, ys, seq, final):
    t, d = x2.shape
    td = TD_ROWS
    tpb = seq // td
    nt = t // td
    return pl.pallas_call(
        functools.partial(_combine_kernel, final=final),
        grid=(nt,),
        in_specs=[pl.BlockSpec((td * TOP_K,), lambda i: (i,), memory_space=pltpu.SMEM),
                  pl.BlockSpec((td * TOP_K,), lambda i: (jnp.minimum(i + 1, nt - 1),),
                               memory_space=pltpu.SMEM),
                  pl.BlockSpec((td, LANES), lambda i: (i, 0)),
                  pl.BlockSpec((td, d), lambda i: (i, 0)),
                  pl.BlockSpec((1, 1, d), lambda i: (i // tpb, 0, 0)),
                  pl.BlockSpec((1, d), lambda i: (0, 0)),
                  pl.BlockSpec(memory_space=pl.ANY)],
        out_specs=pl.BlockSpec((td, d), lambda i: (i, 0)),
        out_shape=jax.ShapeDtypeStruct((t, d), F32),
        scratch_shapes=[pltpu.VMEM((2, TOP_K, td // 8, 8, d // 2), jnp.uint32),
                        pltpu.SemaphoreType.DMA((2,))],
        compiler_params=_cparams(1, 32),
    )(dest_flat, dest_flat, gates, x2, gate_vec, gf.reshape(1, d), ys)


def _moe(x2, routed, gate_vec, w_gu, b_gu, w_d, b_d, gf, seq, final):
    t, d = x2.shape
    ne = w_gu.shape[0]
    h2, onehot, ei, gates = routed
    dest, meta = _positions(onehot, ei)
    dest_flat = dest[:, :TOP_K].reshape(t * TOP_K)

    tm = TM_EXPERT
    n_rows = t * TOP_K + ne * tm
    n_tiles = n_rows // tm
    off = meta[0, :ne].astype(jnp.int32)
    cnt = meta[1, :ne].astype(jnp.int32)
    ends = off + ((cnt + tm - 1) // tm) * tm
    starts = jnp.arange(n_tiles, dtype=jnp.int32) * tm
    valid = starts < ends[-1]
    te = jnp.sum((starts[:, None] >= ends[None, :]).astype(jnp.int32), axis=1)
    te_last = jnp.sum((ends[-1] - 1 >= ends).astype(jnp.int32))
    tile_e = jnp.minimum(jnp.where(valid, te, te_last), ne - 1)
    tile_v = valid.astype(jnp.int32)
    eidx = jnp.arange(ne, dtype=jnp.int32)
    later_used = (eidx[None, :] > eidx[:, None]) & (cnt[None, :] > 0)
    next_of = jnp.min(jnp.where(later_used, eidx[None, :], ne), axis=1)
    next_of = jnp.where(next_of == ne, -1, next_of)
    next_e = jnp.sum(jnp.where(tile_e[:, None] == eidx[None, :], next_of[None, :], 0), axis=1)

    pad_lo = jnp.concatenate([off + cnt, ends[-1:]]).astype(jnp.int32)
    pad_hi = jnp.concatenate([ends, jnp.full((1,), n_rows, jnp.int32)]).astype(jnp.int32)
    src = _sources(dest_flat, pad_lo, pad_hi, n_rows)
    ys = _experts(tile_e, tile_v, next_e.astype(jnp.int32), src, h2, w_gu, b_gu, w_d, b_d)
    return _combine(dest_flat, gates, x2, gate_vec, gf, ys, seq, final)


def kernel(x, c, positions, l0_norm1_g, l0_ada_w, l0_ada_b, l0_s5_w_in, l0_s5_b_re, l0_s5_b_im, l0_s5_c_re, l0_s5_c_im, l0_s5_lam_re, l0_s5_lam_im, l0_s5_log_dt, l0_s5_d, l0_s5_w_glu, l0_s5_w_out, l0_norm2_g, l0_moe_w_router, l0_moe_b_router, l0_moe_w_gate_up, l0_moe_b_gate_up, l0_moe_w_down, l0_moe_b_down, l1_norm1_g, l1_ada_w, l1_ada_b, l1_moba_w_qkv, l1_moba_w_o, l1_norm2_g, l1_moe_w_router, l1_moe_b_router, l1_moe_w_gate_up, l1_moe_b_gate_up, l1_moe_w_down, l1_moe_b_down, final_norm_g):
    bsz, seq, d = x.shape
    t = bsz * seq
    assert seq % TM_DENSE == 0 and seq % MOBA_BLOCK == 0 and seq % S5_CHUNK == 0
    assert S5_CHUNK % S5_SEG == 0 and seq % TD_ROWS == 0
    assert seq // MOBA_BLOCK <= LANES and d == N_HEADS * HEAD_DIM
    x2 = x.reshape(t, d)
    c_pad = jnp.zeros((8, d), F32).at[:bsz].set(c)

    def ada_parts(w, b):
        a = _ada(c_pad, w, b)[:bsz]
        return [a[:, i * d:(i + 1) * d].reshape(bsz, 1, d) for i in range(6)]

    sh1, sc1, g1, sh2, sc2, g2 = ada_parts(l0_ada_w, l0_ada_b)
    u = _nml(x2, l0_norm1_g, sc1, sh1, l0_s5_w_in.astype(BF16), seq, F32)
    rate, theta, bblk, cblk = _s5_params(l0_s5_b_re, l0_s5_b_im, l0_s5_c_re, l0_s5_c_im,
                                         l0_s5_lam_re, l0_s5_lam_im, l0_s5_log_dt)
    z = _s5_scan(u, rate, theta, bblk, cblk, l0_s5_d, bsz, seq)
    x2 = _row_call(_glu_out_kernel, z, x2, g1,
                   [l0_s5_w_glu.astype(BF16), l0_s5_w_out.astype(BF16)], seq)
    routed = _router(x2, l0_norm2_g, sc2, sh2, l0_moe_w_router, l0_moe_b_router, seq)
    x2 = _moe(x2, routed, g2, l0_moe_w_gate_up, l0_moe_b_gate_up, l0_moe_w_down, l0_moe_b_down,
              final_norm_g, seq, final=False)

    sh1, sc1, g1, sh2, sc2, g2 = ada_parts(l1_ada_w, l1_ada_b)
    qkv = _nml(x2, l1_norm1_g, sc1, sh1, l1_moba_w_qkv.astype(BF16), seq, BF16)
    cos, sin = _rope_tables(positions.reshape(t, 1))
    q, k, v, km = _rope(qkv, cos, sin, bsz, seq)
    nb = seq // MOBA_BLOCK
    nbp = -(-nb // 8) * 8
    km_pad = jnp.zeros((bsz, N_HEADS, nbp, HEAD_DIM), F32).at[:, :, :nb].set(km[:, :, :, 0])
    o = _moba(q, k, v, km_pad, bsz, seq)
    x2 = _row_call(_lin_res_kernel, o, x2, g1, [l1_moba_w_o.astype(BF16)], seq)
    routed = _router(x2, l1_norm2_g, sc2, sh2, l1_moe_w_router, l1_moe_b_router, seq)
    x2 = _moe(x2, routed, g2, l1_moe_w_gate_up, l1_moe_b_gate_up, l1_moe_w_down, l1_moe_b_down,
              final_norm_g, seq, final=True)
    return x2.reshape(bsz, seq, d)
```

```python
import functools
import math

import jax
import jax.numpy as jnp
from jax import lax
from jax.experimental import pallas as pl
from jax.experimental.pallas import tpu as pltpu

F32 = jnp.float32
BF16 = jnp.bfloat16
HI = lax.Precision.HIGHEST

EPS = 1e-6
NEG = -1e30
LOG2E = math.log2(math.e)
LANES = 128
MIB = 1024 * 1024

S5_GROUP = 16
S5_STATE = 64
N_HEADS = 8
HEAD_DIM = 128
MOBA_BLOCK = 256
MOBA_TOPK = 3
ROPE_THETA = 10000.0
N_EXPERTS = 32
TOP_K = 4
SWIGLU_LIMIT = 7.0
SWIGLU_ALPHA = 1.702

TM_DENSE = 512
S5_CHUNK = 256
S5_SEG = 32
S5_SUPER = 8
MOBA_HEADS_PER_STEP = 2
TM_EXPERT = 256
TD_ROWS = 256
TM_SRC = 2048


def _cparams(n_axes, vmem_mib):
    return pltpu.CompilerParams(dimension_semantics=("arbitrary",) * n_axes,
                                vmem_limit_bytes=vmem_mib * MIB)


def _pack_rows(x):
    n = x.shape[1] // 2
    xb = x.astype(BF16).astype(F32)
    lo = lax.bitcast_convert_type(xb[:, :n], jnp.uint32) >> 16
    hi = lax.bitcast_convert_type(xb[:, n:], jnp.uint32) & jnp.uint32(0xFFFF0000)
    return lo | hi


def _unpack_rows(p):
    lo = lax.bitcast_convert_type(p << 16, F32)
    hi = lax.bitcast_convert_type(p & jnp.uint32(0xFFFF0000), F32)
    return jnp.concatenate([lo, hi], axis=1)


def _normmod(x, g, sc, sh):
    ms = jnp.mean(x * x, axis=-1, keepdims=True)
    return (x * lax.rsqrt(ms + EPS)) * g * (1.0 + sc) + sh


def _ada_kernel(c_ref, w_ref, b_ref, o_ref):
    c = c_ref[...]
    ca = c * jax.nn.sigmoid(c)
    o_ref[...] = jnp.dot(ca, w_ref[...], preferred_element_type=F32, precision=HI) + b_ref[...]


def _ada(c_pad, w, b):
    d, n = w.shape
    tn = 1536
    return pl.pallas_call(
        _ada_kernel,
        grid=(n // tn,),
        in_specs=[pl.BlockSpec((8, d), lambda j: (0, 0)),
                  pl.BlockSpec((d, tn), lambda j: (0, j)),
                  pl.BlockSpec((1, tn), lambda j: (0, j))],
        out_specs=pl.BlockSpec((8, tn), lambda j: (0, j)),
        out_shape=jax.ShapeDtypeStruct((8, n), F32),
        compiler_params=_cparams(1, 32),
    )(c_pad, w, b.reshape(1, n))


def _nml_kernel(x_ref, g_ref, sc_ref, sh_ref, w_ref, o_ref):
    h = _normmod(x_ref[...], g_ref[...], sc_ref[0], sh_ref[0])
    o_ref[...] = jnp.dot(h.astype(BF16), w_ref[...], preferred_element_type=F32).astype(o_ref.dtype)


def _nml(x2, g, sc, sh, w_bf, seq, out_dtype):
    t, d = x2.shape
    n = w_bf.shape[1]
    tm = TM_DENSE
    tpb = seq // tm
    return pl.pallas_call(
        _nml_kernel,
        grid=(t // tm,),
        in_specs=[pl.BlockSpec((tm, d), lambda i: (i, 0)),
                  pl.BlockSpec((1, d), lambda i: (0, 0)),
                  pl.BlockSpec((1, 1, d), lambda i: (i // tpb, 0, 0)),
                  pl.BlockSpec((1, 1, d), lambda i: (i // tpb, 0, 0)),
                  pl.BlockSpec((d, n), lambda i: (0, 0))],
        out_specs=pl.BlockSpec((tm, n), lambda i: (i, 0)),
        out_shape=jax.ShapeDtypeStruct((t, n), out_dtype),
        compiler_params=_cparams(1, 48),
    )(x2, g.reshape(1, d), sc, sh, w_bf)


def _gelu_tanh(y):
    c = math.sqrt(2.0 / math.pi)
    return y * (0.5 * (1.0 + jnp.tanh(c * (y + 0.044715 * (y * y * y)))))


def _s5_kernel(u_ref, rate_ref, theta_ref, bblk_ref, cblk_ref, d_ref, z_ref,
               apr, api, a1r, a1i, air, aii, xr, xi):
    L = u_ref.shape[0]
    seg = apr.shape[0]
    n_super = bblk_ref.shape[0]
    half = apr.shape[1] // n_super
    nch = u_ref.shape[1] // n_super
    b = pl.program_id(0)
    c = pl.program_id(1)

    @pl.when((b == 0) & (c == 0))
    def _():
        tt = lax.broadcasted_iota(jnp.int32, (seg, 1), 0).astype(F32)
        ph = theta_ref[...] * tt
        rt = rate_ref[...] * tt
        cs = jnp.cos(ph)
        sn = jnp.sin(ph)
        mag = jnp.exp(rt)
        inv = jnp.exp(-rt)
        apr[...] = mag * cs
        api[...] = mag * sn
        air[...] = inv * cs
        aii[...] = -(inv * sn)
        ph1 = theta_ref[...] * (tt + 1.0)
        mag1 = jnp.exp(rate_ref[...] * (tt + 1.0))
        a1r[...] = mag1 * jnp.cos(ph1)
        a1i[...] = mag1 * jnp.sin(ph1)

    @pl.when(c == 0)
    def _():
        xr[...] = jnp.zeros_like(xr)
        xi[...] = jnp.zeros_like(xi)

    row = lax.broadcasted_iota(jnp.int32, (L, L), 0)
    col = lax.broadcasted_iota(jnp.int32, (L, L), 1)
    tril = ((row >= col) & (row // seg == col // seg)).astype(BF16)

    for sg in range(n_super):
        cols = slice(sg * half, (sg + 1) * half)
        ch = slice(sg * nch, (sg + 1) * nch)
        u = u_ref[:, ch]
        bu = jnp.dot(u.astype(BF16), bblk_ref[sg], preferred_element_type=F32)
        ir = air[:, cols]
        ii = aii[:, cols]
        zs = []
        for k in range(L // seg):
            bur = bu[k * seg:(k + 1) * seg, :half]
            bui = bu[k * seg:(k + 1) * seg, half:]
            zs.append(jnp.concatenate([bur * ir - bui * ii, bur * ii + bui * ir], axis=1))
        zc = jnp.concatenate(zs, axis=0).astype(BF16)
        cum = jnp.dot(tril, zc, preferred_element_type=F32)
        pw_r = apr[:, cols]
        pw_i = api[:, cols]
        p1_r = a1r[:, cols]
        p1_i = a1i[:, cols]
        last_r = xr[:, cols]
        last_i = xi[:, cols]
        xs = []
        for k in range(L // seg):
            cr = cum[k * seg:(k + 1) * seg, :half]
            ci = cum[k * seg:(k + 1) * seg, half:]
            x_re = cr * pw_r - ci * pw_i + (p1_r * last_r - p1_i * last_i)
            x_im = cr * pw_i + ci * pw_r + (p1_r * last_i + p1_i * last_r)
            last_r = x_re[seg - 1:seg, :]
            last_i = x_im[seg - 1:seg, :]
            xs.append(jnp.concatenate([x_re, x_im], axis=1))
        xr[:, cols] = last_r
        xi[:, cols] = last_i
        xc = jnp.concatenate(xs, axis=0).astype(BF16)
        y = jnp.dot(xc, cblk_ref[sg], preferred_element_type=F32) + d_ref[:, ch] * u
        z_ref[:, ch] = _gelu_tanh(y).astype(z_ref.dtype)


def _s5_scan(u, rate, theta, bblk, cblk, d_skip, bsz, seq):
    t, w = u.shape
    L = S5_CHUNK
    nc = seq // L
    ns = rate.shape[1]
    return pl.pallas_call(
        _s5_kernel,
        grid=(bsz, nc),
        in_specs=[pl.BlockSpec((L, w), lambda b, c: (b * nc + c, 0)),
                  pl.BlockSpec((1, ns), lambda b, c: (0, 0)),
                  pl.BlockSpec((1, ns), lambda b, c: (0, 0)),
                  pl.BlockSpec(bblk.shape, lambda b, c: (0, 0, 0)),
                  pl.BlockSpec(cblk.shape, lambda b, c: (0, 0, 0)),
                  pl.BlockSpec((1, w), lambda b, c: (0, 0))],
        out_specs=pl.BlockSpec((L, w), lambda b, c: (b * nc + c, 0)),
        out_shape=jax.ShapeDtypeStruct((t, w), BF16),
        scratch_shapes=[pltpu.VMEM((S5_SEG, ns), F32)] * 6 + [pltpu.VMEM((1, ns), F32)] * 2,
        compiler_params=_cparams(2, 48),
    )(u, rate, theta, bblk, cblk, d_skip.reshape(1, w))


def _s5_params(b_re, b_im, c_re, c_im, lam_re, lam_im, log_dt):
    g, p, ch = b_re.shape
    dt = jnp.exp(log_dt)[:, None]
    rate = lam_re * dt
    theta = lam_im * dt
    mag = jnp.exp(rate)
    a_re = mag * jnp.cos(theta)
    a_im = mag * jnp.sin(theta)
    den = lam_re * lam_re + lam_im * lam_im
    f_re = ((a_re - 1.0) * lam_re + a_im * lam_im) / den
    f_im = (a_im * lam_re - (a_re - 1.0) * lam_im) / den
    bb_re = f_re[..., None] * b_re - f_im[..., None] * b_im
    bb_im = f_re[..., None] * b_im + f_im[..., None] * b_re
    ns = S5_SUPER
    nsg = g // ns
    eye = jnp.eye(ns, dtype=F32)

    def blk_b(bb):
        return jnp.einsum('sgpc,gh->sgchp', bb.reshape(nsg, ns, p, ch), eye).reshape(nsg, ns * ch, ns * p)

    def blk_c(cc):
        return jnp.einsum('sgcp,gh->sgphc', cc.reshape(nsg, ns, ch, p), eye).reshape(nsg, ns * p, ns * ch)

    bblk = jnp.concatenate([blk_b(bb_re), blk_b(bb_im)], axis=2).astype(BF16)
    cblk = jnp.concatenate([blk_c(c_re), -blk_c(c_im)], axis=1).astype(BF16)
    return rate.reshape(1, g * p), theta.reshape(1, g * p), bblk, cblk


def _route(x, g, sc, sh, wr, br):
    h = _normmod(x, g, sc, sh).astype(BF16)
    logits = jnp.dot(h, wr, preferred_element_type=F32) + br
    lane = lax.broadcasted_iota(jnp.int32, logits.shape, 1)
    lane_f = lane.astype(F32)
    work = logits
    onehot = jnp.zeros(logits.shape, F32)
    ei = jnp.zeros(logits.shape, F32)
    vals = []
    for kk in range(TOP_K):
        m = jnp.max(work, axis=-1, keepdims=True)
        idx = jnp.min(jnp.where(work == m, lane_f, float(LANES)), axis=-1, keepdims=True)
        hit = lane_f == idx
        onehot = jnp.where(hit, 1.0, onehot)
        ei = jnp.where(lane == kk, idx, ei)
        vals.append(m)
        work = jnp.where(hit, 2.0 * NEG, work)
    ex = [jnp.exp(vv - vals[0]) for vv in vals]
    tot = ex[0] + ex[1] + ex[2] + ex[3]
    gt = jnp.zeros(logits.shape, F32)
    for kk in range(TOP_K):
        gt = jnp.where(lane == kk, ex[kk] / tot, gt)
    return _pack_rows(h), onehot, ei, gt


def _glu_out_kernel(z_ref, x_ref, g_ref, wg_ref, wo_ref, o_ref):
    z = z_ref[...]
    gate = jax.nn.sigmoid(jnp.dot(z, wg_ref[...], preferred_element_type=F32))
    m = jnp.dot((z.astype(F32) * gate).astype(BF16), wo_ref[...], preferred_element_type=F32)
    o_ref[...] = x_ref[...] + g_ref[0] * m


def _lin_res_kernel(a_ref, x_ref, g_ref, w_ref, o_ref):
    m = jnp.dot(a_ref[...].astype(BF16), w_ref[...], preferred_element_type=F32)
    o_ref[...] = x_ref[...] + g_ref[0] * m


def _row_call(kernel, acts, x2, gate, weights, seq):
    t, d = x2.shape
    tm = TM_DENSE
    tpb = seq // tm
    row = pl.BlockSpec((tm, d), lambda i: (i, 0))
    return pl.pallas_call(
        kernel,
        grid=(t // tm,),
        in_specs=[row, row, pl.BlockSpec((1, 1, d), lambda i: (i // tpb, 0, 0))]
                 + [pl.BlockSpec(w.shape, lambda i: (0, 0)) for w in weights],
        out_specs=row,
        out_shape=jax.ShapeDtypeStruct((t, d), F32),
        compiler_params=_cparams(1, 48),
    )(acts, x2, gate, *weights)


def _router_kernel(x_ref, g_ref, sc_ref, sh_ref, wr_ref, br_ref, h_ref, oh_ref, ei_ref, gt_ref):
    h_ref[...], oh_ref[...], ei_ref[...], gt_ref[...] = _route(
        x_ref[...], g_ref[...], sc_ref[0], sh_ref[0], wr_ref[...], br_ref[...])


def _router(x2, g, sc, sh, w_router, b_router, seq):
    t, d = x2.shape
    ne = w_router.shape[1]
    wr_pad = jnp.zeros((d, LANES), F32).at[:, :ne].set(w_router).astype(BF16)
    br_pad = jnp.full((1, LANES), NEG, F32).at[0, :ne].set(b_router)
    tm = TM_DENSE
    tpb = seq // tm
    lane_tile = pl.BlockSpec((tm, LANES), lambda i: (i, 0))
    per_batch = pl.BlockSpec((1, 1, d), lambda i: (i // tpb, 0, 0))
    return pl.pallas_call(
        _router_kernel,
        grid=(t // tm,),
        in_specs=[pl.BlockSpec((tm, d), lambda i: (i, 0)),
                  pl.BlockSpec((1, d), lambda i: (0, 0)), per_batch, per_batch,
                  pl.BlockSpec((d, LANES), lambda i: (0, 0)),
                  pl.BlockSpec((1, LANES), lambda i: (0, 0))],
        out_specs=[pl.BlockSpec((tm, d // 2), lambda i: (i, 0)), lane_tile, lane_tile, lane_tile],
        out_shape=[jax.ShapeDtypeStruct((t, d // 2), jnp.uint32)]
                  + [jax.ShapeDtypeStruct((t, LANES), F32)] * 3,
        compiler_params=_cparams(1, 32),
    )(x2, g.reshape(1, d), sc, sh, wr_pad, br_pad)


def _rope_tab_kernel(pos_ref, inv_ref, sgn_ref, cos_ref, sin_ref):
    ang = pos_ref[...].astype(F32) * inv_ref[...]
    cos_ref[...] = jnp.cos(ang)
    sin_ref[...] = jnp.sin(ang) * sgn_ref[...]


def _rope_tables(pos_col):
    t = pos_col.shape[0]
    ts = 512
    half = HEAD_DIM // 2
    inv = ROPE_THETA ** (-jnp.arange(0, HEAD_DIM, 2, dtype=F32) / HEAD_DIM)
    inv = jnp.concatenate([inv, inv]).reshape(1, HEAD_DIM)
    sgn = jnp.concatenate([-jnp.ones((half,), F32), jnp.ones((half,), F32)]).reshape(1, HEAD_DIM)
    const = pl.BlockSpec((1, HEAD_DIM), lambda i: (0, 0))
    tile = pl.BlockSpec((ts, HEAD_DIM), lambda i: (i, 0))
    return pl.pallas_call(
        _rope_tab_kernel,
        grid=(t // ts,),
        in_specs=[pl.BlockSpec((ts, 1), lambda i: (i, 0)), const, const],
        out_specs=[tile, tile],
        out_shape=[jax.ShapeDtypeStruct((t, HEAD_DIM), F32)] * 2,
        compiler_params=_cparams(1, 16),
    )(pos_col, inv, sgn)


def _rope_kernel(q_ref, k_ref, v_ref, cos_ref, sin_ref, qo_ref, ko_ref, vo_ref, km_ref):
    cs = cos_ref[...]
    sn = sin_ref[...]
    dh = cs.shape[1]
    half = dh // 2
    for hh in range(qo_ref.shape[1]):
        cols = slice(hh * dh, (hh + 1) * dh)
        q = q_ref[:, cols].astype(F32)
        k = k_ref[:, cols].astype(F32)
        qr = q * cs + pltpu.roll(q, half, 1) * sn
        kr = k * cs + pltpu.roll(k, half, 1) * sn
        qo_ref[0, hh] = qr
        ko_ref[0, hh] = kr.astype(BF16)
        vo_ref[0, hh, 0] = v_ref[:, cols].astype(F32).T.astype(BF16)
        km_ref[0, hh, 0] = jnp.mean(kr, axis=0, keepdims=True)


def _rope(qkv, cos, sin, bsz, seq):
    t = qkv.shape[0]
    h, dh, blk = N_HEADS, HEAD_DIM, MOBA_BLOCK
    nb = seq // blk
    part = lambda p: pl.BlockSpec((blk, h * dh), lambda i: (i, p))
    tab = pl.BlockSpec((blk, dh), lambda i: (i, 0))
    head = pl.BlockSpec((1, h, blk, dh), lambda i: (i // nb, 0, i % nb, 0))
    return pl.pallas_call(
        _rope_kernel,
        grid=(t // blk,),
        in_specs=[part(0), part(1), part(2), tab, tab],
        out_specs=[head, head,
                   pl.BlockSpec((1, h, 1, dh, blk), lambda i: (i // nb, 0, i % nb, 0, 0)),
                   pl.BlockSpec((1, h, 1, 1, dh), lambda i: (i // nb, 0, i % nb, 0, 0))],
        out_shape=[jax.ShapeDtypeStruct((bsz, h, seq, dh), F32),
                   jax.ShapeDtypeStruct((bsz, h, seq, dh), BF16),
                   jax.ShapeDtypeStruct((bsz, h, nb, dh, blk), BF16),
                   jax.ShapeDtypeStruct((bsz, h, nb, 1, dh), F32)],
        compiler_params=_cparams(1, 32),
    )(qkv, qkv, qkv, cos, sin)


def _moba_kernel(q_ref, k_ref, vt_ref, km_ref, o_ref, sel_scr, s_a, s_b, s_own):
    nh = q_ref.shape[1]
    blk = q_ref.shape[2]
    dh = q_ref.shape[3]
    j = pl.program_id(2)
    nt = (((1,), (1,)), ((), ()))
    heads = range(nh)

    qs = []
    for hh in heads:
        q = q_ref[0, hh]
        gate = lax.dot_general(km_ref[0, hh].astype(BF16), q.astype(BF16), nt,
                               preferred_element_type=F32)
        bidx = lax.broadcasted_iota(jnp.int32, gate.shape, 0)
        bidx_f = bidx.astype(F32)
        work = jnp.where(bidx < j, gate, NEG)
        sel = jnp.zeros(gate.shape, F32)
        for _ in range(MOBA_TOPK):
            m = jnp.max(work, axis=0, keepdims=True)
            idx = jnp.min(jnp.where(work == m, bidx_f, float(LANES)), axis=0, keepdims=True)
            hit = bidx_f == idx
            sel = jnp.where(hit & (m > 0.5 * NEG), 1.0, sel)
            work = jnp.where(hit, 2.0 * NEG, work)
        sel_scr[hh] = sel
        qs.append((q * (dh ** -0.5 * LOG2E)).astype(BF16))

    def scores(hh, n):
        start = pl.multiple_of(n * blk, blk)
        kb = k_ref[0, hh, pl.ds(start, blk), :]
        return lax.dot_general(kb, qs[hh], nt, preferred_element_type=F32)

    nb_last = vt_ref.shape[2] - 1

    def score_trip(scr, nn):
        for hh in heads:
            for bb in range(2):
                scr[hh, bb] = scores(hh, jnp.minimum(2 * nn + bb, nb_last))

    def past_pair(hh, nn, scr, m_i, l_i, acc):
        n0 = jnp.minimum(2 * nn, nb_last)
        n1 = jnp.minimum(2 * nn + 1, nb_last)
        s0 = scr[hh, 0]
        s1 = scr[hh, 1]
        r0 = (sel_scr[hh, pl.ds(n0, 1), :] > 0.5) & (2 * nn < j)
        r1 = (sel_scr[hh, pl.ds(n1, 1), :] > 0.5) & (2 * nn + 1 < j)
        rm0 = jnp.where(r0, jnp.max(s0, axis=0, keepdims=True), NEG)
        rm1 = jnp.where(r1, jnp.max(s1, axis=0, keepdims=True), NEG)
        m_new = jnp.maximum(m_i, jnp.maximum(rm0, rm1))
        alpha = jnp.exp2(m_i - m_new)
        p0 = jnp.exp2(s0 - jnp.where(r0, m_new, -NEG))
        p1 = jnp.exp2(s1 - jnp.where(r1, m_new, -NEG))
        l_new = alpha * l_i + (jnp.sum(p0, axis=0, keepdims=True) + jnp.sum(p1, axis=0, keepdims=True))
        pv = (jnp.dot(vt_ref[0, hh, n0], p0.astype(BF16), preferred_element_type=F32)
              + jnp.dot(vt_ref[0, hh, n1], p1.astype(BF16), preferred_element_type=F32))
        return m_new, l_new, alpha * acc + pv

    def body(mm, state):
        score_trip(s_b, 2 * mm + 1)
        state = tuple(past_pair(hh, 2 * mm, s_a, *state[hh]) for hh in heads)
        score_trip(s_a, 2 * mm + 2)
        return tuple(past_pair(hh, 2 * mm + 1, s_b, *state[hh]) for hh in heads)

    init = tuple((jnp.full((1, blk), NEG, F32), jnp.zeros((1, blk), F32), jnp.zeros((dh, blk), F32))
                 for _ in heads)
    score_trip(s_a, 0)
    for hh in heads:
        s_own[hh] = scores(hh, j)
    n_trips = (j + 1) // 2
    carry = lax.fori_loop(0, n_trips // 2, body, init)
    carry = lax.cond(n_trips % 2 == 1,
                     lambda st: tuple(past_pair(hh, n_trips - 1, s_a, *st[hh]) for hh in heads),
                     lambda st: st, carry)

    for hh in heads:
        m_i, l_i, acc = carry[hh]
        s = s_own[hh]
        kk = lax.broadcasted_iota(jnp.int32, s.shape, 0)
        qq = lax.broadcasted_iota(jnp.int32, s.shape, 1)
        s = jnp.where(kk <= qq, s, NEG)
        m_new = jnp.maximum(m_i, jnp.max(s, axis=0, keepdims=True))
        alpha = jnp.exp2(m_i - m_new)
        p = jnp.exp2(s - m_new)
        l_i = alpha * l_i + jnp.sum(p, axis=0, keepdims=True)
        acc = alpha * acc + jnp.dot(vt_ref[0, hh, j], p.astype(BF16), preferred_element_type=F32)
        o_ref[:, hh * dh:(hh + 1) * dh] = (acc / l_i).T.astype(o_ref.dtype)


def _moba(q, k, vt, km_pad, bsz, seq):
    h, dh, blk = N_HEADS, HEAD_DIM, MOBA_BLOCK
    nb = seq // blk
    nbp = km_pad.shape[2]
    nh = MOBA_HEADS_PER_STEP
    return pl.pallas_call(
        _moba_kernel,
        grid=(bsz, h // nh, nb),
        in_specs=[pl.BlockSpec((1, nh, blk, dh), lambda b, j, i: (b, j, i, 0)),
                  pl.BlockSpec((1, nh, seq, dh), lambda b, j, i: (b, j, 0, 0)),
                  pl.BlockSpec((1, nh, nb, dh, blk), lambda b, j, i: (b, j, 0, 0, 0)),
                  pl.BlockSpec((1, nh, nbp, dh), lambda b, j, i: (b, j, 0, 0))],
        out_specs=pl.BlockSpec((blk, nh * dh), lambda b, j, i: (b * nb + i, j)),
        out_shape=jax.ShapeDtypeStruct((bsz * seq, h * dh), BF16),
        scratch_shapes=[pltpu.VMEM((nh, nbp, blk), F32)] + [pltpu.VMEM((nh, 2, blk, blk), F32)] * 2
                       + [pltpu.VMEM((nh, blk, blk), F32)],
        compiler_params=_cparams(3, 48),
    )(q, k, vt, km_pad)


def _pos_kernel(oh_ref, ei_ref, dest_ref, meta_ref, cnt_scr, off_scr, run_scr):
    p = pl.program_id(0)
    i = pl.program_id(1)
    tm = oh_ref.shape[0]
    oh = oh_ref[...]
    colsum = jnp.sum(oh, axis=0, keepdims=True)

    @pl.when((p == 0) & (i == 0))
    def _():
        cnt_scr[...] = jnp.zeros_like(cnt_scr)

    @pl.when(p == 0)
    def _():
        cnt_scr[...] += colsum
        dest_ref[...] = jnp.zeros_like(dest_ref)

    @pl.when((p == 1) & (i == 0))
    def _():
        cnt = cnt_scr[...]
        padded = jnp.ceil(cnt * (1.0 / TM_EXPERT)) * float(TM_EXPERT)
        r = lax.broadcasted_iota(jnp.int32, (LANES, LANES), 0)
        cidx = lax.broadcasted_iota(jnp.int32, (LANES, LANES), 1)
        tri = (r < cidx).astype(F32)
        off = jnp.dot(jnp.broadcast_to(padded, (8, LANES)), tri,
                      preferred_element_type=F32, precision=HI)
        off_scr[...] = off[0:1]
        run_scr[...] = jnp.zeros_like(run_scr)

    @pl.when(p == 1)
    def _():
        r = lax.broadcasted_iota(jnp.int32, (tm, tm), 0)
        cidx = lax.broadcasted_iota(jnp.int32, (tm, tm), 1)
        ltri = (r > cidx).astype(BF16)
        excl = jnp.dot(ltri, oh.astype(BF16), preferred_element_type=F32)
        dfull = off_scr[...] + run_scr[...] + excl
        ei = ei_ref[...]
        lane = lax.broadcasted_iota(jnp.int32, oh.shape, 1)
        lane_f = lane.astype(F32)
        dest = jnp.zeros(oh.shape, F32)
        for kk in range(TOP_K):
            ek = jnp.sum(jnp.where(lane == kk, ei, 0.0), axis=-1, keepdims=True)
            dk = jnp.sum(jnp.where(lane_f == ek, dfull, 0.0), axis=-1, keepdims=True)
            dest = jnp.where(lane == kk, dk, dest)
        dest_ref[...] = dest.astype(jnp.int32)
        run_scr[...] += colsum

    meta_ref[...] = jnp.concatenate([off_scr[...], cnt_scr[...],
                                     jnp.zeros((6, LANES), F32)], axis=0)


def _positions(onehot, ei):
    t = onehot.shape[0]
    tm = TM_DENSE
    tile = pl.BlockSpec((tm, LANES), lambda p, i: (i, 0))
    return pl.pallas_call(
        _pos_kernel,
        grid=(2, t // tm),
        in_specs=[tile, tile],
        out_specs=[pl.BlockSpec((tm, LANES), lambda p, i: (i * p, 0)),
                   pl.BlockSpec((8, LANES), lambda p, i: (0, 0))],
        out_shape=[jax.ShapeDtypeStruct((t, LANES), jnp.int32),
                   jax.ShapeDtypeStruct((8, LANES), F32)],
        scratch_shapes=[pltpu.VMEM((1, LANES), F32)] * 3,
        compiler_params=_cparams(2, 16),
    )(onehot, ei)


def _src_kernel(dest_ref, pad_lo_ref, pad_hi_ref, src_ref):
    i = pl.program_id(0)
    tb = dest_ref.shape[0] // TOP_K

    @pl.when(i == 0)
    def _():
        def zero(j, carry):
            src_ref[j] = 0
            return carry

        def per_range(rr, carry):
            lax.fori_loop(pad_lo_ref[rr], pad_hi_ref[rr], zero, 0)
            return carry
        lax.fori_loop(0, pad_lo_ref.shape[0], per_range, 0)

    def body(tt, carry):
        for kk in range(TOP_K):
            src_ref[dest_ref[tt * TOP_K + kk]] = i * tb + tt
        return carry

    lax.fori_loop(0, tb, body, 0, unroll=8)


def _sources(dest_flat, pad_lo, pad_hi, n_rows):
    n = dest_flat.shape[0]
    tb = TM_SRC
    whole = pl.BlockSpec(memory_space=pltpu.SMEM)
    return pl.pallas_call(
        _src_kernel,
        grid=(n // (tb * TOP_K),),
        in_specs=[pl.BlockSpec((tb * TOP_K,), lambda i: (i,), memory_space=pltpu.SMEM), whole, whole],
        out_specs=pl.BlockSpec(memory_space=pltpu.SMEM),
        out_shape=jax.ShapeDtypeStruct((n_rows,), jnp.int32),
        compiler_params=_cparams(1, 16),
    )(dest_flat, pad_lo, pad_hi)


def _expert_kernel(te_ref, tv_ref, nx_ref, src_cur, src_nxt, h_hbm, wgu_hbm, bgu_ref, wd_hbm, bd_ref,
                   o_ref, xbuf, sem, wgu_st, wd_st, wsem, wgu_bf, wd_bf):
    i = pl.program_id(0)
    nt = pl.num_programs(0)
    tm = o_ref.shape[0]
    slot = lax.rem(i, 2)
    e = te_ref[i]
    prev = te_ref[jnp.maximum(i - 1, 0)]
    dff = wd_hbm.shape[1]

    def weight_copies(ex):
        return (pltpu.make_async_copy(wgu_hbm.at[ex], wgu_st, wsem.at[0]),
                pltpu.make_async_copy(wd_hbm.at[ex], wd_st, wsem.at[1]))

    def gather(src_ref, s):
        def body(r8, carry):
            for uu in range(8):
                pltpu.make_async_copy(h_hbm.at[pl.ds(src_ref[r8 * 8 + uu], 1)],
                                      xbuf.at[s, r8, pl.ds(uu, 1)], sem.at[s]).start()
            return carry
        lax.fori_loop(0, tm // 8, body, 0)

    @pl.when(i == 0)
    def _():
        for cp in weight_copies(e):
            cp.start(priority=1)

    @pl.when((i == 0) & (tv_ref[0] > 0))
    def _():
        gather(src_cur, 0)

    nxt = jnp.minimum(i + 1, nt - 1)

    @pl.when((i + 1 < nt) & (tv_ref[nxt] > 0))
    def _():
        gather(src_nxt, 1 - slot)

    @pl.when((i == 0) | (e != prev))
    def _():
        for cp in weight_copies(e):
            cp.wait()
        wgu_bf[...] = wgu_st[...].astype(BF16)
        wd_bf[...] = wd_st[...].astype(BF16)

        @pl.when(nx_ref[i] >= 0)
        def _():
            for cp in weight_copies(nx_ref[i]):
                cp.start(priority=1)

    @pl.when(tv_ref[i] > 0)
    def _():
        for r8 in range(tm // 8):
            pltpu.make_async_copy(h_hbm.at[pl.ds(0, 8)], xbuf.at[slot, r8], sem.at[slot]).wait()
        x = _unpack_rows(xbuf[slot].reshape(tm, xbuf.shape[3])).astype(BF16)
        hu = jnp.dot(x, wgu_bf[...], preferred_element_type=F32) + bgu_ref[0]
        g = jnp.minimum(hu[:, :dff], SWIGLU_LIMIT)
        up = jnp.clip(hu[:, dff:], -SWIGLU_LIMIT, SWIGLU_LIMIT)
        act = (up + 1.0) * g * jax.nn.sigmoid(SWIGLU_ALPHA * g)
        y = jnp.dot(act.astype(BF16), wd_bf[...], preferred_element_type=F32) + bd_ref[0]
        o_ref[...] = _pack_rows(y)

    @pl.when(tv_ref[i] == 0)
    def _():
        o_ref[...] = jnp.zeros_like(o_ref)


def _experts(tile_e, tile_v, next_e, src, h2p, w_gu, b_gu, w_d, b_d):
    n_rows = src.shape[0]
    dp = h2p.shape[1]
    ne, d, f2 = w_gu.shape
    dff = w_d.shape[1]
    tm = TM_EXPERT
    nt = n_rows // tm
    grid_spec = pltpu.PrefetchScalarGridSpec(
        num_scalar_prefetch=3,
        grid=(nt,),
        in_specs=[pl.BlockSpec((tm,), lambda i, te, tv, nx: (i,), memory_space=pltpu.SMEM),
                  pl.BlockSpec((tm,), lambda i, te, tv, nx: (jnp.minimum(i + 1, nt - 1),),
                               memory_space=pltpu.SMEM),
                  pl.BlockSpec(memory_space=pl.ANY),
                  pl.BlockSpec(memory_space=pl.ANY),
                  pl.BlockSpec((1, 1, f2), lambda i, te, tv, nx: (te[i], 0, 0)),
                  pl.BlockSpec(memory_space=pl.ANY),
                  pl.BlockSpec((1, 1, d), lambda i, te, tv, nx: (te[i], 0, 0))],
        out_specs=pl.BlockSpec((tm, dp), lambda i, te, tv, nx: (i, 0)),
        scratch_shapes=[pltpu.VMEM((2, tm // 8, 8, dp), jnp.uint32), pltpu.SemaphoreType.DMA((2,)),
                        pltpu.VMEM((d, f2), F32), pltpu.VMEM((dff, d), F32),
                        pltpu.SemaphoreType.DMA((2,)),
                        pltpu.VMEM((d, f2), BF16), pltpu.VMEM((dff, d), BF16)],
    )
    return pl.pallas_call(
        _expert_kernel,
        grid_spec=grid_spec,
        out_shape=jax.ShapeDtypeStruct((n_rows, dp), jnp.uint32),
        compiler_params=_cparams(1, 48),
    )(tile_e, tile_v, next_e, src, src, h2p, w_gu, b_gu.reshape(ne, 1, f2), w_d, b_d.reshape(ne, 1, d))


def _combine_kernel(dcur, dnxt, gt_ref, x_ref, g_ref, gf_ref, ys_hbm, o_ref, buf, sem, *, final):
    i = pl.program_id(0)
    nt = pl.num_programs(0)
    td = x_ref.shape[0]
    slot = lax.rem(i, 2)

    def gather(dref, s):
        def body(t8, carry):
            for uu in range(8):
                for kk in range(TOP_K):
                    pltpu.make_async_copy(ys_hbm.at[pl.ds(dref[(t8 * 8 + uu) * TOP_K + kk], 1)],
                                          buf.at[s, kk, t8, pl.ds(uu, 1)], sem.at[s]).start()
            return carry
        lax.fori_loop(0, td // 8, body, 0)

    @pl.when(i == 0)
    def _():
        gather(dcur, 0)

    @pl.when(i + 1 < nt)
    def _():
        gather(dnxt, 1 - slot)

    for kk in range(TOP_K):
        for t8 in range(td // 8):
            pltpu.make_async_copy(ys_hbm.at[pl.ds(0, 8)], buf.at[slot, kk, t8], sem.at[slot]).wait()
    gt = gt_ref[...]
    rows = lambda kk: _unpack_rows(buf[slot, kk].reshape(td, buf.shape[4]))
    y = gt[:, 0:1] * rows(0)
    for kk in range(1, TOP_K):
        y = y + gt[:, kk:kk + 1] * rows(kk)
    xn = x_ref[...] + g_ref[0] * y
    if final:
        ms = jnp.mean(xn * xn, axis=-1, keepdims=True)
        xn = (xn * lax.rsqrt(ms + EPS)) * gf_ref[...]
    o_ref[...] = xn


def _combine(dest_flat, gates, x2, gate_vec, gf, ys, seq, final):
    t, d = x2.shape
    td = TD_ROWS
    tpb = seq // td
    nt = t // td
    return pl.pallas_call(
        functools.partial(_combine_kernel, final=final),
        grid=(nt,),
        in_specs=[pl.BlockSpec((td * TOP_K,), lambda i: (i,), memory_space=pltpu.SMEM),
                  pl.BlockSpec((td * TOP_K,), lambda i: (jnp.minimum(i + 1, nt - 1),),
                               memory_space=pltpu.SMEM),
                  pl.BlockSpec((td, LANES), lambda i: (i, 0)),
                  pl.BlockSpec((td, d), lambda i: (i, 0)),
                  pl.BlockSpec((1, 1, d), lambda i: (i // tpb, 0, 0)),
                  pl.BlockSpec((1, d), lambda i: (0, 0)),
                  pl.BlockSpec(memory_space=pl.ANY)],
        out_specs=pl.BlockSpec((td, d), lambda i: (i, 0)),
        out_shape=jax.ShapeDtypeStruct((t, d), F32),
        scratch_shapes=[pltpu.VMEM((2, TOP_K, td // 8, 8, d // 2), jnp.uint32),
                        pltpu.SemaphoreType.DMA((2,))],
        compiler_params=_cparams(1, 32),
    )(dest_flat, dest_flat, gates, x2, gate_vec, gf.reshape(1, d), ys)


def _moe(x2, routed, gate_vec, w_gu, b_gu, w_d, b_d, gf, seq, final):
    t, d = x2.shape
    ne = w_gu.shape[0]
    h2, onehot, ei, gates = routed
    dest, meta = _positions(onehot, ei)
    dest_flat = dest[:, :TOP_K].reshape(t * TOP_K)

    tm = TM_EXPERT
    n_rows = t * TOP_K + ne * tm
    n_tiles = n_rows // tm
    off = meta[0, :ne].astype(jnp.int32)
    cnt = meta[1, :ne].astype(jnp.int32)
    ends = off + ((cnt + tm - 1) // tm) * tm
    starts = jnp.arange(n_tiles, dtype=jnp.int32) * tm
    valid = starts < ends[-1]
    te = jnp.sum((starts[:, None] >= ends[None, :]).astype(jnp.int32), axis=1)
    te_last = jnp.sum((ends[-1] - 1 >= ends).astype(jnp.int32))
    tile_e = jnp.minimum(jnp.where(valid, te, te_last), ne - 1)
    tile_v = valid.astype(jnp.int32)
    eidx = jnp.arange(ne, dtype=jnp.int32)
    later_used = (eidx[None, :] > eidx[:, None]) & (cnt[None, :] > 0)
    next_of = jnp.min(jnp.where(later_used, eidx[None, :], ne), axis=1)
    next_of = jnp.where(next_of == ne, -1, next_of)
    next_e = jnp.sum(jnp.where(tile_e[:, None] == eidx[None, :], next_of[None, :], 0), axis=1)

    pad_lo = jnp.concatenate([off + cnt, ends[-1:]]).astype(jnp.int32)
    pad_hi = jnp.concatenate([ends, jnp.full((1,), n_rows, jnp.int32)]).astype(jnp.int32)
    src = _sources(dest_flat, pad_lo, pad_hi, n_rows)
    ys = _experts(tile_e, tile_v, next_e.astype(jnp.int32), src, h2, w_gu, b_gu, w_d, b_d)
    return _combine(dest_flat, gates, x2, gate_vec, gf, ys, seq, final)


def kernel(x, c, positions, l0_norm1_g, l0_ada_w, l0_ada_b, l0_s5_w_in, l0_s5_b_re, l0_s5_b_im, l0_s5_c_re, l0_s5_c_im, l0_s5_lam_re, l0_s5_lam_im, l0_s5_log_dt, l0_s5_d, l0_s5_w_glu, l0_s5_w_out, l0_norm2_g, l0_moe_w_router, l0_moe_b_router, l0_moe_w_gate_up, l0_moe_b_gate_up, l0_moe_w_down, l0_moe_b_down, l1_norm1_g, l1_ada_w, l1_ada_b, l1_moba_w_qkv, l1_moba_w_o, l1_norm2_g, l1_moe_w_router, l1_moe_b_router, l1_moe_w_gate_up, l1_moe_b_gate_up, l1_moe_w_down, l1_moe_b_down, final_norm_g):
    bsz, seq, d = x.shape
    t = bsz * seq
    assert seq % TM_DENSE == 0 and seq % MOBA_BLOCK == 0 and seq % S5_CHUNK == 0
    assert S5_CHUNK % S5_SEG == 0 and seq % TD_ROWS == 0
    assert seq // MOBA_BLOCK <= LANES and d == N_HEADS * HEAD_DIM
    x2 = x.reshape(t, d)
    c_pad = jnp.zeros((8, d), F32).at[:bsz].set(c)

    def ada_parts(w, b):
        a = _ada(c_pad, w, b)[:bsz]
        return [a[:, i * d:(i + 1) * d].reshape(bsz, 1, d) for i in range(6)]

    sh1, sc1, g1, sh2, sc2, g2 = ada_parts(l0_ada_w, l0_ada_b)
    u = _nml(x2, l0_norm1_g, sc1, sh1, l0_s5_w_in.astype(BF16), seq, F32)
    rate, theta, bblk, cblk = _s5_params(l0_s5_b_re, l0_s5_b_im, l0_s5_c_re, l0_s5_c_im,
                                         l0_s5_lam_re, l0_s5_lam_im, l0_s5_log_dt)
    z = _s5_scan(u, rate, theta, bblk, cblk, l0_s5_d, bsz, seq)
    x2 = _row_call(_glu_out_kernel, z, x2, g1,
                   [l0_s5_w_glu.astype(BF16), l0_s5_w_out.astype(BF16)], seq)
    routed = _router(x2, l0_norm2_g, sc2, sh2, l0_moe_w_router, l0_moe_b_router, seq)
    x2 = _moe(x2, routed, g2, l0_moe_w_gate_up, l0_moe_b_gate_up, l0_moe_w_down, l0_moe_b_down,
              final_norm_g, seq, final=False)

    sh1, sc1, g1, sh2, sc2, g2 = ada_parts(l1_ada_w, l1_ada_b)
    qkv = _nml(x2, l1_norm1_g, sc1, sh1, l1_moba_w_qkv.astype(BF16), seq, BF16)
    cos, sin = _rope_tables(positions.reshape(t, 1))
    q, k, v, km = _rope(qkv, cos, sin, bsz, seq)
    nb = seq // MOBA_BLOCK
    nbp = -(-nb // 8) * 8
    km_pad = jnp.zeros((bsz, N_HEADS, nbp, HEAD_DIM), F32).at[:, :, :nb].set(km[:, :, :, 0])
    o = _moba(q, k, v, km_pad, bsz, seq)
    x2 = _row_call(_lin_res_kernel, o, x2, g1, [l1_moba_w_o.astype(BF16)], seq)
    routed = _router(x2, l1_norm2_g, sc2, sh2, l1_moe_w_router, l1_moe_b_router, seq)
    x2 = _moe(x2, routed, g2, l1_moe_w_gate_up, l1_moe_b_gate_up, l1_moe_w_down, l1_moe_b_down,
              final_norm_g, seq, final=True)
    return x2.reshape(bsz, seq, d)
```
